```python
import math
import jax
import jax.numpy as jnp
from jax import lax
import numpy as np

D_MODEL = 1024
BATCH = 8
SEQ = 2048
DEPTH = 2

CTX_LEN = 256
GRID_W = 64
EPS = 1e-6
N_MOD = 6

SSD_HEADS = 16
SSD_HEAD_DIM = 64
SSD_WIDTH = SSD_HEADS * SSD_HEAD_DIM
SSD_GROUPS = 4
SSD_STATE = 128
SSD_CONV = 3
SSD_CHUNK = 128
SSD_BC_WIDTH = SSD_GROUPS * SSD_STATE
SSD_CONV_CH = SSD_WIDTH + 2 * SSD_BC_WIDTH
SSD_DT_MIN = 1e-3
SSD_DT_MAX = 1e-1

ATT_HEADS = 8
ATT_HEAD_DIM = 64
ATT_V_DIM = 2 * ATT_HEAD_DIM
ATT_QK_WIDTH = ATT_HEADS * 2 * ATT_HEAD_DIM
ATT_WIDTH = ATT_HEADS * ATT_V_DIM
ATT_SCALE = ATT_HEAD_DIM ** -0.5
Q_BLOCK = 128
ROPE_BASE = 10000.0
ROPE_PAIRS = ATT_HEAD_DIM // 4

N_BRANCH = 2
D_FF = 2816
FFN_CONV = 3

IN_SIZES = (SSD_WIDTH, SSD_CONV_CH, 2 * SSD_HEADS, ATT_QK_WIDTH, ATT_QK_WIDTH, ATT_WIDTH, N_BRANCH * D_MODEL)
IN_WIDTH = SSD_WIDTH + SSD_CONV_CH + 2 * SSD_HEADS + 2 * ATT_QK_WIDTH + ATT_WIDTH + N_BRANCH * D_MODEL

kernel_name = 'hybrid_ssd_diffattn_dit_block'


def rmsnorm(x, w):
    xf = x.astype(jnp.float32)
    y = xf * lax.rsqrt(jnp.mean(xf * xf, axis=-1, keepdims=True) + EPS)
    return (y * w.astype(jnp.float32)).astype(x.dtype)


def modulate(h, shift, scale):
    return h * (1.0 + scale) + shift


def dwconv_centred(x, w, b):
    k_w = w.shape[0]
    pad = k_w // 2
    n = x.shape[1]
    xp = jnp.pad(x, ((0, 0), (pad, pad), (0, 0)))
    y = xp[:, 0:n] * w[0]
    for k in range(1, k_w):
        y = y + xp[:, k:k + n] * w[k]
    return y + b


def split_in(u):
    points = []
    acc = 0
    for s in IN_SIZES[:-1]:
        acc += s
        points.append(acc)
    return jnp.split(u, points, axis=-1)


def segsum(a):
    t = a.shape[-1]
    cs = jnp.cumsum(a, axis=-1)
    diff = cs[..., :, None] - cs[..., None, :]
    mask = jnp.tril(jnp.ones((t, t), dtype=bool))
    return jnp.where(mask, diff, -jnp.inf)


def ssd_scan(xs, dt, a_head, b_mat, c_mat, h0):
    bsz, n, n_heads, p = xs.shape
    g, s_dim = b_mat.shape[-2:]
    r = n_heads // g
    nc = n // SSD_CHUNK
    xd = (xs * dt[..., None]).reshape(bsz, nc, SSD_CHUNK, g, r, p)
    a = (dt * a_head).reshape(bsz, nc, SSD_CHUNK, g, r).transpose(0, 3, 4, 1, 2)
    a_cs = jnp.cumsum(a, axis=-1)
    bc = b_mat.reshape(bsz, nc, SSD_CHUNK, g, s_dim)
    cc = c_mat.reshape(bsz, nc, SSD_CHUNK, g, s_dim)
    cb = jnp.einsum('bclgn,bcsgn->bgcls', cc, bc)
    decay_in = jnp.exp(segsum(a))
    y_diag = jnp.einsum('bgcls,bgrcls,bcsgrp->bclgrp', cb, decay_in, xd)
    decay_to_end = jnp.exp(a_cs[..., -1:] - a_cs)
    states = jnp.einsum('bclgn,bgrcl,bclgrp->bcgrpn', bc, decay_to_end, xd)
    states = jnp.concatenate([h0.reshape(bsz, 1, g, r, p, s_dim).astype(states.dtype), states], axis=1)
    chunk_tot = jnp.pad(a_cs[..., -1], ((0, 0), (0, 0), (0, 0), (1, 0)))
    decay_chunk = jnp.exp(segsum(chunk_tot))
    states = jnp.einsum('bgrzc,bcgrpn->bzgrpn', decay_chunk, states)
    h_in, h_last = states[:, :-1], states[:, -1]
    y_off = jnp.einsum('bclgn,bcgrpn,bgrcl->bclgrp', cc, h_in, jnp.exp(a_cs))
    y = (y_diag + y_off).reshape(bsz, n, n_heads, p)
    return y, h_last.reshape(bsz, n_heads, p, s_dim)


def ssd_branch(xbc_l, dt_raw_l, z_l, xbc_c, dt_raw_c, z_c, conv_w, conv_b, a_log, dt_bias, d_skip, norm_w, with_ctx):
    a_dir = -jnp.exp(a_log.astype(jnp.float32))
    dt_b = dt_bias.astype(jnp.float32)

    def prep(xbc, dt_raw):
        bsz, n, _ = xbc.shape
        xbc = jax.nn.silu(dwconv_centred(xbc, conv_w, conv_b))
        xs, b_mat, c_mat = jnp.split(xbc, [SSD_WIDTH, SSD_WIDTH + SSD_BC_WIDTH], axis=-1)
        xs = xs.reshape(bsz, n, SSD_HEADS, SSD_HEAD_DIM)
        b_mat = b_mat.reshape(bsz, n, SSD_GROUPS, SSD_STATE)
        c_mat = c_mat.reshape(bsz, n, SSD_GROUPS, SSD_STATE)
        dt = jax.nn.softplus(dt_raw.astype(jnp.float32).reshape(bsz, n, 2, SSD_HEADS) + dt_b)
        return xs, b_mat, c_mat, dt

    def flip(t):
        return jnp.flip(t, axis=1)

    xs_c, b_c, c_c, dt_c = prep(xbc_c, dt_raw_c)
    xs_l, b_l, c_l, dt_l = prep(xbc_l, dt_raw_l)
    bsz = xs_l.shape[0]
    h0 = jnp.zeros((bsz, SSD_HEADS, SSD_HEAD_DIM, SSD_STATE), jnp.float32)
    yf_c, hf_c = ssd_scan(xs_c, dt_c[:, :, 0], a_dir[0], b_c, c_c, h0)
    yb_c, hb_c = ssd_scan(flip(xs_c), flip(dt_c[:, :, 1]), a_dir[1], flip(b_c), flip(c_c), h0)
    yf_l, _ = ssd_scan(xs_l, dt_l[:, :, 0], a_dir[0], b_l, c_l, hf_c)
    yb_l, _ = ssd_scan(flip(xs_l), flip(dt_l[:, :, 1]), a_dir[1], flip(b_l), flip(c_l), hb_c)

    def readout(xs, yf, yb_rev, z):
        bsz_, n, _, _ = xs.shape
        y = yf + flip(yb_rev) + d_skip.astype(jnp.float32)[:, None] * xs
        y = y.reshape(bsz_, n, SSD_WIDTH).astype(z.dtype) * jax.nn.silu(z)
        return rmsnorm(y, norm_w)

    y_l = readout(xs_l, yf_l, yb_l, z_l)
    y_c = readout(xs_c, yf_c, yb_c, z_c) if with_ctx else None
    return y_l, y_c


def rope_2d_tables(n_tokens):
    rows = n_tokens // GRID_W
    inv_freq = ROPE_BASE ** (-jnp.arange(ROPE_PAIRS, dtype=jnp.float32) / ROPE_PAIRS)
    row_pos = jnp.arange(rows, dtype=jnp.float32)
    col_pos = jnp.arange(GRID_W, dtype=jnp.float32)
    ang_r = jnp.broadcast_to(row_pos[:, None, None] * inv_freq, (rows, GRID_W, ROPE_PAIRS))
    ang_c = jnp.broadcast_to(col_pos[None, :, None] * inv_freq, (rows, GRID_W, ROPE_PAIRS))
    ang = jnp.stack([ang_r, ang_c], axis=2).reshape(n_tokens, 2, ROPE_PAIRS)
    return jnp.cos(ang), jnp.sin(ang)


def apply_rope_2d(x, cos, sin):
    xs = x.astype(jnp.float32).reshape(*x.shape[:-1], 2, 2, ROPE_PAIRS)
    x1, x2 = xs[..., 0, :], xs[..., 1, :]
    c = cos[None, :, None, None]
    s = sin[None, :, None, None]
    out = jnp.stack([x1 * c - x2 * s, x2 * c + x1 * s], axis=-2)
    return out.reshape(x.shape).astype(x.dtype)


def diff_softmax_attend(q, k, v, lam):
    s = jnp.einsum('bqhcd,bkhcd->bhcqk', q, k).astype(jnp.float32) * ATT_SCALE
    p = jax.nn.softmax(s, axis=-1)
    p_diff = p[:, :, 0] - lam * p[:, :, 1]
    return jnp.einsum('bhqk,bkhe->bqhe', p_diff.astype(v.dtype), v)


def diff_attn_branch(q_l, k_l, v_l, q_c, k_c, v_c, cos, sin, lambdas, subln_w, layer_idx, with_ctx):
    bsz, n, _ = q_l.shape
    n_ctx = q_c.shape[1]
    lam_init = 0.8 - 0.6 * math.exp(-0.3 * layer_idx)
    lf = lambdas.astype(jnp.float32)
    lam = jnp.exp(jnp.sum(lf[0] * lf[1])) - jnp.exp(jnp.sum(lf[2] * lf[3])) + lam_init
    qk_shape = (ATT_HEADS, 2, ATT_HEAD_DIM)
    q_l = apply_rope_2d(q_l.reshape(bsz, n, *qk_shape), cos, sin)
    k_l = apply_rope_2d(k_l.reshape(bsz, n, *qk_shape), cos, sin)
    v_l = v_l.reshape(bsz, n, ATT_HEADS, ATT_V_DIM)
    q_c = q_c.reshape(bsz, n_ctx, *qk_shape)
    k_c = k_c.reshape(bsz, n_ctx, *qk_shape)
    v_c = v_c.reshape(bsz, n_ctx, ATT_HEADS, ATT_V_DIM)
    k_all = jnp.concatenate([k_c, k_l], axis=1)
    v_all = jnp.concatenate([v_c, v_l], axis=1)
    nb = n // Q_BLOCK
    q_blocks = q_l.reshape(bsz, nb, Q_BLOCK, *qk_shape).swapaxes(0, 1)
    o_l = lax.map(lambda qb: diff_softmax_attend(qb, k_all, v_all, lam), q_blocks)
    o_l = o_l.swapaxes(0, 1).reshape(bsz, n, ATT_HEADS, ATT_V_DIM)

    def readout(o):
        return (rmsnorm(o, subln_w) * (1.0 - lam_init)).reshape(o.shape[0], o.shape[1], ATT_WIDTH)

    y_l = readout(o_l)
    y_c = readout(diff_softmax_attend(q_c, k_c, v_c, lam)) if with_ctx else None
    return y_l, y_c


def branch_merge(y_s, y_a, gates, w_br_s, w_br_a, w_o):
    g_s, g_a = jnp.split(jax.nn.sigmoid(gates), N_BRANCH, axis=-1)
    return (g_s * (y_s @ w_br_s) + g_a * (y_a @ w_br_a)) @ w_o


def conv_ffn(h, w_u, conv_w, conv_b, w_d):
    u = dwconv_centred(h @ w_u, conv_w, conv_b)
    a, v = jnp.split(u, 2, axis=-1)
    return (jax.nn.silu(a) * v) @ w_d


def setup_inputs(seed: int = 0) -> dict:
    key = jax.random.key(seed)
    ks = jax.random.split(key, 26)
    f32 = jnp.float32

    def nrm(k, shape, scale):
        return jax.random.normal(k, shape, f32) * scale

    x = nrm(ks[0], (BATCH, SEQ, D_MODEL), 1.0)
    c = nrm(ks[1], (BATCH, D_MODEL), 1.0)
    ctx = nrm(ks[2], (BATCH, CTX_LEN, D_MODEL), 1.0)
    c_ctx = nrm(ks[3], (D_MODEL,), 1.0)
    w_mod = nrm(ks[4], (DEPTH, D_MODEL, N_MOD * D_MODEL), 0.5 * D_MODEL ** -0.5)
    b_mod = nrm(ks[5], (DEPTH, N_MOD * D_MODEL), 0.02)
    norm1_w = 1.0 + nrm(ks[6], (DEPTH, D_MODEL), 0.02)
    w_in = nrm(ks[7], (DEPTH, D_MODEL, IN_WIDTH), D_MODEL ** -0.5)
    ssd_conv_w = nrm(ks[8], (DEPTH, SSD_CONV, SSD_CONV_CH), SSD_CONV ** -0.5)
    ssd_conv_b = nrm(ks[9], (DEPTH, SSD_CONV_CH), 0.02)
    ssd_a_log = jnp.log(jax.random.uniform(ks[10], (DEPTH, 2, SSD_HEADS), f32, 1.0, 16.0))
    u_dt = jax.random.uniform(ks[11], (DEPTH, 2, SSD_HEADS), f32)
    dt0 = jnp.exp(u_dt * (math.log(SSD_DT_MAX) - math.log(SSD_DT_MIN)) + math.log(SSD_DT_MIN))
    ssd_dt_bias = dt0 + jnp.log(-jnp.expm1(-dt0))
    ssd_d = 1.0 + nrm(ks[12], (DEPTH, SSD_HEADS), 0.1)
    ssd_norm_w = 1.0 + nrm(ks[13], (DEPTH, SSD_WIDTH), 0.02)
    diff_lambda = nrm(ks[14], (DEPTH, 4, ATT_HEAD_DIM), 0.1)
    att_subln_w = 1.0 + nrm(ks[15], (DEPTH, ATT_V_DIM), 0.02)
    w_br_ssd = nrm(ks[16], (DEPTH, SSD_WIDTH, D_MODEL), SSD_WIDTH ** -0.5)
    w_br_att = nrm(ks[17], (DEPTH, ATT_WIDTH, D_MODEL), ATT_WIDTH ** -0.5)
    w_out = nrm(ks[18], (DEPTH, D_MODEL, D_MODEL), D_MODEL ** -0.5)
    norm2_w = 1.0 + nrm(ks[19], (DEPTH, D_MODEL), 0.02)
    w_up = nrm(ks[20], (DEPTH, D_MODEL, 2 * D_FF), D_MODEL ** -0.5)
    ffn_conv_w = nrm(ks[21], (DEPTH, FFN_CONV, 2 * D_FF), FFN_CONV ** -0.5)
    ffn_conv_b = nrm(ks[22], (DEPTH, 2 * D_FF), 0.02)
    w_down = nrm(ks[23], (DEPTH, D_FF, D_MODEL), D_FF ** -0.5)
    final_norm_w = 1.0 + nrm(ks[24], (D_MODEL,), 0.02)
    return {'x': x, 'c': c, 'ctx': ctx, 'c_ctx': c_ctx, 'w_mod': w_mod, 'b_mod': b_mod,
            'norm1_w': norm1_w, 'w_in': w_in, 'ssd_conv_w': ssd_conv_w, 'ssd_conv_b': ssd_conv_b,
            'ssd_a_log': ssd_a_log, 'ssd_dt_bias': ssd_dt_bias, 'ssd_d': ssd_d, 'ssd_norm_w': ssd_norm_w,
            'diff_lambda': diff_lambda, 'att_subln_w': att_subln_w, 'w_br_ssd': w_br_ssd,
            'w_br_att': w_br_att, 'w_out': w_out, 'norm2_w': norm2_w, 'w_up': w_up,
            'ffn_conv_w': ffn_conv_w, 'ffn_conv_b': ffn_conv_b, 'w_down': w_down,
            'final_norm_w': final_norm_w}


def reference(x, c, ctx, c_ctx, w_mod, b_mod, norm1_w, w_in, ssd_conv_w, ssd_conv_b, ssd_a_log,
              ssd_dt_bias, ssd_d, ssd_norm_w, diff_lambda, att_subln_w, w_br_ssd, w_br_att, w_out,
              norm2_w, w_up, ffn_conv_w, ffn_conv_b, w_down, final_norm_w):
    n = x.shape[1]
    cos, sin = rope_2d_tables(n)
    xl, xc = x, ctx
    for i in range(DEPTH):
        with_ctx = i < DEPTH - 1
        mod_l = (jax.nn.silu(c) @ w_mod[i] + b_mod[i])[:, None, :]
        mod_c = (jax.nn.silu(c_ctx) @ w_mod[i] + b_mod[i])[None, None, :]
        sh1_l, sc1_l, g1_l, sh2_l, sc2_l, g2_l = jnp.split(mod_l, N_MOD, axis=-1)
        sh1_c, sc1_c, g1_c, sh2_c, sc2_c, g2_c = jnp.split(mod_c, N_MOD, axis=-1)
        u_l = modulate(rmsnorm(xl, norm1_w[i]), sh1_l, sc1_l) @ w_in[i]
        u_c = modulate(rmsnorm(xc, norm1_w[i]), sh1_c, sc1_c) @ w_in[i]
        z_l, xbc_l, dt_l, q_l, k_l, v_l, gate_l = split_in(u_l)
        z_c, xbc_c, dt_c, q_c, k_c, v_c, gate_c = split_in(u_c)
        ys_l, ys_c = ssd_branch(xbc_l, dt_l, z_l, xbc_c, dt_c, z_c, ssd_conv_w[i], ssd_conv_b[i],
                                ssd_a_log[i], ssd_dt_bias[i], ssd_d[i], ssd_norm_w[i], with_ctx)
        ya_l, ya_c = diff_attn_branch(q_l, k_l, v_l, q_c, k_c, v_c, cos, sin, diff_lambda[i],
                                      att_subln_w[i], i, with_ctx)
        xl = xl + g1_l * branch_merge(ys_l, ya_l, gate_l, w_br_ssd[i], w_br_att[i], w_out[i])
        h_l = modulate(rmsnorm(xl, norm2_w[i]), sh2_l, sc2_l)
        xl = xl + g2_l * conv_ffn(h_l, w_up[i], ffn_conv_w[i], ffn_conv_b[i], w_down[i])
        if with_ctx:
            xc = xc + g1_c * branch_merge(ys_c, ya_c, gate_c, w_br_ssd[i], w_br_att[i], w_out[i])
            h_c = modulate(rmsnorm(xc, norm2_w[i]), sh2_c, sc2_c)
            xc = xc + g2_c * conv_ffn(h_c, w_up[i], ffn_conv_w[i], ffn_conv_b[i], w_down[i])
    return rmsnorm(xl, final_norm_w)
```

```python
import functools
import math

import jax
import jax.numpy as jnp
import numpy as np
from jax import lax
from jax.experimental import pallas as pl
from jax.experimental.pallas import tpu as pltpu

F32 = jnp.float32
BF16 = jnp.bfloat16

D_MODEL = 1024
EPS = 1e-6
N_MOD = 6
GRID_W = 64

SSD_HEADS = 16
SSD_HEAD_DIM = 64
SSD_WIDTH = SSD_HEADS * SSD_HEAD_DIM
SSD_GROUPS = 4
SSD_HEADS_PER_GROUP = SSD_HEADS // SSD_GROUPS
SSD_STATE = 128
SSD_CHUNK = 128
SSD_BC_WIDTH = SSD_GROUPS * SSD_STATE
SSD_CONV_CH = SSD_WIDTH + 2 * SSD_BC_WIDTH
GROUP_X = SSD_HEADS_PER_GROUP * SSD_HEAD_DIM

ATT_HEADS = 8
ATT_HEAD_DIM = 64
ATT_V_DIM = 2 * ATT_HEAD_DIM
ATT_QK_WIDTH = ATT_HEADS * 2 * ATT_HEAD_DIM
ATT_WIDTH = ATT_HEADS * ATT_V_DIM
ATT_SCALE = ATT_HEAD_DIM ** -0.5
ROPE_BASE = 10000.0
ROPE_PAIRS = ATT_HEAD_DIM // 4

D_FF = 2816
N_DT = 2 * SSD_HEADS

COL_Z = 0
COL_Q = COL_Z + SSD_WIDTH
COL_GS = COL_Q + ATT_QK_WIDTH
COL_GA = COL_GS + D_MODEL
COL_XS = COL_GA + D_MODEL
COL_B = COL_XS + SSD_WIDTH
COL_C = COL_B + SSD_BC_WIDTH
COL_K = COL_C + SSD_BC_WIDTH
COL_V = COL_K + ATT_QK_WIDTH
N_COLS = COL_V + ATT_WIDTH

LANES = 128
BF16_ROWS = 16
VMEM_LIMIT = 56 * 1024 * 1024


def _sigmoid(x):
    return 1.0 / (1.0 + jnp.exp(-x))


def _silu(x):
    return x * _sigmoid(x)


def _softplus(x):
    return jnp.maximum(x, 0.0) + jnp.log1p(jnp.exp(-jnp.abs(x)))


def _bdot(a, b):
    return jnp.dot(a.astype(BF16), b.astype(BF16), preferred_element_type=F32)


def _bdot_nt(a, b):
    return lax.dot_general(a.astype(BF16), b.astype(BF16), (((1,), (1,)), ((), ())),
                           preferred_element_type=F32)


def _split3(x):
    hi = x.astype(BF16)
    r1 = x - hi.astype(F32)
    mid = r1.astype(BF16)
    lo = (r1 - mid.astype(F32)).astype(BF16)
    return hi, mid, lo


def _dot3_rhs(a_bf16, x):
    hi, mid, lo = _split3(x)
    out = jnp.dot(a_bf16, hi, preferred_element_type=F32)
    out = out + jnp.dot(a_bf16, mid, preferred_element_type=F32)
    return out + jnp.dot(a_bf16, lo, preferred_element_type=F32)


def _dot3_lhs(x, b_bf16):
    hi, mid, lo = _split3(x)
    out = jnp.dot(hi, b_bf16, preferred_element_type=F32)
    out = out + jnp.dot(mid, b_bf16, preferred_element_type=F32)
    return out + jnp.dot(lo, b_bf16, preferred_element_type=F32)


def _cparams(sem):
    return pltpu.CompilerParams(dimension_semantics=sem, vmem_limit_bytes=VMEM_LIMIT)


def _mod_kernel(c_ref, w_ref, b_ref, o_ref):
    a = _silu(c_ref[...])
    o_ref[...] = _dot3_both(a, w_ref[...]) + b_ref[...]


def _dot3_both(a, w):
    ah, am, al = _split3(a)
    wh, wm, wl = _split3(w)
    d = lambda p, q: jnp.dot(p, q, preferred_element_type=F32)
    return (d(ah, wh) + (d(ah, wm) + d(am, wh))
            + (d(ah, wl) + d(am, wm) + d(al, wh)))


def _modulation(cc, w_mod, b_mod):
    depth = w_mod.shape[0]
    tn = 1024
    return pl.pallas_call(
        _mod_kernel,
        grid=(depth, N_MOD * D_MODEL // tn),
        in_specs=[
            pl.BlockSpec((16, D_MODEL), lambda l, j: (0, 0)),
            pl.BlockSpec((None, D_MODEL, tn), lambda l, j: (l, 0, j)),
            pl.BlockSpec((None, 1, tn), lambda l, j: (l, 0, j)),
        ],
        out_specs=pl.BlockSpec((None, 16, tn), lambda l, j: (l, 0, j)),
        out_shape=jax.ShapeDtypeStruct((depth, 16, N_MOD * D_MODEL), F32),
        compiler_params=_cparams(("parallel", "parallel")),
        name="modulation",
    )(cc, w_mod, b_mod.reshape(depth, 1, N_MOD * D_MODEL))


def _in_proj_kernel(x_ref, sh_ref, sc_ref, nw_ref, w_ref, wdt_ref, wdtt_ref,
                    u_ref, dt_ref, dtt_ref, h_ref):
    @pl.when(pl.program_id(1) == 0)
    def _():
        x = x_ref[...]
        r = lax.rsqrt(jnp.mean(x * x, axis=-1, keepdims=True) + EPS)
        h = (x * r * nw_ref[...]) * (1.0 + sc_ref[0]) + sh_ref[0]
        hb = h.astype(BF16)
        h_ref[...] = hb
        dt_ref[...] = jnp.dot(hb, wdt_ref[...], preferred_element_type=F32)
        dtt_ref[...] = lax.dot_general(wdtt_ref[...], hb, (((1,), (1,)), ((), ())),
                                       preferred_element_type=F32)

    u_ref[...] = jnp.dot(h_ref[...], w_ref[...], preferred_element_type=F32).astype(BF16)


def _in_proj(x2d, mod3, mod_row, norm_w, w_main, w_dt, w_dtt, *, tm, tn, col0):
    rows = x2d.shape[0]
    n_out = w_main.shape[1] - col0
    cb0 = col0 // tn
    return pl.pallas_call(
        _in_proj_kernel,
        grid=(rows // tm, n_out // tn),
        in_specs=[
            pl.BlockSpec((tm, D_MODEL), lambda i, j: (i, 0)),
            pl.BlockSpec((1, 1, D_MODEL), lambda i, j: (mod_row(i), 0, 0)),
            pl.BlockSpec((1, 1, D_MODEL), lambda i, j: (mod_row(i), 0, 1)),
            pl.BlockSpec((1, D_MODEL), lambda i, j: (0, 0)),
            pl.BlockSpec((D_MODEL, tn), lambda i, j: (0, j + cb0)),
            pl.BlockSpec((D_MODEL, LANES), lambda i, j: (0, 0)),
            pl.BlockSpec((N_DT, D_MODEL), lambda i, j: (0, 0)),
        ],
        out_specs=[
            pl.BlockSpec((tm, tn), lambda i, j: (i, j)),
            pl.BlockSpec((tm, LANES), lambda i, j: (i, 0)),
            pl.BlockSpec((N_DT, tm), lambda i, j: (0, i)),
        ],
        out_shape=[
            jax.ShapeDtypeStruct((rows, n_out), BF16),
            jax.ShapeDtypeStruct((rows, LANES), F32),
            jax.ShapeDtypeStruct((N_DT, rows), F32),
        ],
        scratch_shapes=[pltpu.VMEM((tm, D_MODEL), BF16)],
        compiler_params=_cparams(("parallel", "arbitrary")),
        name="in_proj",
    )(x2d, mod3, mod3, norm_w, w_main, w_dt, w_dtt)


def _ssd_kernel(xl_ref, bl_ref, cl_ref, xc_ref, bc_ref, cc_ref,
                dtl_ref, dtc_ref, dttl_ref, dttc_ref,
                cwx_ref, cwb_ref, cwc_ref, cbx_ref, cbb_ref, cbc_ref,
                biasr_ref, biasc_ref, alogx_ref, alogc_ref, dskip_ref,
                yl_ref, yc_ref,
                cm_s, y_s, ex_s, st_s, dec_s, hin_s, *, n_lat, n_ctx):
    t = SSD_CHUNK
    g = pl.program_id(1)
    nc_ctx = n_ctx // t
    nc_lat = n_lat // t
    n_chunks = nc_ctx + nc_lat
    gx = GROUP_X

    ri = lax.broadcasted_iota(jnp.int32, (t, t), 0)
    ci = lax.broadcasted_iota(jnp.int32, (t, t), 1)
    lower = ri >= ci
    tril = lower.astype(BF16)
    triu = (ri <= ci).astype(BF16)
    er = lax.broadcasted_iota(jnp.int32, (LANES, 2 * gx), 0)
    ec = lax.broadcasted_iota(jnp.int32, (LANES, 2 * gx), 1)
    expand = (er == g * (2 * SSD_HEADS_PER_GROUP) + ec // SSD_HEAD_DIM).astype(BF16)
    tt = jnp.concatenate([triu, tril], axis=1)
    a_x = -jnp.exp(alogx_ref[0])
    a_c = -jnp.exp(alogc_ref[...])
    row_id = lax.broadcasted_iota(jnp.int32, (t, 1), 0)

    def conv_silu(ref, c, n_seg_chunks, w_ref, b_ref):
        seg_rows = n_seg_chunks * t
        r0 = pl.multiple_of(c * t, t)
        cur = ref[0, pl.ds(r0, t), :].astype(F32)
        lo = pl.multiple_of(jnp.maximum(r0 - BF16_ROWS, 0), BF16_ROWS)
        hi = pl.multiple_of(jnp.minimum(r0 + t, seg_rows - BF16_ROWS), BF16_ROWS)
        prev_row = ref[0, pl.ds(lo, BF16_ROWS), :].astype(F32)[BF16_ROWS - 1:BF16_ROWS]
        next_row = ref[0, pl.ds(hi, BF16_ROWS), :].astype(F32)[0:1]
        prev_row = jnp.where(c > 0, prev_row, 0.0)
        next_row = jnp.where(c < n_seg_chunks - 1, next_row, 0.0)
        up = jnp.where(row_id == 0, prev_row, pltpu.roll(cur, 1, axis=0))
        dn = jnp.where(row_id == t - 1, next_row, pltpu.roll(cur, t - 1, axis=0))
        w = w_ref[...]
        return _silu(up * w[0:1] + cur * w[1:2] + dn * w[2:3] + b_ref[...])

    def phase_a(c, x_ref, b_ref, c_ref, dt_ref, dtt_ref, n_seg_chunks, chunk0):
        r0 = pl.multiple_of(c * t, t)
        o0 = pl.multiple_of((c + chunk0) * t, t)
        xs = conv_silu(x_ref, c, n_seg_chunks, cwx_ref, cbx_ref)
        bm = conv_silu(b_ref, c, n_seg_chunks, cwb_ref, cbb_ref).astype(BF16)
        cm = conv_silu(c_ref, c, n_seg_chunks, cwc_ref, cbc_ref).astype(BF16)
        cm_s[pl.ds(o0, t), :] = cm
        dt = _softplus(dt_ref[0, pl.ds(r0, t), :] + biasr_ref[...])
        dtx = _dot3_lhs(dt, expand)
        ax = dtx * a_x
        csf = _dot3_rhs(tril, ax[:, :gx])
        csb = _dot3_rhs(triu, ax[:, gx:])
        at = _softplus(dtt_ref[:, pl.ds(r0, t)] + biasc_ref[...]) * a_c
        hi, mid, lo = _split3(at)
        ct3 = jnp.dot(jnp.concatenate([hi, mid, lo], axis=0), tt, preferred_element_type=F32)
        n8 = at.shape[0]
        cst = ct3[0:n8] + ct3[n8:2 * n8] + ct3[2 * n8:3 * n8]
        cb = _bdot_nt(cm, bm)
        y = dskip_ref[0] * xs
        for d, cs in ((0, csf), (1, csb)):
            mask = lower if d == 0 else (ri <= ci)
            parts = []
            for r in range(SSD_HEADS_PER_GROUP):
                col = cs[:, r * SSD_HEAD_DIM:r * SSD_HEAD_DIM + 1]
                k = d * SSD_HEADS_PER_GROUP + r
                row = cst[k:k + 1, d * t:(d + 1) * t]
                seg = jnp.exp(jnp.where(mask, col - row, -jnp.inf))
                m = (cb * seg).astype(BF16)
                sl = slice(r * SSD_HEAD_DIM, (r + 1) * SSD_HEAD_DIM)
                xd = (xs[:, sl] * dtx[:, d * gx + r * SSD_HEAD_DIM:d * gx + (r + 1) * SSD_HEAD_DIM])
                parts.append(jnp.dot(m, xd.astype(BF16), preferred_element_type=F32))
            y = y + jnp.concatenate(parts, axis=1)
        y_s[pl.ds(o0, t), :] = y
        endf = csf[t - 1:t, :]
        endb = csb[0:1, :]
        ex_s[pl.ds(o0, t), :] = jnp.exp(jnp.concatenate([csf, csb], axis=1))
        xdd = jnp.concatenate([xs * dtx[:, :gx] * jnp.exp(endf - csf),
                               xs * dtx[:, gx:] * jnp.exp(endb - csb)], axis=1).astype(BF16)
        bt = jnp.transpose(bm.astype(F32)).astype(BF16)
        st_s[c + chunk0] = jnp.dot(bt, xdd, preferred_element_type=F32)
        dec_s[pl.ds(c + chunk0, 1), :] = jnp.exp(jnp.concatenate([endf, endb], axis=1))

    def ctx_a(c, carry):
        phase_a(c, xc_ref, bc_ref, cc_ref, dtc_ref, dttc_ref, nc_ctx, 0)
        return carry

    def lat_a(c, carry):
        phase_a(c, xl_ref, bl_ref, cl_ref, dtl_ref, dttl_ref, nc_lat, nc_ctx)
        return carry

    lax.fori_loop(0, nc_ctx, ctx_a, 0)
    lax.fori_loop(0, nc_lat, lat_a, 0)

    order_f = list(range(n_chunks))
    order_b = list(range(nc_ctx - 1, -1, -1)) + list(range(n_chunks - 1, nc_ctx - 1, -1))
    for d, order in ((0, order_f), (1, order_b)):
        h = jnp.zeros((SSD_STATE, gx), F32)
        for c in order:
            hin_s[c, :, d * gx:(d + 1) * gx] = h.astype(BF16)
            h = dec_s[c:c + 1, d * gx:(d + 1) * gx] * h + st_s[c, :, d * gx:(d + 1) * gx]

    def phase_c(c, carry):
        o0 = pl.multiple_of(c * t, t)
        yo = jnp.dot(cm_s[pl.ds(o0, t), :], hin_s[c], preferred_element_type=F32) * ex_s[pl.ds(o0, t), :]
        y_s[pl.ds(o0, t), :] = y_s[pl.ds(o0, t), :] + yo[:, :gx] + yo[:, gx:]
        return carry

    lax.fori_loop(0, n_chunks, phase_c, 0)
    yc_ref[0] = y_s[0:n_ctx, :].astype(BF16)
    yl_ref[0] = y_s[n_ctx:n_ctx + n_lat, :].astype(BF16)


def _ssd(ul3, uc3, cols_l, cols_c, dt_l, dt_c, dtt_l, dtt_c, p):
    bsz, n_lat, _ = ul3.shape
    n_ctx = uc3.shape[1]
    gx = GROUP_X
    n_chunks = (n_lat + n_ctx) // SSD_CHUNK
    n_tot = n_lat + n_ctx

    def seq_specs(cols, n):
        xb, bb, cb = cols["xs"] // gx, cols["b"] // SSD_STATE, cols["c"] // SSD_STATE
        return [
            pl.BlockSpec((1, n, gx), lambda b, g: (b, 0, xb + g)),
            pl.BlockSpec((1, n, SSD_STATE), lambda b, g: (b, 0, bb + g)),
            pl.BlockSpec((1, n, SSD_STATE), lambda b, g: (b, 0, cb + g)),
        ]

    xoff = 0
    boff = SSD_WIDTH // SSD_STATE
    coff = (SSD_WIDTH + SSD_BC_WIDTH) // SSD_STATE
    n8 = 2 * SSD_HEADS_PER_GROUP
    in_specs = (
        seq_specs(cols_l, n_lat) + seq_specs(cols_c, n_ctx) + [
            pl.BlockSpec((1, n_lat, LANES), lambda b, g: (b, 0, 0)),
            pl.BlockSpec((1, n_ctx, LANES), lambda b, g: (b, 0, 0)),
            pl.BlockSpec((n8, n_lat), lambda b, g: (g, b)),
            pl.BlockSpec((n8, n_ctx), lambda b, g: (g, b)),
            pl.BlockSpec((3, gx), lambda b, g: (0, xoff + g)),
            pl.BlockSpec((3, SSD_STATE), lambda b, g: (0, boff + g)),
            pl.BlockSpec((3, SSD_STATE), lambda b, g: (0, coff + g)),
            pl.BlockSpec((1, gx), lambda b, g: (0, xoff + g)),
            pl.BlockSpec((1, SSD_STATE), lambda b, g: (0, boff + g)),
            pl.BlockSpec((1, SSD_STATE), lambda b, g: (0, coff + g)),
            pl.BlockSpec((1, LANES), lambda b, g: (0, 0)),
            pl.BlockSpec((n8, 1), lambda b, g: (g, 0)),
            pl.BlockSpec((1, 1, 2 * gx), lambda b, g: (g, 0, 0)),
            pl.BlockSpec((n8, 1), lambda b, g: (g, 0)),
            pl.BlockSpec((1, 1, gx), lambda b, g: (g, 0, 0)),
        ])
    return pl.pallas_call(
        functools.partial(_ssd_kernel, n_lat=n_lat, n_ctx=n_ctx),
        grid=(bsz, SSD_GROUPS),
        in_specs=in_specs,
        out_specs=[
            pl.BlockSpec((1, n_lat, gx), lambda b, g: (b, 0, g)),
            pl.BlockSpec((1, n_ctx, gx), lambda b, g: (b, 0, g)),
        ],
        out_shape=[
            jax.ShapeDtypeStruct((bsz, n_lat, SSD_WIDTH), BF16),
            jax.ShapeDtypeStruct((bsz, n_ctx, SSD_WIDTH), BF16),
        ],
        scratch_shapes=[
            pltpu.VMEM((n_tot, SSD_STATE), BF16),
            pltpu.VMEM((n_tot, gx), F32),
            pltpu.VMEM((n_tot, 2 * gx), F32),
            pltpu.VMEM((n_chunks, SSD_STATE, 2 * gx), F32),
            pltpu.VMEM((n_chunks + (-n_chunks) % 8, 2 * gx), F32),
            pltpu.VMEM((n_chunks, SSD_STATE, 2 * gx), BF16),
        ],
        compiler_params=_cparams(("parallel", "parallel")),
        name="ssd_scan",
    )(ul3, ul3, ul3, uc3, uc3, uc3,
      dt_l.reshape(bsz, n_lat, LANES), dt_c.reshape(bsz, n_ctx, LANES), dtt_l, dtt_c,
      p["conv_w"], p["conv_w"], p["conv_w"], p["conv_b"], p["conv_b"], p["conv_b"],
      p["bias_row"], p["bias_col"], p["alog_x"], p["alog_col"], p["dskip_x"])


def _rope(x, cos, sin_a, sin_b):
    return x * cos + pltpu.roll(x, LANES - ROPE_PAIRS, axis=1) * sin_a + pltpu.roll(x, ROPE_PAIRS, axis=1) * sin_b


def _attn_kernel(*refs, rope, n_ctx, n_lat, tq, lam_init):
    if n_lat:
        (q_ref, kc_ref, vc_ref, kl_ref, vl_ref, cos_ref, sa_ref, sb_ref, lam_ref, sw_ref,
         o_ref, k0_s, k1_s, v_s) = refs
    else:
        (q_ref, kc_ref, vc_ref, lam_ref, sw_ref, o_ref, k0_s, k1_s, v_s) = refs
    qi = pl.program_id(2)
    hd = ATT_HEAD_DIM

    @pl.when(qi == 0)
    def _():
        kc = kc_ref[0]
        k0_s[0:n_ctx, :] = kc[:, :hd]
        k1_s[0:n_ctx, :] = kc[:, hd:]
        v_s[0:n_ctx, :] = vc_ref[0]
        if n_lat:
            kl = kl_ref[0].astype(F32)
            if rope:
                kl = _rope(kl, cos_ref[...], sa_ref[...], sb_ref[...])
            kl = kl.astype(BF16)
            k0_s[n_ctx:n_ctx + n_lat, :] = kl[:, :hd]
            k1_s[n_ctx:n_ctx + n_lat, :] = kl[:, hd:]
            v_s[n_ctx:n_ctx + n_lat, :] = vl_ref[0]

    lf = lam_ref[...]
    lam = (jnp.exp(jnp.sum(lf[0:1] * lf[1:2], axis=-1, keepdims=True))
           - jnp.exp(jnp.sum(lf[2:3] * lf[3:4], axis=-1, keepdims=True)) + lam_init)

    q = q_ref[0].astype(F32)
    if rope:
        r0 = pl.multiple_of(qi * tq, tq)
        q = _rope(q, cos_ref[pl.ds(r0, tq), :], sa_ref[pl.ds(r0, tq), :], sb_ref[pl.ds(r0, tq), :])
    q = (q * ATT_SCALE).astype(BF16)

    def probs(qc, k_s):
        s = lax.dot_general(qc, k_s[...], (((1,), (1,)), ((), ())), preferred_element_type=F32)
        e = jnp.exp(s - jnp.max(s, axis=-1, keepdims=True))
        return e, 1.0 / jnp.sum(e, axis=-1, keepdims=True)

    e0, r0_ = probs(q[:, :hd], k0_s)
    e1, r1_ = probs(q[:, hd:], k1_s)
    pd = (e0 * r0_ - e1 * (lam * r1_)).astype(BF16)
    o = jnp.dot(pd, v_s[...], preferred_element_type=F32)
    on = o * lax.rsqrt(jnp.mean(o * o, axis=-1, keepdims=True) + EPS) * sw_ref[...]
    o_ref[0] = (on * (1.0 - lam_init)).astype(BF16)


def _attention(uq3, col_q, uc3, cols_c, ul3, cols_l, rope_tabs, lam_p, subln_w, *, lam_init, tq):
    bsz, n_q, _ = uq3.shape
    n_ctx = uc3.shape[1]
    n_lat = 0 if ul3 is None else ul3.shape[1]
    n_k = n_ctx + n_lat
    vd = ATT_V_DIM
    qb = col_q // vd
    in_specs = [
        pl.BlockSpec((1, tq, vd), lambda b, h, i: (b, i, qb + h)),
        pl.BlockSpec((1, n_ctx, vd), lambda b, h, i: (b, 0, cols_c["k"] // vd + h)),
        pl.BlockSpec((1, n_ctx, vd), lambda b, h, i: (b, 0, cols_c["v"] // vd + h)),
    ]
    args = [uq3, uc3, uc3]
    if n_lat:
        in_specs += [
            pl.BlockSpec((1, n_lat, vd), lambda b, h, i: (b, 0, cols_l["k"] // vd + h)),
            pl.BlockSpec((1, n_lat, vd), lambda b, h, i: (b, 0, cols_l["v"] // vd + h)),
            pl.BlockSpec((n_lat, vd), lambda b, h, i: (0, 0)),
            pl.BlockSpec((n_lat, vd), lambda b, h, i: (0, 0)),
            pl.BlockSpec((n_lat, vd), lambda b, h, i: (0, 0)),
        ]
        args += [ul3, ul3, *rope_tabs]
    in_specs += [
        pl.BlockSpec((4, ATT_HEAD_DIM), lambda b, h, i: (0, 0)),
        pl.BlockSpec((1, vd), lambda b, h, i: (0, 0)),
    ]
    args += [lam_p, subln_w]
    return pl.pallas_call(
        functools.partial(_attn_kernel, rope=bool(n_lat), n_ctx=n_ctx, n_lat=n_lat, tq=tq,
                          lam_init=lam_init),
        grid=(bsz, ATT_HEADS, n_q // tq),
        in_specs=in_specs,
        out_specs=pl.BlockSpec((1, tq, vd), lambda b, h, i: (b, i, h)),
        out_shape=jax.ShapeDtypeStruct((bsz, n_q, ATT_WIDTH), BF16),
        scratch_shapes=[
            pltpu.VMEM((n_k, ATT_HEAD_DIM), BF16),
            pltpu.VMEM((n_k, ATT_HEAD_DIM), BF16),
            pltpu.VMEM((n_k, vd), BF16),
        ],
        compiler_params=_cparams(("parallel", "parallel", "arbitrary")),
        name="diff_attn",
    )(*args)


def _merge_kernel(ys_ref, z_ref, ya_ref, gs_ref, ga_ref, x_ref, g1_ref, nw_ref,
                  wbs_ref, wba_ref, wo_ref, o_ref):
    yz = ys_ref[...].astype(F32) * _silu(z_ref[...].astype(F32))
    ysn = yz * lax.rsqrt(jnp.mean(yz * yz, axis=-1, keepdims=True) + EPS) * nw_ref[...]
    ts = jnp.dot(ysn.astype(BF16), wbs_ref[...], preferred_element_type=F32)
    ta = jnp.dot(ya_ref[...], wba_ref[...], preferred_element_type=F32)
    tmix = _sigmoid(gs_ref[...].astype(F32)) * ts + _sigmoid(ga_ref[...].astype(F32)) * ta
    o_ref[...] = x_ref[...] + g1_ref[0] * jnp.dot(tmix.astype(BF16), wo_ref[...],
                                                   preferred_element_type=F32)


def _merge(ys2, u2, cols, ya2, x2d, mod3, mod_row, norm_w, w_bs, w_ba, w_o, *, tm):
    rows = x2d.shape[0]
    d = D_MODEL
    row_blk = lambda cb: pl.BlockSpec((tm, d), lambda i: (i, cb))
    full = lambda shape: pl.BlockSpec(shape, lambda i: (0, 0))
    return pl.pallas_call(
        _merge_kernel,
        grid=(rows // tm,),
        in_specs=[
            row_blk(0), row_blk(cols["z"] // d), row_blk(0),
            row_blk(cols["gs"] // d), row_blk(cols["ga"] // d), row_blk(0),
            pl.BlockSpec((1, 1, d), lambda i: (mod_row(i), 0, 2)),
            full((1, d)), full((d, d)), full((d, d)), full((d, d)),
        ],
        out_specs=row_blk(0),
        out_shape=jax.ShapeDtypeStruct((rows, d), F32),
        compiler_params=_cparams(("parallel",)),
        name="branch_merge",
    )(ys2, u2, ya2, u2, u2, x2d, mod3, norm_w, w_bs, w_ba, w_o)


def _ffn_kernel(x_ref, xp_ref, xn_ref, sh_ref, sc_ref, g2_ref, nw_ref, wu_ref, cw_ref, cb_ref,
                wd_ref, fw_ref, o_ref, h_s, acc_s, *, tm, tf, seg_len, final_norm):
    i = pl.program_id(0)
    j = pl.program_id(1)
    halo = BF16_ROWS

    @pl.when(j == 0)
    def _():
        def norm_mod(x):
            r = lax.rsqrt(jnp.mean(x * x, axis=-1, keepdims=True) + EPS)
            return ((x * r * nw_ref[...]) * (1.0 + sc_ref[0]) + sh_ref[0]).astype(BF16)

        h_s[0:halo, :] = norm_mod(xp_ref[...])
        h_s[halo:halo + tm, :] = norm_mod(x_ref[...])
        h_s[halo + tm:2 * halo + tm, :] = norm_mod(xn_ref[...])
        acc_s[...] = jnp.zeros_like(acc_s)

    u = jnp.dot(h_s[...], wu_ref[...], preferred_element_type=F32)
    n_ext = tm + 2 * halo
    pos = (i * tm + lax.broadcasted_iota(jnp.int32, (tm, 1), 0)) % seg_len
    up = jnp.where(pos == 0, 0.0, pltpu.roll(u, 1, axis=0)[halo:halo + tm])
    dn = jnp.where(pos == seg_len - 1, 0.0, pltpu.roll(u, n_ext - 1, axis=0)[halo:halo + tm])
    w = cw_ref[...]
    uc = up * w[0:1] + u[halo:halo + tm] * w[1:2] + dn * w[2:3] + cb_ref[...]
    act = (_silu(uc[:, :tf]) * uc[:, tf:]).astype(BF16)
    acc_s[...] += jnp.dot(act, wd_ref[...], preferred_element_type=F32)

    @pl.when(j == pl.num_programs(1) - 1)
    def _():
        y = x_ref[...] + g2_ref[0] * acc_s[...]
        if final_norm:
            y = y * lax.rsqrt(jnp.mean(y * y, axis=-1, keepdims=True) + EPS) * fw_ref[...]
        o_ref[...] = y


def _conv_ffn(x2d, mod3, mod_row, norm_w, w_up, conv_w, conv_b, w_down, final_w, *,
              tm, tf, seg_len, final_norm):
    rows = x2d.shape[0]
    d = D_MODEL
    halo = BF16_ROWS
    hb = tm // halo
    last_blk = rows // halo - 1
    return pl.pallas_call(
        functools.partial(_ffn_kernel, tm=tm, tf=tf, seg_len=seg_len, final_norm=final_norm),
        grid=(rows // tm, D_FF // tf),
        in_specs=[
            pl.BlockSpec((tm, d), lambda i, j: (i, 0)),
            pl.BlockSpec((halo, d), lambda i, j: (jnp.maximum(i * hb - 1, 0), 0)),
            pl.BlockSpec((halo, d), lambda i, j: (jnp.minimum((i + 1) * hb, last_blk), 0)),
            pl.BlockSpec((1, 1, d), lambda i, j: (mod_row(i), 0, 3)),
            pl.BlockSpec((1, 1, d), lambda i, j: (mod_row(i), 0, 4)),
            pl.BlockSpec((1, 1, d), lambda i, j: (mod_row(i), 0, 5)),
            pl.BlockSpec((1, d), lambda i, j: (0, 0)),
            pl.BlockSpec((d, 2 * tf), lambda i, j: (0, j)),
            pl.BlockSpec((3, 2 * tf), lambda i, j: (0, j)),
            pl.BlockSpec((1, 2 * tf), lambda i, j: (0, j)),
            pl.BlockSpec((tf, d), lambda i, j: (j, 0)),
            pl.BlockSpec((1, d), lambda i, j: (0, 0)),
        ],
        out_specs=pl.BlockSpec((tm, d), lambda i, j: (i, 0)),
        out_shape=jax.ShapeDtypeStruct((rows, d), F32),
        scratch_shapes=[pltpu.VMEM((tm + 2 * halo, d), BF16), pltpu.VMEM((tm, d), F32)],
        compiler_params=_cparams(("parallel", "arbitrary")),
        name="conv_ffn",
    )(x2d, x2d, x2d, mod3, mod3, mod3, norm_w, w_up, conv_w, conv_b, w_down, final_w)


def _in_col_perm():
    sizes = (SSD_WIDTH, SSD_CONV_CH, N_DT, ATT_QK_WIDTH, ATT_QK_WIDTH, ATT_WIDTH, 2 * D_MODEL)
    starts = np.concatenate([[0], np.cumsum(sizes)])
    rng = lambda k: np.arange(starts[k], starts[k + 1])
    z, xbc, dt, q, k, v, gates = (rng(n) for n in range(7))
    main = np.concatenate([z, q, gates, xbc, k, v])
    dt_order = np.array([dt[d * SSD_HEADS + g * SSD_HEADS_PER_GROUP + r]
                         for g in range(SSD_GROUPS) for d in range(2)
                         for r in range(SSD_HEADS_PER_GROUP)])
    return main, dt_order


def _dt_param_order(p2h):
    return p2h.reshape(2, SSD_GROUPS, SSD_HEADS_PER_GROUP).transpose(1, 0, 2).reshape(N_DT)


def _rope_tables(n_tokens):
    rows = n_tokens // GRID_W
    inv_freq = ROPE_BASE ** (-jnp.arange(ROPE_PAIRS, dtype=F32) / ROPE_PAIRS)
    ang_r = jnp.broadcast_to(jnp.arange(rows, dtype=F32)[:, None, None] * inv_freq, (rows, GRID_W, ROPE_PAIRS))
    ang_c = jnp.broadcast_to(jnp.arange(GRID_W, dtype=F32)[None, :, None] * inv_freq, (rows, GRID_W, ROPE_PAIRS))
    ang = jnp.stack([ang_r, ang_c], axis=2).reshape(n_tokens, 2, 1, ROPE_PAIRS)
    cos = jnp.broadcast_to(jnp.cos(ang), (n_tokens, 2, 2, ROPE_PAIRS))
    sin = jnp.broadcast_to(jnp.sin(ang), (n_tokens, 2, 2, ROPE_PAIRS))
    zero = jnp.zeros_like(sin[:, :, :1])
    sin_a = jnp.concatenate([-sin[:, :, :1], zero], axis=2)
    sin_b = jnp.concatenate([zero, sin[:, :, 1:]], axis=2)
    tile = lambda a: jnp.tile(a.reshape(n_tokens, ATT_HEAD_DIM), (1, 2))
    return tile(cos), tile(sin_a), tile(sin_b)


def _ffn_interleave(a, tf):
    lead = a.shape[:-1]
    nf = D_FF // tf
    a = a.reshape(*lead, 2, nf, tf)
    return jnp.swapaxes(a, -3, -2).reshape(*lead, 2 * D_FF)


def kernel(x, c, ctx, c_ctx, w_mod, b_mod, norm1_w, w_in, ssd_conv_w, ssd_conv_b, ssd_a_log,
           ssd_dt_bias, ssd_d, ssd_norm_w, diff_lambda, att_subln_w, w_br_ssd, w_br_att, w_out,
           norm2_w, w_up, ffn_conv_w, ffn_conv_b, w_down, final_norm_w):
    bsz, n_lat, d = x.shape
    n_ctx = ctx.shape[1]
    depth = w_mod.shape[0]
    assert d == D_MODEL and bsz + 1 <= 16
    ctx_row = bsz

    tm_in = min(1024, n_lat)
    tm_in_c = min(1024, bsz * n_ctx)
    tn_in = 1024
    tm_merge = min(512, n_lat)
    tm_merge_c = min(512, bsz * n_ctx)
    tm_ffn = min(512, n_lat)
    tf = 256
    tq = min(256, n_lat)

    cc = jnp.zeros((16, d), F32).at[:bsz].set(c).at[ctx_row].set(c_ctx)
    mod = _modulation(cc, w_mod, b_mod)

    main_perm, dt_perm = _in_col_perm()
    rope_tabs = _rope_tables(n_lat)
    cols_full = dict(z=COL_Z, q=COL_Q, gs=COL_GS, ga=COL_GA, xs=COL_XS, b=COL_B, c=COL_C, k=COL_K, v=COL_V)

    xl = x.reshape(bsz * n_lat, d)
    xc = ctx.reshape(bsz * n_ctx, d)
    lat_row = lambda tm: (lambda i: (i * tm) // n_lat)
    ctx_rowf = lambda i: ctx_row

    for li in range(depth):
        with_ctx = li < depth - 1
        mod3 = mod[li].reshape(16, 1, N_MOD * d)
        w_main = w_in[li][:, main_perm].astype(BF16)
        w_dt = jnp.zeros((d, LANES), BF16).at[:, :N_DT].set(w_in[li][:, dt_perm].astype(BF16))
        w_dtt = w_in[li][:, dt_perm].T.astype(BF16)
        n1 = norm1_w[li].reshape(1, d)

        ul, dt_l, dtt_l = _in_proj(xl, mod3, lat_row(tm_in), n1, w_main, w_dt, w_dtt,
                                   tm=tm_in, tn=tn_in, col0=0)
        col0_c = 0 if with_ctx else COL_XS
        uc, dt_c, dtt_c = _in_proj(xc, mod3, ctx_rowf, n1, w_main, w_dt, w_dtt,
                                   tm=tm_in_c, tn=tn_in, col0=col0_c)
        cols_c = {k_: v_ - col0_c for k_, v_ in cols_full.items()}
        ul3 = ul.reshape(bsz, n_lat, -1)
        uc3 = uc.reshape(bsz, n_ctx, -1)

        ssd_p = dict(
            conv_w=ssd_conv_w[li], conv_b=ssd_conv_b[li].reshape(1, SSD_CONV_CH),
            bias_row=jnp.zeros((1, LANES), F32).at[0, :N_DT].set(_dt_param_order(ssd_dt_bias[li])),
            bias_col=_dt_param_order(ssd_dt_bias[li]).reshape(N_DT, 1),
            alog_col=_dt_param_order(ssd_a_log[li]).reshape(N_DT, 1),
            alog_x=jnp.repeat(ssd_a_log[li].reshape(2, SSD_GROUPS, SSD_HEADS_PER_GROUP).transpose(1, 0, 2)
                              .reshape(SSD_GROUPS, 2 * SSD_HEADS_PER_GROUP), SSD_HEAD_DIM, axis=1)
            .reshape(SSD_GROUPS, 1, 2 * GROUP_X),
            dskip_x=jnp.repeat(ssd_d[li], SSD_HEAD_DIM).reshape(SSD_GROUPS, 1, GROUP_X),
        )
        ys_l, ys_c = _ssd(ul3, uc3, cols_full, cols_c, dt_l, dt_c, dtt_l, dtt_c, ssd_p)

        lam_init = 0.8 - 0.6 * math.exp(-0.3 * li)
        sw = att_subln_w[li].reshape(1, ATT_V_DIM)
        ya_l = _attention(ul3, COL_Q, uc3, cols_c, ul3, cols_full, rope_tabs, diff_lambda[li], sw,
                          lam_init=lam_init, tq=tq)

        w_bs = w_br_ssd[li].astype(BF16)
        w_ba = w_br_att[li].astype(BF16)
        w_o = w_out[li].astype(BF16)
        sn = ssd_norm_w[li].reshape(1, SSD_WIDTH)
        n2 = norm2_w[li].reshape(1, d)
        w_u = _ffn_interleave(w_up[li], tf).astype(BF16)
        cw = _ffn_interleave(ffn_conv_w[li], tf)
        cb = _ffn_interleave(ffn_conv_b[li], tf).reshape(1, 2 * D_FF)
        w_d = w_down[li].astype(BF16)
        fw = final_norm_w.reshape(1, d)

        if with_ctx:
            ya_c = _attention(uc3, cols_c["q"], uc3, cols_c, None, None, None, diff_lambda[li], sw,
                              lam_init=lam_init, tq=n_ctx)
            xc = _merge(ys_c.reshape(-1, SSD_WIDTH), uc, cols_c, ya_c.reshape(-1, ATT_WIDTH), xc, mod3,
                        ctx_rowf, sn, w_bs, w_ba, w_o, tm=tm_merge_c)
            xc = _conv_ffn(xc, mod3, ctx_rowf, n2, w_u, cw, cb, w_d, fw, tm=n_ctx, tf=tf,
                           seg_len=n_ctx, final_norm=False)

        xl = _merge(ys_l.reshape(-1, SSD_WIDTH), ul, cols_full, ya_l.reshape(-1, ATT_WIDTH), xl, mod3,
                    lat_row(tm_merge), sn, w_bs, w_ba, w_o, tm=tm_merge)
        xl = _conv_ffn(xl, mod3, lat_row(tm_ffn), n2, w_u, cw, cb, w_d, fw, tm=tm_ffn, tf=tf,
                       seg_len=n_lat, final_norm=not with_ctx)

    return xl.reshape(bsz, n_lat, d)
```

```python
import functools
import math

import jax
import jax.numpy as jnp
import numpy as np
from jax import lax
from jax.experimental import pallas as pl
from jax.experimental.pallas import tpu as pltpu

F32 = jnp.float32
BF16 = jnp.bfloat16

D_MODEL = 1024
EPS = 1e-6
N_MOD = 6
GRID_W = 64

SSD_HEADS = 16
SSD_HEAD_DIM = 64
SSD_WIDTH = SSD_HEADS * SSD_HEAD_DIM
SSD_GROUPS = 4
SSD_HEADS_PER_GROUP = SSD_HEADS // SSD_GROUPS
SSD_STATE = 128
SSD_CHUNK = 128
SSD_BC_WIDTH = SSD_GROUPS * SSD_STATE
SSD_CONV_CH = SSD_WIDTH + 2 * SSD_BC_WIDTH
GROUP_X = SSD_HEADS_PER_GROUP * SSD_HEAD_DIM
SSD_GROUP_CHUNKS = 4

ATT_HEADS = 8
ATT_HEAD_DIM = 64
ATT_V_DIM = 2 * ATT_HEAD_DIM
ATT_QK_WIDTH = ATT_HEADS * 2 * ATT_HEAD_DIM
ATT_WIDTH = ATT_HEADS * ATT_V_DIM
ATT_SCALE = ATT_HEAD_DIM ** -0.5
LOG2_E = math.log2(math.e)
ATT_SUB_ROWS = 128
ROPE_BASE = 10000.0
ROPE_PAIRS = ATT_HEAD_DIM // 4

D_FF = 2816
N_DT = 2 * SSD_HEADS

COL_Z = 0
COL_Q = COL_Z + SSD_WIDTH
COL_GS = COL_Q + ATT_QK_WIDTH
COL_GA = COL_GS + D_MODEL
COL_XS = COL_GA + D_MODEL
COL_B = COL_XS + SSD_WIDTH
COL_C = COL_B + SSD_BC_WIDTH
COL_K = COL_C + SSD_BC_WIDTH
COL_V = COL_K + ATT_QK_WIDTH
N_COLS = COL_V + ATT_WIDTH

LANES = 128
BF16_ROWS = 16
VMEM_LIMIT = 56 * 1024 * 1024


def _sigmoid(x):
    return 0.5 * jnp.tanh(0.5 * x) + 0.5


def _silu(x):
    return x * _sigmoid(x)


def _softplus(x):
    return jnp.maximum(x, 0.0) + jnp.log1p(jnp.exp(-jnp.abs(x)))


def _bdot(a, b):
    return jnp.dot(a.astype(BF16), b.astype(BF16), preferred_element_type=F32)


def _bdot_nt(a, b):
    return lax.dot_general(a.astype(BF16), b.astype(BF16), (((1,), (1,)), ((), ())),
                           preferred_element_type=F32)


def _split3(x):
    hi = x.astype(BF16)
    r1 = x - hi.astype(F32)
    mid = r1.astype(BF16)
    lo = (r1 - mid.astype(F32)).astype(BF16)
    return hi, mid, lo


def _dot3_rhs(a_bf16, x):
    hi, mid, lo = _split3(x)
    out = jnp.dot(a_bf16, hi, preferred_element_type=F32)
    out = out + jnp.dot(a_bf16, mid, preferred_element_type=F32)
    return out + jnp.dot(a_bf16, lo, preferred_element_type=F32)


def _dot3_lhs(x, b_bf16):
    hi, mid, lo = _split3(x)
    out = jnp.dot(hi, b_bf16, preferred_element_type=F32)
    out = out + jnp.dot(mid, b_bf16, preferred_element_type=F32)
    return out + jnp.dot(lo, b_bf16, preferred_element_type=F32)


def _cparams(sem):
    return pltpu.CompilerParams(dimension_semantics=sem, vmem_limit_bytes=VMEM_LIMIT)


def _mod_kernel(c_ref, w_ref, b_ref, o_ref):
    a = _silu(c_ref[...])
    o_ref[...] = _dot3_both(a, w_ref[...]) + b_ref[...]


def _dot3_both(a, w):
    ah, am, al = _split3(a)
    wh, wm, wl = _split3(w)
    d = lambda p, q: jnp.dot(p, q, preferred_element_type=F32)
    return (d(ah, wh) + (d(ah, wm) + d(am, wh))
            + (d(ah, wl) + d(am, wm) + d(al, wh)))


def _modulation(cc, w_mod, b_mod):
    depth = w_mod.shape[0]
    tn = 1024
    return pl.pallas_call(
        _mod_kernel,
        grid=(depth, N_MOD * D_MODEL // tn),
        in_specs=[
            pl.BlockSpec((16, D_MODEL), lambda l, j: (0, 0)),
            pl.BlockSpec((None, D_MODEL, tn), lambda l, j: (l, 0, j)),
            pl.BlockSpec((None, 1, tn), lambda l, j: (l, 0, j)),
        ],
        out_specs=pl.BlockSpec((None, 16, tn), lambda l, j: (l, 0, j)),
        out_shape=jax.ShapeDtypeStruct((depth, 16, N_MOD * D_MODEL), F32),
        compiler_params=_cparams(("parallel", "parallel")),
        name="modulation",
    )(cc, w_mod, b_mod.reshape(depth, 1, N_MOD * D_MODEL))


def _in_proj_kernel(x_ref, sh_ref, sc_ref, nw_ref, w_ref, wdt_ref, wdtt_ref,
                    u_ref, dt_ref, dtt_ref, h_ref):
    @pl.when(pl.program_id(1) == 0)
    def _():
        x = x_ref[...]
        r = lax.rsqrt(jnp.mean(x * x, axis=-1, keepdims=True) + EPS)
        h = (x * r * nw_ref[...]) * (1.0 + sc_ref[0]) + sh_ref[0]
        hb = h.astype(BF16)
        h_ref[...] = hb
        dt_ref[...] = jnp.dot(hb, wdt_ref[...], preferred_element_type=F32)
        dtt_ref[...] = lax.dot_general(wdtt_ref[...], hb, (((1,), (1,)), ((), ())),
                                       preferred_element_type=F32)

    u_ref[...] = jnp.dot(h_ref[...], w_ref[...], preferred_element_type=F32).astype(BF16)


def _in_proj(x2d, mod3, mod_row, norm_w, w_main, w_dt, w_dtt, *, tm, tn, col0):
    rows = x2d.shape[0]
    n_out = w_main.shape[1] - col0
    cb0 = col0 // tn
    return pl.pallas_call(
        _in_proj_kernel,
        grid=(rows // tm, n_out // tn),
        in_specs=[
            pl.BlockSpec((tm, D_MODEL), lambda i, j: (i, 0)),
            pl.BlockSpec((1, 1, D_MODEL), lambda i, j: (mod_row(i), 0, 0)),
            pl.BlockSpec((1, 1, D_MODEL), lambda i, j: (mod_row(i), 0, 1)),
            pl.BlockSpec((1, D_MODEL), lambda i, j: (0, 0)),
            pl.BlockSpec((D_MODEL, tn), lambda i, j: (0, j + cb0)),
            pl.BlockSpec((D_MODEL, LANES), lambda i, j: (0, 0)),
            pl.BlockSpec((N_DT, D_MODEL), lambda i, j: (0, 0)),
        ],
        out_specs=[
            pl.BlockSpec((tm, tn), lambda i, j: (i, j)),
            pl.BlockSpec((tm, LANES), lambda i, j: (i, 0)),
            pl.BlockSpec((N_DT, tm), lambda i, j: (0, i)),
        ],
        out_shape=[
            jax.ShapeDtypeStruct((rows, n_out), BF16),
            jax.ShapeDtypeStruct((rows, LANES), F32),
            jax.ShapeDtypeStruct((N_DT, rows), F32),
        ],
        scratch_shapes=[pltpu.VMEM((tm, D_MODEL), BF16)],
        compiler_params=_cparams(("parallel", "arbitrary")),
        name="in_proj",
    )(x2d, mod3, mod3, norm_w, w_main, w_dt, w_dtt)


def _ssd_kernel(xl_ref, bl_ref, cl_ref, xc_ref, bc_ref, cc_ref,
                dtl_ref, dtc_ref, dttl_ref, dttc_ref,
                cwx_ref, cwb_ref, cwc_ref, cbx_ref, cbb_ref, cbc_ref,
                biasr_ref, biasc_ref, alogx_ref, alogc_ref, dskip_ref,
                yl_ref, yc_ref,
                cm_s, y_s, ex_s, st_s, dec_s, hin_s, *, n_lat, n_ctx):
    t = SSD_CHUNK
    g = pl.program_id(1)
    nc_ctx = n_ctx // t
    nc_lat = n_lat // t
    n_chunks = nc_ctx + nc_lat
    gx = GROUP_X

    ri = lax.broadcasted_iota(jnp.int32, (t, t), 0)
    ci = lax.broadcasted_iota(jnp.int32, (t, t), 1)
    lower = ri >= ci
    tril = lower.astype(BF16)
    triu = (ri <= ci).astype(BF16)
    er = lax.broadcasted_iota(jnp.int32, (LANES, 2 * gx), 0)
    ec = lax.broadcasted_iota(jnp.int32, (LANES, 2 * gx), 1)
    expand = (er == g * (2 * SSD_HEADS_PER_GROUP) + ec // SSD_HEAD_DIM).astype(BF16)
    tt = jnp.concatenate([triu, tril], axis=1)
    a_x = -jnp.exp(alogx_ref[0]) * LOG2_E
    a_c = -jnp.exp(alogc_ref[...]) * LOG2_E

    si = lax.broadcasted_iota(jnp.int32, (2 * t, t + 2 * BF16_ROWS), 0)
    sj = lax.broadcasted_iota(jnp.int32, (2 * t, t + 2 * BF16_ROWS), 1)
    src = jnp.where(si < t, jnp.where(si == 0, t + BF16_ROWS - 1, si - 1),
                    jnp.where(si == 2 * t - 1, t + BF16_ROWS, si - t + 1))
    shift_m = (sj == src).astype(BF16)
    conv_w = jnp.concatenate([cwx_ref[...], cwb_ref[...], cwc_ref[...]], axis=1)
    conv_b = jnp.concatenate([cbx_ref[...], cbb_ref[...], cbc_ref[...]], axis=1)

    def phase_a(cs, x_ref, b_ref, c_ref, dt_ref, dtt_ref, n_seg_chunks, chunk0):
        seg_rows = n_seg_chunks * t
        n8 = 2 * SSD_HEADS_PER_GROUP
        st = [dict(c=c, r0=pl.multiple_of(c * t, t), o0=pl.multiple_of((c + chunk0) * t, t)) for c in cs]

        for s in st:
            c, r0 = s["c"], s["r0"]
            lo = pl.multiple_of(jnp.maximum(r0 - BF16_ROWS, 0), BF16_ROWS)
            hi = pl.multiple_of(jnp.minimum(r0 + t, seg_rows - BF16_ROWS), BF16_ROWS)
            rows = lambda a, n: jnp.concatenate(
                [x_ref[0, pl.ds(a, n), :], b_ref[0, pl.ds(a, n), :], c_ref[0, pl.ds(a, n), :]], axis=1)
            cur = rows(r0, t)
            zero = jnp.zeros((BF16_ROWS, cur.shape[1]), BF16)
            before = jnp.where(c > 0, rows(lo, BF16_ROWS), zero)
            after = jnp.where(c < n_seg_chunks - 1, rows(hi, BF16_ROWS), zero)
            s["cur"] = cur
            s["sh"] = jnp.dot(shift_m, jnp.concatenate([cur, before, after], axis=0),
                              preferred_element_type=F32)
            dt = _softplus(dt_ref[0, pl.ds(r0, t), :] + biasr_ref[...])
            s["dtx"] = _dot3_lhs(dt, expand)
            at = _softplus(dtt_ref[:, pl.ds(r0, t)] + biasc_ref[...]) * a_c
            hi3, mid3, lo3 = _split3(at)
            s["ct3"] = jnp.dot(jnp.concatenate([hi3, mid3, lo3], axis=0), tt, preferred_element_type=F32)

        for s in st:
            ax = s["dtx"] * a_x
            s["csf"] = _dot3_rhs(tril, ax[:, :gx])
            s["csb"] = _dot3_rhs(triu, ax[:, gx:])
            sh = s["sh"]
            xbc = _silu(sh[:t] * conv_w[0:1] + s["cur"].astype(F32) * conv_w[1:2]
                        + sh[t:] * conv_w[2:3] + conv_b)
            s["xs"] = xbc[:, :gx]
            s["bm"] = xbc[:, gx:gx + SSD_STATE].astype(BF16)
            s["cm"] = xbc[:, gx + SSD_STATE:].astype(BF16)
            cm_s[pl.ds(s["o0"], t), :] = s["cm"]
            s["cb"] = _bdot_nt(s["cm"], s["bm"])

        for s in st:
            ct3 = s["ct3"]
            cst = ct3[0:n8] + ct3[n8:2 * n8] + ct3[2 * n8:3 * n8]
            xs, dtx, cb = s["xs"], s["dtx"], s["cb"]
            y = dskip_ref[0] * xs
            for d, cs_d in ((0, s["csf"]), (1, s["csb"])):
                mask = lower if d == 0 else (ri <= ci)
                parts = []
                for r in range(SSD_HEADS_PER_GROUP):
                    col = cs_d[:, r * SSD_HEAD_DIM:r * SSD_HEAD_DIM + 1]
                    k = d * SSD_HEADS_PER_GROUP + r
                    row = cst[k:k + 1, d * t:(d + 1) * t]
                    seg = jnp.exp2(jnp.where(mask, col - row, -jnp.inf))
                    m = (cb * seg).astype(BF16)
                    sl = slice(r * SSD_HEAD_DIM, (r + 1) * SSD_HEAD_DIM)
                    xd = xs[:, sl] * dtx[:, d * gx + r * SSD_HEAD_DIM:d * gx + (r + 1) * SSD_HEAD_DIM]
                    parts.append(jnp.dot(m, xd.astype(BF16), preferred_element_type=F32))
                y = y + jnp.concatenate(parts, axis=1)
            y_s[pl.ds(s["o0"], t), :] = y

        for s in st:
            xs, dtx, csf, csb = s["xs"], s["dtx"], s["csf"], s["csb"]
            endf = csf[t - 1:t, :]
            endb = csb[0:1, :]
            ex_s[pl.ds(s["o0"], t), :] = jnp.exp2(jnp.concatenate([csf, csb], axis=1))
            xdd = jnp.concatenate([xs * dtx[:, :gx] * jnp.exp2(endf - csf),
                                   xs * dtx[:, gx:] * jnp.exp2(endb - csb)], axis=1).astype(BF16)
            bt = jnp.transpose(s["bm"].astype(F32)).astype(BF16)
            st_s[s["c"] + chunk0] = jnp.dot(bt, xdd, preferred_element_type=F32)
            dec_s[pl.ds(s["c"] + chunk0, 1), :] = jnp.exp2(jnp.concatenate([endf, endb], axis=1))

    def run_phase_a(n_seg_chunks, chunk0, refs):
        group = math.gcd(n_seg_chunks, SSD_GROUP_CHUNKS)

        def body(i, carry):
            phase_a([i * group + k for k in range(group)], *refs, n_seg_chunks, chunk0)
            return carry

        lax.fori_loop(0, n_seg_chunks // group, body, 0)

    run_phase_a(nc_ctx, 0, (xc_ref, bc_ref, cc_ref, dtc_ref, dttc_ref))
    run_phase_a(nc_lat, nc_ctx, (xl_ref, bl_ref, cl_ref, dtl_ref, dttl_ref))

    order_f = list(range(n_chunks))
    order_b = list(range(nc_ctx - 1, -1, -1)) + list(range(n_chunks - 1, nc_ctx - 1, -1))
    for d, order in ((0, order_f), (1, order_b)):
        h = jnp.zeros((SSD_STATE, gx), F32)
        for c in order:
            hin_s[c, :, d * gx:(d + 1) * gx] = h.astype(BF16)
            h = dec_s[c:c + 1, d * gx:(d + 1) * gx] * h + st_s[c, :, d * gx:(d + 1) * gx]

    def phase_c(c, carry):
        o0 = pl.multiple_of(c * t, t)
        yo = jnp.dot(cm_s[pl.ds(o0, t), :], hin_s[c], preferred_element_type=F32) * ex_s[pl.ds(o0, t), :]
        y_s[pl.ds(o0, t), :] = y_s[pl.ds(o0, t), :] + yo[:, :gx] + yo[:, gx:]
        return carry

    lax.fori_loop(0, n_chunks, phase_c, 0, unroll=2)
    yc_ref[0] = y_s[0:n_ctx, :].astype(BF16)
    yl_ref[0] = y_s[n_ctx:n_ctx + n_lat, :].astype(BF16)


def _ssd(ul3, uc3, cols_l, cols_c, dt_l, dt_c, dtt_l, dtt_c, p):
    bsz, n_lat, _ = ul3.shape
    n_ctx = uc3.shape[1]
    gx = GROUP_X
    n_chunks = (n_lat + n_ctx) // SSD_CHUNK
    n_tot = n_lat + n_ctx

    def seq_specs(cols, n):
        xb, bb, cb = cols["xs"] // gx, cols["b"] // SSD_STATE, cols["c"] // SSD_STATE
        return [
            pl.BlockSpec((1, n, gx), lambda b, g: (b, 0, xb + g)),
            pl.BlockSpec((1, n, SSD_STATE), lambda b, g: (b, 0, bb + g)),
            pl.BlockSpec((1, n, SSD_STATE), lambda b, g: (b, 0, cb + g)),
        ]

    xoff = 0
    boff = SSD_WIDTH // SSD_STATE
    coff = (SSD_WIDTH + SSD_BC_WIDTH) // SSD_STATE
    n8 = 2 * SSD_HEADS_PER_GROUP
    in_specs = (
        seq_specs(cols_l, n_lat) + seq_specs(cols_c, n_ctx) + [
            pl.BlockSpec((1, n_lat, LANES), lambda b, g: (b, 0, 0)),
            pl.BlockSpec((1, n_ctx, LANES), lambda b, g: (b, 0, 0)),
            pl.BlockSpec((n8, n_lat), lambda b, g: (g, b)),
            pl.BlockSpec((n8, n_ctx), lambda b, g: (g, b)),
            pl.BlockSpec((3, gx), lambda b, g: (0, xoff + g)),
            pl.BlockSpec((3, SSD_STATE), lambda b, g: (0, boff + g)),
            pl.BlockSpec((3, SSD_STATE), lambda b, g: (0, coff + g)),
            pl.BlockSpec((1, gx), lambda b, g: (0, xoff + g)),
            pl.BlockSpec((1, SSD_STATE), lambda b, g: (0, boff + g)),
            pl.BlockSpec((1, SSD_STATE), lambda b, g: (0, coff + g)),
            pl.BlockSpec((1, LANES), lambda b, g: (0, 0)),
            pl.BlockSpec((n8, 1), lambda b, g: (g, 0)),
            pl.BlockSpec((1, 1, 2 * gx), lambda b, g: (g, 0, 0)),
            pl.BlockSpec((n8, 1), lambda b, g: (g, 0)),
            pl.BlockSpec((1, 1, gx), lambda b, g: (g, 0, 0)),
        ])
    return pl.pallas_call(
        functools.partial(_ssd_kernel, n_lat=n_lat, n_ctx=n_ctx),
        grid=(bsz, SSD_GROUPS),
        in_specs=in_specs,
        out_specs=[
            pl.BlockSpec((1, n_lat, gx), lambda b, g: (b, 0, g)),
            pl.BlockSpec((1, n_ctx, gx), lambda b, g: (b, 0, g)),
        ],
        out_shape=[
            jax.ShapeDtypeStruct((bsz, n_lat, SSD_WIDTH), BF16),
            jax.ShapeDtypeStruct((bsz, n_ctx, SSD_WIDTH), BF16),
        ],
        scratch_shapes=[
            pltpu.VMEM((n_tot, SSD_STATE), BF16),
            pltpu.VMEM((n_tot, gx), F32),
            pltpu.VMEM((n_tot, 2 * gx), F32),
            pltpu.VMEM((n_chunks, SSD_STATE, 2 * gx), F32),
            pltpu.VMEM((n_chunks + (-n_chunks) % 8, 2 * gx), F32),
            pltpu.VMEM((n_chunks, SSD_STATE, 2 * gx), BF16),
        ],
        compiler_params=_cparams(("parallel", "parallel")),
        name="ssd_scan",
    )(ul3, ul3, ul3, uc3, uc3, uc3,
      dt_l.reshape(bsz, n_lat, LANES), dt_c.reshape(bsz, n_ctx, LANES), dtt_l, dtt_c,
      p["conv_w"], p["conv_w"], p["conv_w"], p["conv_b"], p["conv_b"], p["conv_b"],
      p["bias_row"], p["bias_col"], p["alog_x"], p["alog_col"], p["dskip_x"])


def _rope(x, cos, sin_a, sin_b):
    return x * cos + pltpu.roll(x, LANES - ROPE_PAIRS, axis=1) * sin_a + pltpu.roll(x, ROPE_PAIRS, axis=1) * sin_b


def _attn_kernel(*refs, rope, n_ctx, n_lat, tq, lam_init):
    if n_lat:
        (q_ref, kc_ref, vc_ref, kl_ref, vl_ref, cos_ref, sa_ref, sb_ref, lam_ref, sw_ref,
         o_ref, k0_s, k1_s, v_s) = refs
    else:
        (q_ref, kc_ref, vc_ref, lam_ref, sw_ref, o_ref, k0_s, k1_s, v_s) = refs
    qi = pl.program_id(2)
    hd = ATT_HEAD_DIM

    @pl.when(qi == 0)
    def _():
        kc = kc_ref[0]
        k0_s[0:n_ctx, :] = kc[:, :hd]
        k1_s[0:n_ctx, :] = kc[:, hd:]
        v_s[0:n_ctx, :] = vc_ref[0]
        if n_lat:
            kl = kl_ref[0].astype(F32)
            if rope:
                kl = _rope(kl, cos_ref[...], sa_ref[...], sb_ref[...])
            kl = kl.astype(BF16)
            k0_s[n_ctx:n_ctx + n_lat, :] = kl[:, :hd]
            k1_s[n_ctx:n_ctx + n_lat, :] = kl[:, hd:]
            v_s[n_ctx:n_ctx + n_lat, :] = vl_ref[0]

    lf = lam_ref[...]
    lam = (jnp.exp(jnp.sum(lf[0:1] * lf[1:2], axis=-1, keepdims=True))
           - jnp.exp(jnp.sum(lf[2:3] * lf[3:4], axis=-1, keepdims=True)) + lam_init)

    q = q_ref[0].astype(F32)
    if rope:
        r0 = pl.multiple_of(qi * tq, tq)
        q = _rope(q, cos_ref[pl.ds(r0, tq), :], sa_ref[pl.ds(r0, tq), :], sb_ref[pl.ds(r0, tq), :])
    q = (q * (ATT_SCALE * LOG2_E)).astype(BF16)

    sub = min(ATT_SUB_ROWS, tq)
    chains = [(r, c) for r in range(tq // sub) for c in range(2)]

    def scores(r, c):
        qc = q[r * sub:(r + 1) * sub, c * hd:(c + 1) * hd]
        k_s = k0_s if c == 0 else k1_s
        return lax.dot_general(qc, k_s[...], (((1,), (1,)), ((), ())), preferred_element_type=F32)

    s_next = scores(*chains[0])
    outs = {}
    for n, (r, c) in enumerate(chains):
        s = s_next
        if n + 1 < len(chains):
            s_next = scores(*chains[n + 1])
        e = jnp.exp2(s - jnp.max(s, axis=-1, keepdims=True))
        inv = 1.0 / jnp.sum(e, axis=-1, keepdims=True)
        outs[c] = jnp.dot(e.astype(BF16), v_s[...], preferred_element_type=F32) * inv
        if c == 1:
            o = outs[0] - lam * outs[1]
            on = o * lax.rsqrt(jnp.mean(o * o, axis=-1, keepdims=True) + EPS) * sw_ref[...]
            o_ref[0, r * sub:(r + 1) * sub, :] = (on * (1.0 - lam_init)).astype(BF16)


def _attention(uq3, col_q, uc3, cols_c, ul3, cols_l, rope_tabs, lam_p, subln_w, *, lam_init, tq):
    bsz, n_q, _ = uq3.shape
    n_ctx = uc3.shape[1]
    n_lat = 0 if ul3 is None else ul3.shape[1]
    n_k = n_ctx + n_lat
    vd = ATT_V_DIM
    qb = col_q // vd
    in_specs = [
        pl.BlockSpec((1, tq, vd), lambda b, h, i: (b, i, qb + h)),
        pl.BlockSpec((1, n_ctx, vd), lambda b, h, i: (b, 0, cols_c["k"] // vd + h)),
        pl.BlockSpec((1, n_ctx, vd), lambda b, h, i: (b, 0, cols_c["v"] // vd + h)),
    ]
    args = [uq3, uc3, uc3]
    if n_lat:
        in_specs += [
            pl.BlockSpec((1, n_lat, vd), lambda b, h, i: (b, 0, cols_l["k"] // vd + h)),
            pl.BlockSpec((1, n_lat, vd), lambda b, h, i: (b, 0, cols_l["v"] // vd + h)),
            pl.BlockSpec((n_lat, vd), lambda b, h, i: (0, 0)),
            pl.BlockSpec((n_lat, vd), lambda b, h, i: (0, 0)),
            pl.BlockSpec((n_lat, vd), lambda b, h, i: (0, 0)),
        ]
        args += [ul3, ul3, *rope_tabs]
    in_specs += [
        pl.BlockSpec((4, ATT_HEAD_DIM), lambda b, h, i: (0, 0)),
        pl.BlockSpec((1, vd), lambda b, h, i: (0, 0)),
    ]
    args += [lam_p, subln_w]
    return pl.pallas_call(
        functools.partial(_attn_kernel, rope=bool(n_lat), n_ctx=n_ctx, n_lat=n_lat, tq=tq,
                          lam_init=lam_init),
        grid=(bsz, ATT_HEADS, n_q // tq),
        in_specs=in_specs,
        out_specs=pl.BlockSpec((1, tq, vd), lambda b, h, i: (b, i, h)),
        out_shape=jax.ShapeDtypeStruct((bsz, n_q, ATT_WIDTH), BF16),
        scratch_shapes=[
            pltpu.VMEM((n_k, ATT_HEAD_DIM), BF16),
            pltpu.VMEM((n_k, ATT_HEAD_DIM), BF16),
            pltpu.VMEM((n_k, vd), BF16),
        ],
        compiler_params=_cparams(("parallel", "parallel", "arbitrary")),
        name="diff_attn",
    )(*args)


def _merge_kernel(ys_ref, z_ref, ya_ref, gs_ref, ga_ref, x_ref, g1_ref, nw_ref,
                  wbs_ref, wba_ref, wo_ref, o_ref):
    yz = ys_ref[...].astype(F32) * _silu(z_ref[...].astype(F32))
    ysn = yz * lax.rsqrt(jnp.mean(yz * yz, axis=-1, keepdims=True) + EPS) * nw_ref[...]
    ts = jnp.dot(ysn.astype(BF16), wbs_ref[...], preferred_element_type=F32)
    ta = jnp.dot(ya_ref[...], wba_ref[...], preferred_element_type=F32)
    tmix = _sigmoid(gs_ref[...].astype(F32)) * ts + _sigmoid(ga_ref[...].astype(F32)) * ta
    o_ref[...] = x_ref[...] + g1_ref[0] * jnp.dot(tmix.astype(BF16), wo_ref[...],
                                                   preferred_element_type=F32)


def _merge(ys2, u2, cols, ya2, x2d, mod3, mod_row, norm_w, w_bs, w_ba, w_o, *, tm):
    rows = x2d.shape[0]
    d = D_MODEL
    row_blk = lambda cb: pl.BlockSpec((tm, d), lambda i: (i, cb))
    full = lambda shape: pl.BlockSpec(shape, lambda i: (0, 0))
    return pl.pallas_call(
        _merge_kernel,
        grid=(rows // tm,),
        in_specs=[
            row_blk(0), row_blk(cols["z"] // d), row_blk(0),
            row_blk(cols["gs"] // d), row_blk(cols["ga"] // d), row_blk(0),
            pl.BlockSpec((1, 1, d), lambda i: (mod_row(i), 0, 2)),
            full((1, d)), full((d, d)), full((d, d)), full((d, d)),
        ],
        out_specs=row_blk(0),
        out_shape=jax.ShapeDtypeStruct((rows, d), F32),
        compiler_params=_cparams(("parallel",)),
        name="branch_merge",
    )(ys2, u2, ya2, u2, u2, x2d, mod3, norm_w, w_bs, w_ba, w_o)


def _ffn_kernel(x_ref, xp_ref, xn_ref, sh_ref, sc_ref, g2_ref, nw_ref, wu_ref, cw_ref, cb_ref,
                wd_ref, fw_ref, o_ref, h_s, u_s, acc_s, *, tm, tf, tiles_per_seg, final_norm):
    i = pl.program_id(0)
    halo = BF16_ROWS
    seg_first = (i % tiles_per_seg) == 0
    seg_last = (i % tiles_per_seg) == tiles_per_seg - 1

    def norm_mod(x):
        r = lax.rsqrt(jnp.mean(x * x, axis=-1, keepdims=True) + EPS)
        return ((x * r * nw_ref[...]) * (1.0 + sc_ref[0]) + sh_ref[0]).astype(BF16)

    h_s[0:halo, :] = norm_mod(xp_ref[...])
    h_s[halo:halo + tm, :] = norm_mod(x_ref[...])
    h_s[halo + tm:2 * halo + tm, :] = norm_mod(xn_ref[...])
    def up_proj(j):
        us = u_s.at[j % 2]
        us[...] = jnp.dot(h_s[...], wu_ref[:, j * 2 * tf:(j + 1) * 2 * tf], preferred_element_type=F32)
        us[halo - 1:halo, :] = jnp.where(seg_first, 0.0, us[halo - 1:halo, :])
        us[halo + tm:halo + tm + 1, :] = jnp.where(seg_last, 0.0, us[halo + tm:halo + tm + 1, :])

    n_f = D_FF // tf
    up_proj(0)
    for j in range(n_f):
        if j + 1 < n_f:
            up_proj(j + 1)
        us = u_s.at[j % 2]
        cols = slice(j * 2 * tf, (j + 1) * 2 * tf)
        w = cw_ref[:, cols]
        uc = (us[halo - 1:halo - 1 + tm, :] * w[0:1] + us[halo:halo + tm, :] * w[1:2]
              + us[halo + 1:halo + 1 + tm, :] * w[2:3] + cb_ref[:, cols])
        act = (_silu(uc[:, :tf]) * uc[:, tf:]).astype(BF16)
        down = jnp.dot(act, wd_ref[j * tf:(j + 1) * tf, :], preferred_element_type=F32)
        if j == 0:
            acc_s[...] = down
        else:
            acc_s[...] += down
    y = x_ref[...] + g2_ref[0] * acc_s[...]
    if final_norm:
        y = y * lax.rsqrt(jnp.mean(y * y, axis=-1, keepdims=True) + EPS) * fw_ref[...]
    o_ref[...] = y


def _conv_ffn(x2d, mod3, mod_row, norm_w, w_up, conv_w, conv_b, w_down, final_w, *,
              tm, tf, seg_len, final_norm):
    rows = x2d.shape[0]
    d = D_MODEL
    halo = BF16_ROWS
    hb = tm // halo
    last_blk = rows // halo - 1
    assert seg_len % tm == 0
    resident = lambda shape: pl.BlockSpec(shape, lambda i: (0, 0), pipeline_mode=pl.Buffered(1))
    return pl.pallas_call(
        functools.partial(_ffn_kernel, tm=tm, tf=tf, tiles_per_seg=seg_len // tm, final_norm=final_norm),
        grid=(rows // tm,),
        in_specs=[
            pl.BlockSpec((tm, d), lambda i: (i, 0)),
            pl.BlockSpec((halo, d), lambda i: (jnp.maximum(i * hb - 1, 0), 0)),
            pl.BlockSpec((halo, d), lambda i: (jnp.minimum((i + 1) * hb, last_blk), 0)),
            pl.BlockSpec((1, 1, d), lambda i: (mod_row(i), 0, 3)),
            pl.BlockSpec((1, 1, d), lambda i: (mod_row(i), 0, 4)),
            pl.BlockSpec((1, 1, d), lambda i: (mod_row(i), 0, 5)),
            resident((1, d)),
            resident((d, 2 * D_FF)),
            resident((3, 2 * D_FF)),
            resident((1, 2 * D_FF)),
            resident((D_FF, d)),
            resident((1, d)),
        ],
        out_specs=pl.BlockSpec((tm, d), lambda i: (i, 0)),
        out_shape=jax.ShapeDtypeStruct((rows, d), F32),
        scratch_shapes=[
            pltpu.VMEM((tm + 2 * halo, d), BF16),
            pltpu.VMEM((2, tm + 2 * halo, 2 * tf), F32),
            pltpu.VMEM((tm, d), F32),
        ],
        compiler_params=_cparams(("parallel",)),
        name="conv_ffn",
    )(x2d, x2d, x2d, mod3, mod3, mod3, norm_w, w_up, conv_w, conv_b, w_down, final_w)


def _in_col_perm():
    sizes = (SSD_WIDTH, SSD_CONV_CH, N_DT, ATT_QK_WIDTH, ATT_QK_WIDTH, ATT_WIDTH, 2 * D_MODEL)
    starts = np.concatenate([[0], np.cumsum(sizes)])
    rng = lambda k: np.arange(starts[k], starts[k + 1])
    z, xbc, dt, q, k, v, gates = (rng(n) for n in range(7))
    main = np.concatenate([z, q, gates, xbc, k, v])
    dt_order = np.array([dt[d * SSD_HEADS + g * SSD_HEADS_PER_GROUP + r]
                         for g in range(SSD_GROUPS) for d in range(2)
                         for r in range(SSD_HEADS_PER_GROUP)])
    return main, dt_order


def _dt_param_order(p2h):
    return p2h.reshape(2, SSD_GROUPS, SSD_HEADS_PER_GROUP).transpose(1, 0, 2).reshape(N_DT)


def _rope_tables(n_tokens):
    rows = n_tokens // GRID_W
    inv_freq = ROPE_BASE ** (-jnp.arange(ROPE_PAIRS, dtype=F32) / ROPE_PAIRS)
    ang_r = jnp.broadcast_to(jnp.arange(rows, dtype=F32)[:, None, None] * inv_freq, (rows, GRID_W, ROPE_PAIRS))
    ang_c = jnp.broadcast_to(jnp.arange(GRID_W, dtype=F32)[None, :, None] * inv_freq, (rows, GRID_W, ROPE_PAIRS))
    ang = jnp.stack([ang_r, ang_c], axis=2).reshape(n_tokens, 2, 1, ROPE_PAIRS)
    cos = jnp.broadcast_to(jnp.cos(ang), (n_tokens, 2, 2, ROPE_PAIRS))
    sin = jnp.broadcast_to(jnp.sin(ang), (n_tokens, 2, 2, ROPE_PAIRS))
    zero = jnp.zeros_like(sin[:, :, :1])
    sin_a = jnp.concatenate([-sin[:, :, :1], zero], axis=2)
    sin_b = jnp.concatenate([zero, sin[:, :, 1:]], axis=2)
    tile = lambda a: jnp.tile(a.reshape(n_tokens, ATT_HEAD_DIM), (1, 2))
    return tile(cos), tile(sin_a), tile(sin_b)


def _ffn_interleave(a, tf):
    lead = a.shape[:-1]
    nf = D_FF // tf
    a = a.reshape(*lead, 2, nf, tf)
    return jnp.swapaxes(a, -3, -2).reshape(*lead, 2 * D_FF)


def kernel(x, c, ctx, c_ctx, w_mod, b_mod, norm1_w, w_in, ssd_conv_w, ssd_conv_b, ssd_a_log,
           ssd_dt_bias, ssd_d, ssd_norm_w, diff_lambda, att_subln_w, w_br_ssd, w_br_att, w_out,
           norm2_w, w_up, ffn_conv_w, ffn_conv_b, w_down, final_norm_w):
    bsz, n_lat, d = x.shape
    n_ctx = ctx.shape[1]
    depth = w_mod.shape[0]
    assert d == D_MODEL and bsz + 1 <= 16
    ctx_row = bsz

    tm_in = min(1024, n_lat)
    tm_in_c = min(1024, bsz * n_ctx)
    tn_in = 1024
    tm_merge = min(512, n_lat)
    tm_merge_c = min(512, bsz * n_ctx)
    tm_ffn = min(512, n_lat)
    tf = 256
    tq = min(512, n_lat)

    cc = jnp.zeros((16, d), F32).at[:bsz].set(c).at[ctx_row].set(c_ctx)
    mod = _modulation(cc, w_mod, b_mod)

    main_perm, dt_perm = _in_col_perm()
    rope_tabs = _rope_tables(n_lat)
    cols_full = dict(z=COL_Z, q=COL_Q, gs=COL_GS, ga=COL_GA, xs=COL_XS, b=COL_B, c=COL_C, k=COL_K, v=COL_V)

    xl = x.reshape(bsz * n_lat, d)
    xc = ctx.reshape(bsz * n_ctx, d)
    lat_row = lambda tm: (lambda i: (i * tm) // n_lat)
    ctx_rowf = lambda i: ctx_row

    for li in range(depth):
        with_ctx = li < depth - 1
        mod3 = mod[li].reshape(16, 1, N_MOD * d)
        w_main = w_in[li][:, main_perm].astype(BF16)
        w_dt = jnp.zeros((d, LANES), BF16).at[:, :N_DT].set(w_in[li][:, dt_perm].astype(BF16))
        w_dtt = w_in[li][:, dt_perm].T.astype(BF16)
        n1 = norm1_w[li].reshape(1, d)

        ul, dt_l, dtt_l = _in_proj(xl, mod3, lat_row(tm_in), n1, w_main, w_dt, w_dtt,
                                   tm=tm_in, tn=tn_in, col0=0)
        col0_c = 0 if with_ctx else COL_XS
        uc, dt_c, dtt_c = _in_proj(xc, mod3, ctx_rowf, n1, w_main, w_dt, w_dtt,
                                   tm=tm_in_c, tn=tn_in, col0=col0_c)
        cols_c = {k_: v_ - col0_c for k_, v_ in cols_full.items()}
        ul3 = ul.reshape(bsz, n_lat, -1)
        uc3 = uc.reshape(bsz, n_ctx, -1)

        ssd_p = dict(
            conv_w=ssd_conv_w[li], conv_b=ssd_conv_b[li].reshape(1, SSD_CONV_CH),
            bias_row=jnp.zeros((1, LANES), F32).at[0, :N_DT].set(_dt_param_order(ssd_dt_bias[li])),
            bias_col=_dt_param_order(ssd_dt_bias[li]).reshape(N_DT, 1),
            alog_col=_dt_param_order(ssd_a_log[li]).reshape(N_DT, 1),
            alog_x=jnp.repeat(ssd_a_log[li].reshape(2, SSD_GROUPS, SSD_HEADS_PER_GROUP).transpose(1, 0, 2)
                              .reshape(SSD_GROUPS, 2 * SSD_HEADS_PER_GROUP), SSD_HEAD_DIM, axis=1)
            .reshape(SSD_GROUPS, 1, 2 * GROUP_X),
            dskip_x=jnp.repeat(ssd_d[li], SSD_HEAD_DIM).reshape(SSD_GROUPS, 1, GROUP_X),
        )
        ys_l, ys_c = _ssd(ul3, uc3, cols_full, cols_c, dt_l, dt_c, dtt_l, dtt_c, ssd_p)

        lam_init = 0.8 - 0.6 * math.exp(-0.3 * li)
        sw = att_subln_w[li].reshape(1, ATT_V_DIM)
        ya_l = _attention(ul3, COL_Q, uc3, cols_c, ul3, cols_full, rope_tabs, diff_lambda[li], sw,
                          lam_init=lam_init, tq=tq)

        w_bs = w_br_ssd[li].astype(BF16)
        w_ba = w_br_att[li].astype(BF16)
        w_o = w_out[li].astype(BF16)
        sn = ssd_norm_w[li].reshape(1, SSD_WIDTH)
        n2 = norm2_w[li].reshape(1, d)
        w_u = _ffn_interleave(w_up[li], tf).astype(BF16)
        cw = _ffn_interleave(ffn_conv_w[li], tf)
        cb = _ffn_interleave(ffn_conv_b[li], tf).reshape(1, 2 * D_FF)
        w_d = w_down[li].astype(BF16)
        fw = final_norm_w.reshape(1, d)

        if with_ctx:
            ya_c = _attention(uc3, cols_c["q"], uc3, cols_c, None, None, None, diff_lambda[li], sw,
                              lam_init=lam_init, tq=n_ctx)
            xc = _merge(ys_c.reshape(-1, SSD_WIDTH), uc, cols_c, ya_c.reshape(-1, ATT_WIDTH), xc, mod3,
                        ctx_rowf, sn, w_bs, w_ba, w_o, tm=tm_merge_c)
            xc = _conv_ffn(xc, mod3, ctx_rowf, n2, w_u, cw, cb, w_d, fw, tm=n_ctx, tf=tf,
                           seg_len=n_ctx, final_norm=False)

        xl = _merge(ys_l.reshape(-1, SSD_WIDTH), ul, cols_full, ya_l.reshape(-1, ATT_WIDTH), xl, mod3,
                    lat_row(tm_merge), sn, w_bs, w_ba, w_o, tm=tm_merge)
        xl = _conv_ffn(xl, mod3, lat_row(tm_ffn), n2, w_u, cw, cb, w_d, fw, tm=tm_ffn, tf=tf,
                       seg_len=n_lat, final_norm=not with_ctx)

    return xl.reshape(bsz, n_lat, d)
```

```python
import functools
import math

import jax
import jax.numpy as jnp
import numpy as np
from jax import lax
from jax.experimental import pallas as pl
from jax.experimental.pallas import tpu as pltpu

F32 = jnp.float32
BF16 = jnp.bfloat16

D_MODEL = 1024
EPS = 1e-6
N_MOD = 6
GRID_W = 64

SSD_HEADS = 16
SSD_HEAD_DIM = 64
SSD_WIDTH = SSD_HEADS * SSD_HEAD_DIM
SSD_GROUPS = 4
SSD_HEADS_PER_GROUP = SSD_HEADS // SSD_GROUPS
SSD_STATE = 128
SSD_CHUNK = 128
SSD_BC_WIDTH = SSD_GROUPS * SSD_STATE
SSD_CONV_CH = SSD_WIDTH + 2 * SSD_BC_WIDTH
GROUP_X = SSD_HEADS_PER_GROUP * SSD_HEAD_DIM
SSD_GROUP_CHUNKS = 4

ATT_HEADS = 8
ATT_HEAD_DIM = 64
ATT_V_DIM = 2 * ATT_HEAD_DIM
ATT_QK_WIDTH = ATT_HEADS * 2 * ATT_HEAD_DIM
ATT_WIDTH = ATT_HEADS * ATT_V_DIM
ATT_SCALE = ATT_HEAD_DIM ** -0.5
LOG2_E = math.log2(math.e)
ATT_SUB_ROWS = 128
ROPE_BASE = 10000.0
ROPE_PAIRS = ATT_HEAD_DIM // 4

D_FF = 2816
N_DT = 2 * SSD_HEADS

COL_Z = 0
COL_Q = COL_Z + SSD_WIDTH
COL_GS = COL_Q + ATT_QK_WIDTH
COL_GA = COL_GS + D_MODEL
COL_XS = COL_GA + D_MODEL
COL_B = COL_XS + SSD_WIDTH
COL_C = COL_B + SSD_BC_WIDTH
COL_K = COL_C + SSD_BC_WIDTH
COL_V = COL_K + ATT_QK_WIDTH
N_COLS = COL_V + ATT_WIDTH

LANES = 128
BF16_ROWS = 16
VMEM_LIMIT = 56 * 1024 * 1024


def _sigmoid(x):
    return 0.5 * jnp.tanh(0.5 * x) + 0.5


def _silu(x):
    return x * _sigmoid(x)


def _softplus(x):
    return jnp.maximum(x, 0.0) + jnp.log1p(jnp.exp(-jnp.abs(x)))


def _bdot(a, b):
    return jnp.dot(a.astype(BF16), b.astype(BF16), preferred_element_type=F32)


def _bdot_nt(a, b):
    return lax.dot_general(a.astype(BF16), b.astype(BF16), (((1,), (1,)), ((), ())),
                           preferred_element_type=F32)


def _split3(x):
    hi = x.astype(BF16)
    r1 = x - hi.astype(F32)
    mid = r1.astype(BF16)
    lo = (r1 - mid.astype(F32)).astype(BF16)
    return hi, mid, lo


def _dot3_rhs(a_bf16, x):
    hi, mid, lo = _split3(x)
    out = jnp.dot(a_bf16, hi, preferred_element_type=F32)
    out = out + jnp.dot(a_bf16, mid, preferred_element_type=F32)
    return out + jnp.dot(a_bf16, lo, preferred_element_type=F32)


def _dot2_lhs(x, b_bf16):
    hi = x.astype(BF16)
    lo = (x - hi.astype(F32)).astype(BF16)
    return (jnp.dot(hi, b_bf16, preferred_element_type=F32)
            + jnp.dot(lo, b_bf16, preferred_element_type=F32))


def _cparams(sem):
    return pltpu.CompilerParams(dimension_semantics=sem, vmem_limit_bytes=VMEM_LIMIT)


def _mod_kernel(c_ref, w_ref, b_ref, o_ref):
    a = _silu(c_ref[...])
    o_ref[...] = _dot3_both(a, w_ref[...]) + b_ref[...]


def _dot3_both(a, w):
    ah, am, al = _split3(a)
    wh, wm, wl = _split3(w)
    d = lambda p, q: jnp.dot(p, q, preferred_element_type=F32)
    return (d(ah, wh) + (d(ah, wm) + d(am, wh))
            + (d(ah, wl) + d(am, wm) + d(al, wh)))


def _modulation(cc, w_mod, b_mod):
    depth = w_mod.shape[0]
    tn = 1024
    return pl.pallas_call(
        _mod_kernel,
        grid=(depth, N_MOD * D_MODEL // tn),
        in_specs=[
            pl.BlockSpec((16, D_MODEL), lambda l, j: (0, 0)),
            pl.BlockSpec((None, D_MODEL, tn), lambda l, j: (l, 0, j)),
            pl.BlockSpec((None, 1, tn), lambda l, j: (l, 0, j)),
        ],
        out_specs=pl.BlockSpec((None, 16, tn), lambda l, j: (l, 0, j)),
        out_shape=jax.ShapeDtypeStruct((depth, 16, N_MOD * D_MODEL), F32),
        compiler_params=_cparams(("parallel", "parallel")),
        name="modulation",
    )(cc, w_mod, b_mod.reshape(depth, 1, N_MOD * D_MODEL))


def _in_proj_kernel(x_ref, sh_ref, sc_ref, nw_ref, w_ref, wdt_ref, wdtt_ref,
                    u_ref, dt_ref, dtt_ref, h_ref):
    @pl.when(pl.program_id(1) == 0)
    def _():
        x = x_ref[...]
        r = lax.rsqrt(jnp.mean(x * x, axis=-1, keepdims=True) + EPS)
        h = (x * r * nw_ref[...]) * (1.0 + sc_ref[0]) + sh_ref[0]
        hb = h.astype(BF16)
        h_ref[...] = hb
        dt_ref[...] = jnp.dot(hb, wdt_ref[...], preferred_element_type=F32)
        dtt_ref[...] = lax.dot_general(wdtt_ref[...], hb, (((1,), (1,)), ((), ())),
                                       preferred_element_type=F32)

    u_ref[...] = jnp.dot(h_ref[...], w_ref[...], preferred_element_type=F32).astype(BF16)


def _in_proj(x2d, mod3, mod_row, norm_w, w_main, w_dt, w_dtt, *, tm, tn, col0):
    rows = x2d.shape[0]
    n_out = w_main.shape[1] - col0
    cb0 = col0 // tn
    return pl.pallas_call(
        _in_proj_kernel,
        grid=(rows // tm, n_out // tn),
        in_specs=[
            pl.BlockSpec((tm, D_MODEL), lambda i, j: (i, 0)),
            pl.BlockSpec((1, 1, D_MODEL), lambda i, j: (mod_row(i), 0, 0)),
            pl.BlockSpec((1, 1, D_MODEL), lambda i, j: (mod_row(i), 0, 1)),
            pl.BlockSpec((1, D_MODEL), lambda i, j: (0, 0)),
            pl.BlockSpec((D_MODEL, tn), lambda i, j: (0, j + cb0)),
            pl.BlockSpec((D_MODEL, LANES), lambda i, j: (0, 0)),
            pl.BlockSpec((N_DT, D_MODEL), lambda i, j: (0, 0)),
        ],
        out_specs=[
            pl.BlockSpec((tm, tn), lambda i, j: (i, j)),
            pl.BlockSpec((tm, LANES), lambda i, j: (i, 0)),
            pl.BlockSpec((N_DT, tm), lambda i, j: (0, i)),
        ],
        out_shape=[
            jax.ShapeDtypeStruct((rows, n_out), BF16),
            jax.ShapeDtypeStruct((rows, LANES), F32),
            jax.ShapeDtypeStruct((N_DT, rows), F32),
        ],
        scratch_shapes=[pltpu.VMEM((tm, D_MODEL), BF16)],
        compiler_params=_cparams(("parallel", "arbitrary")),
        name="in_proj",
    )(x2d, mod3, mod3, norm_w, w_main, w_dt, w_dtt)


def _ssd_kernel(xl_ref, bl_ref, cl_ref, xc_ref, bc_ref, cc_ref,
                dtl_ref, dtc_ref, dttl_ref, dttc_ref,
                cwx_ref, cwb_ref, cwc_ref, cbx_ref, cbb_ref, cbc_ref,
                biasr_ref, biasc_ref, alogx_ref, alogc_ref, dskip_ref,
                yl_ref, yc_ref,
                cm_s, y_s, ex_s, st_s, dec_s, hin_s, *, n_lat, n_ctx):
    t = SSD_CHUNK
    g = pl.program_id(1)
    nc_ctx = n_ctx // t
    nc_lat = n_lat // t
    n_chunks = nc_ctx + nc_lat
    gx = GROUP_X

    ri = lax.broadcasted_iota(jnp.int32, (t, t), 0)
    ci = lax.broadcasted_iota(jnp.int32, (t, t), 1)
    lower = ri >= ci
    tril = lower.astype(BF16)
    triu = (ri <= ci).astype(BF16)
    er = lax.broadcasted_iota(jnp.int32, (LANES, 2 * gx), 0)
    ec = lax.broadcasted_iota(jnp.int32, (LANES, 2 * gx), 1)
    expand = (er == g * (2 * SSD_HEADS_PER_GROUP) + ec // SSD_HEAD_DIM).astype(BF16)
    head_of_lane = lax.broadcasted_iota(jnp.int32, (t, gx), 1) // SSD_HEAD_DIM
    tt = jnp.concatenate([triu, tril], axis=1)
    a_x = -jnp.exp(alogx_ref[0]) * LOG2_E
    a_c = -jnp.exp(alogc_ref[...]) * LOG2_E

    si = lax.broadcasted_iota(jnp.int32, (2 * t, t + 2 * BF16_ROWS), 0)
    sj = lax.broadcasted_iota(jnp.int32, (2 * t, t + 2 * BF16_ROWS), 1)
    src = jnp.where(si < t, jnp.where(si == 0, t + BF16_ROWS - 1, si - 1),
                    jnp.where(si == 2 * t - 1, t + BF16_ROWS, si - t + 1))
    shift_m = (sj == src).astype(BF16)
    conv_w = jnp.concatenate([cwx_ref[...], cwb_ref[...], cwc_ref[...]], axis=1)
    conv_b = jnp.concatenate([cbx_ref[...], cbb_ref[...], cbc_ref[...]], axis=1)

    def phase_a(cs, x_ref, b_ref, c_ref, dt_ref, dtt_ref, n_seg_chunks, chunk0):
        seg_rows = n_seg_chunks * t
        n8 = 2 * SSD_HEADS_PER_GROUP
        st = [dict(c=c, r0=pl.multiple_of(c * t, t), o0=pl.multiple_of((c + chunk0) * t, t)) for c in cs]

        for s in st:
            c, r0 = s["c"], s["r0"]
            lo = pl.multiple_of(jnp.maximum(r0 - BF16_ROWS, 0), BF16_ROWS)
            hi = pl.multiple_of(jnp.minimum(r0 + t, seg_rows - BF16_ROWS), BF16_ROWS)
            rows = lambda a, n: jnp.concatenate(
                [x_ref[0, pl.ds(a, n), :], b_ref[0, pl.ds(a, n), :], c_ref[0, pl.ds(a, n), :]], axis=1)
            cur = rows(r0, t)
            zero = jnp.zeros((BF16_ROWS, cur.shape[1]), BF16)
            before = jnp.where(c > 0, rows(lo, BF16_ROWS), zero)
            after = jnp.where(c < n_seg_chunks - 1, rows(hi, BF16_ROWS), zero)
            s["cur"] = cur
            s["sh"] = jnp.dot(shift_m, jnp.concatenate([cur, before, after], axis=0),
                              preferred_element_type=F32)
            dt = _softplus(dt_ref[0, pl.ds(r0, t), :] + biasr_ref[...])
            s["dtx"] = _dot2_lhs(dt, expand)
            at = _softplus(dtt_ref[:, pl.ds(r0, t)] + biasc_ref[...]) * a_c
            hi3, mid3, lo3 = _split3(at)
            s["ct3"] = jnp.dot(jnp.concatenate([hi3, mid3, lo3], axis=0), tt, preferred_element_type=F32)

        for s in st:
            ax = s["dtx"] * a_x
            s["csf"] = _dot3_rhs(tril, ax[:, :gx])
            s["csb"] = _dot3_rhs(triu, ax[:, gx:])
            sh = s["sh"]
            xbc = _silu(sh[:t] * conv_w[0:1] + s["cur"].astype(F32) * conv_w[1:2]
                        + sh[t:] * conv_w[2:3] + conv_b)
            s["xs"] = xbc[:, :gx]
            s["bm"] = xbc[:, gx:gx + SSD_STATE].astype(BF16)
            s["cm"] = xbc[:, gx + SSD_STATE:].astype(BF16)
            cm_s[pl.ds(s["o0"], t), :] = s["cm"]
            s["cb"] = _bdot_nt(s["cm"], s["bm"])

        for s in st:
            ct3 = s["ct3"]
            cst = ct3[0:n8] + ct3[n8:2 * n8] + ct3[2 * n8:3 * n8]
            xs, dtx, cb = s["xs"], s["dtx"], s["cb"]
            y = dskip_ref[0] * xs
            for d, cs_d in ((0, s["csf"]), (1, s["csb"])):
                mask = lower if d == 0 else (ri <= ci)
                xd = (xs * dtx[:, d * gx:(d + 1) * gx]).astype(BF16)
                ms, blocks = [], []
                for r in range(SSD_HEADS_PER_GROUP):
                    col = cs_d[:, r * SSD_HEAD_DIM:r * SSD_HEAD_DIM + 1]
                    k = d * SSD_HEADS_PER_GROUP + r
                    row = cst[k:k + 1, d * t:(d + 1) * t]
                    seg = jnp.exp2(jnp.where(mask, col - row, -jnp.inf))
                    ms.append((cb * seg).astype(BF16))
                    blocks.append(jnp.where(head_of_lane == r, xd, jnp.zeros_like(xd)))
                y = y + jnp.dot(jnp.concatenate(ms, axis=1), jnp.concatenate(blocks, axis=0),
                                preferred_element_type=F32)
            y_s[pl.ds(s["o0"], t), :] = y

        for s in st:
            xs, dtx, csf, csb = s["xs"], s["dtx"], s["csf"], s["csb"]
            endf = csf[t - 1:t, :]
            endb = csb[0:1, :]
            ex_s[pl.ds(s["o0"], t), :] = jnp.exp2(jnp.concatenate([csf, csb], axis=1))
            xdd = jnp.concatenate([xs * dtx[:, :gx] * jnp.exp2(endf - csf),
                                   xs * dtx[:, gx:] * jnp.exp2(endb - csb)], axis=1).astype(BF16)
            bt = jnp.transpose(s["bm"].astype(F32)).astype(BF16)
            st_s[s["c"] + chunk0] = jnp.dot(bt, xdd, preferred_element_type=F32)
            dec_s[pl.ds(s["c"] + chunk0, 1), :] = jnp.exp2(jnp.concatenate([endf, endb], axis=1))

    def run_phase_a(n_seg_chunks, chunk0, refs):
        group = math.gcd(n_seg_chunks, SSD_GROUP_CHUNKS)

        def body(i, carry):
            phase_a([i * group + k for k in range(group)], *refs, n_seg_chunks, chunk0)
            return carry

        lax.fori_loop(0, n_seg_chunks // group, body, 0)

    run_phase_a(nc_ctx, 0, (xc_ref, bc_ref, cc_ref, dtc_ref, dttc_ref))
    run_phase_a(nc_lat, nc_ctx, (xl_ref, bl_ref, cl_ref, dtl_ref, dttl_ref))

    order_f = list(range(n_chunks))
    order_b = list(range(nc_ctx - 1, -1, -1)) + list(range(n_chunks - 1, nc_ctx - 1, -1))
    for d, order in ((0, order_f), (1, order_b)):
        h = jnp.zeros((SSD_STATE, gx), F32)
        for c in order:
            hin_s[c, :, d * gx:(d + 1) * gx] = h.astype(BF16)
            h = dec_s[c:c + 1, d * gx:(d + 1) * gx] * h + st_s[c, :, d * gx:(d + 1) * gx]

    def phase_c(c, carry):
        o0 = pl.multiple_of(c * t, t)
        yo = jnp.dot(cm_s[pl.ds(o0, t), :], hin_s[c], preferred_element_type=F32) * ex_s[pl.ds(o0, t), :]
        y_s[pl.ds(o0, t), :] = y_s[pl.ds(o0, t), :] + yo[:, :gx] + yo[:, gx:]
        return carry

    lax.fori_loop(0, n_chunks, phase_c, 0, unroll=2)
    yc_ref[0] = y_s[0:n_ctx, :].astype(BF16)
    yl_ref[0] = y_s[n_ctx:n_ctx + n_lat, :].astype(BF16)


def _ssd(ul3, uc3, cols_l, cols_c, dt_l, dt_c, dtt_l, dtt_c, p):
    bsz, n_lat, _ = ul3.shape
    n_ctx = uc3.shape[1]
    gx = GROUP_X
    n_chunks = (n_lat + n_ctx) // SSD_CHUNK
    n_tot = n_lat + n_ctx

    def seq_specs(cols, n):
        xb, bb, cb = cols["xs"] // gx, cols["b"] // SSD_STATE, cols["c"] // SSD_STATE
        return [
            pl.BlockSpec((1, n, gx), lambda b, g: (b, 0, xb + g)),
            pl.BlockSpec((1, n, SSD_STATE), lambda b, g: (b, 0, bb + g)),
            pl.BlockSpec((1, n, SSD_STATE), lambda b, g: (b, 0, cb + g)),
        ]

    xoff = 0
    boff = SSD_WIDTH // SSD_STATE
    coff = (SSD_WIDTH + SSD_BC_WIDTH) // SSD_STATE
    n8 = 2 * SSD_HEADS_PER_GROUP
    in_specs = (
        seq_specs(cols_l, n_lat) + seq_specs(cols_c, n_ctx) + [
            pl.BlockSpec((1, n_lat, LANES), lambda b, g: (b, 0, 0)),
            pl.BlockSpec((1, n_ctx, LANES), lambda b, g: (b, 0, 0)),
            pl.BlockSpec((n8, n_lat), lambda b, g: (g, b)),
            pl.BlockSpec((n8, n_ctx), lambda b, g: (g, b)),
            pl.BlockSpec((3, gx), lambda b, g: (0, xoff + g)),
            pl.BlockSpec((3, SSD_STATE), lambda b, g: (0, boff + g)),
            pl.BlockSpec((3, SSD_STATE), lambda b, g: (0, coff + g)),
            pl.BlockSpec((1, gx), lambda b, g: (0, xoff + g)),
            pl.BlockSpec((1, SSD_STATE), lambda b, g: (0, boff + g)),
            pl.BlockSpec((1, SSD_STATE), lambda b, g: (0, coff + g)),
            pl.BlockSpec((1, LANES), lambda b, g: (0, 0)),
            pl.BlockSpec((n8, 1), lambda b, g: (g, 0)),
            pl.BlockSpec((1, 1, 2 * gx), lambda b, g: (g, 0, 0)),
            pl.BlockSpec((n8, 1), lambda b, g: (g, 0)),
            pl.BlockSpec((1, 1, gx), lambda b, g: (g, 0, 0)),
        ])
    return pl.pallas_call(
        functools.partial(_ssd_kernel, n_lat=n_lat, n_ctx=n_ctx),
        grid=(bsz, SSD_GROUPS),
        in_specs=in_specs,
        out_specs=[
            pl.BlockSpec((1, n_lat, gx), lambda b, g: (b, 0, g)),
            pl.BlockSpec((1, n_ctx, gx), lambda b, g: (b, 0, g)),
        ],
        out_shape=[
            jax.ShapeDtypeStruct((bsz, n_lat, SSD_WIDTH), BF16),
            jax.ShapeDtypeStruct((bsz, n_ctx, SSD_WIDTH), BF16),
        ],
        scratch_shapes=[
            pltpu.VMEM((n_tot, SSD_STATE), BF16),
            pltpu.VMEM((n_tot, gx), F32),
            pltpu.VMEM((n_tot, 2 * gx), F32),
            pltpu.VMEM((n_chunks, SSD_STATE, 2 * gx), F32),
            pltpu.VMEM((n_chunks + (-n_chunks) % 8, 2 * gx), F32),
            pltpu.VMEM((n_chunks, SSD_STATE, 2 * gx), BF16),
        ],
        compiler_params=_cparams(("parallel", "parallel")),
        name="ssd_scan",
    )(ul3, ul3, ul3, uc3, uc3, uc3,
      dt_l.reshape(bsz, n_lat, LANES), dt_c.reshape(bsz, n_ctx, LANES), dtt_l, dtt_c,
      p["conv_w"], p["conv_w"], p["conv_w"], p["conv_b"], p["conv_b"], p["conv_b"],
      p["bias_row"], p["bias_col"], p["alog_x"], p["alog_col"], p["dskip_x"])


def _rope(x, cos, sin_a, sin_b):
    return x * cos + pltpu.roll(x, LANES - ROPE_PAIRS, axis=1) * sin_a + pltpu.roll(x, ROPE_PAIRS, axis=1) * sin_b


def _attn_kernel(*refs, rope, n_ctx, n_lat, tq, lam_init):
    if n_lat:
        (q_ref, kc_ref, vc_ref, kl_ref, vl_ref, cos_ref, sa_ref, sb_ref, lam_ref, sw_ref,
         o_ref, k0_s, k1_s, v_s) = refs
    else:
        (q_ref, kc_ref, vc_ref, lam_ref, sw_ref, o_ref, k0_s, k1_s, v_s) = refs
    qi = pl.program_id(2)
    hd = ATT_HEAD_DIM

    @pl.when(qi == 0)
    def _():
        kc = kc_ref[0]
        k0_s[0:n_ctx, :] = kc[:, :hd]
        k1_s[0:n_ctx, :] = kc[:, hd:]
        v_s[0:n_ctx, :] = vc_ref[0]
        if n_lat:
            kl = kl_ref[0].astype(F32)
            if rope:
                kl = _rope(kl, cos_ref[...], sa_ref[...], sb_ref[...])
            kl = kl.astype(BF16)
            k0_s[n_ctx:n_ctx + n_lat, :] = kl[:, :hd]
            k1_s[n_ctx:n_ctx + n_lat, :] = kl[:, hd:]
            v_s[n_ctx:n_ctx + n_lat, :] = vl_ref[0]

    lf = lam_ref[...]
    lam = (jnp.exp(jnp.sum(lf[0:1] * lf[1:2], axis=-1, keepdims=True))
           - jnp.exp(jnp.sum(lf[2:3] * lf[3:4], axis=-1, keepdims=True)) + lam_init)

    q = q_ref[0].astype(F32)
    if rope:
        r0 = pl.multiple_of(qi * tq, tq)
        q = _rope(q, cos_ref[pl.ds(r0, tq), :], sa_ref[pl.ds(r0, tq), :], sb_ref[pl.ds(r0, tq), :])
    q = (q * (ATT_SCALE * LOG2_E)).astype(BF16)

    sub = min(ATT_SUB_ROWS, tq)
    chains = [(r, c) for r in range(tq // sub) for c in range(2)]

    def scores(r, c):
        qc = q[r * sub:(r + 1) * sub, c * hd:(c + 1) * hd]
        k_s = k0_s if c == 0 else k1_s
        return lax.dot_general(qc, k_s[...], (((1,), (1,)), ((), ())), preferred_element_type=F32)

    s_next = scores(*chains[0])
    outs = {}
    for n, (r, c) in enumerate(chains):
        s = s_next
        if n + 1 < len(chains):
            s_next = scores(*chains[n + 1])
        e = jnp.exp2(s - jnp.max(s, axis=-1, keepdims=True))
        inv = 1.0 / jnp.sum(e, axis=-1, keepdims=True)
        outs[c] = jnp.dot(e.astype(BF16), v_s[...], preferred_element_type=F32) * inv
        if c == 1:
            o = outs[0] - lam * outs[1]
            on = o * lax.rsqrt(jnp.mean(o * o, axis=-1, keepdims=True) + EPS) * sw_ref[...]
            o_ref[0, r * sub:(r + 1) * sub, :] = (on * (1.0 - lam_init)).astype(BF16)


def _attention(uq3, col_q, uc3, cols_c, ul3, cols_l, rope_tabs, lam_p, subln_w, *, lam_init, tq):
    bsz, n_q, _ = uq3.shape
    n_ctx = uc3.shape[1]
    n_lat = 0 if ul3 is None else ul3.shape[1]
    n_k = n_ctx + n_lat
    vd = ATT_V_DIM
    qb = col_q // vd
    in_specs = [
        pl.BlockSpec((1, tq, vd), lambda b, h, i: (b, i, qb + h)),
        pl.BlockSpec((1, n_ctx, vd), lambda b, h, i: (b, 0, cols_c["k"] // vd + h)),
        pl.BlockSpec((1, n_ctx, vd), lambda b, h, i: (b, 0, cols_c["v"] // vd + h)),
    ]
    args = [uq3, uc3, uc3]
    if n_lat:
        in_specs += [
            pl.BlockSpec((1, n_lat, vd), lambda b, h, i: (b, 0, cols_l["k"] // vd + h)),
            pl.BlockSpec((1, n_lat, vd), lambda b, h, i: (b, 0, cols_l["v"] // vd + h)),
            pl.BlockSpec((n_lat, vd), lambda b, h, i: (0, 0)),
            pl.BlockSpec((n_lat, vd), lambda b, h, i: (0, 0)),
            pl.BlockSpec((n_lat, vd), lambda b, h, i: (0, 0)),
        ]
        args += [ul3, ul3, *rope_tabs]
    in_specs += [
        pl.BlockSpec((4, ATT_HEAD_DIM), lambda b, h, i: (0, 0)),
        pl.BlockSpec((1, vd), lambda b, h, i: (0, 0)),
    ]
    args += [lam_p, subln_w]
    return pl.pallas_call(
        functools.partial(_attn_kernel, rope=bool(n_lat), n_ctx=n_ctx, n_lat=n_lat, tq=tq,
                          lam_init=lam_init),
        grid=(bsz, ATT_HEADS, n_q // tq),
        in_specs=in_specs,
        out_specs=pl.BlockSpec((1, tq, vd), lambda b, h, i: (b, i, h)),
        out_shape=jax.ShapeDtypeStruct((bsz, n_q, ATT_WIDTH), BF16),
        scratch_shapes=[
            pltpu.VMEM((n_k, ATT_HEAD_DIM), BF16),
            pltpu.VMEM((n_k, ATT_HEAD_DIM), BF16),
            pltpu.VMEM((n_k, vd), BF16),
        ],
        compiler_params=_cparams(("parallel", "parallel", "arbitrary")),
        name="diff_attn",
    )(*args)


def _merge_kernel(ys_ref, z_ref, ya_ref, gs_ref, ga_ref, x_ref, g1_ref, nw_ref,
                  wbs_ref, wba_ref, wo_ref, o_ref):
    yz = ys_ref[...].astype(F32) * _silu(z_ref[...].astype(F32))
    ysn = yz * lax.rsqrt(jnp.mean(yz * yz, axis=-1, keepdims=True) + EPS) * nw_ref[...]
    ts = jnp.dot(ysn.astype(BF16), wbs_ref[...], preferred_element_type=F32)
    ta = jnp.dot(ya_ref[...], wba_ref[...], preferred_element_type=F32)
    tmix = _sigmoid(gs_ref[...].astype(F32)) * ts + _sigmoid(ga_ref[...].astype(F32)) * ta
    o_ref[...] = x_ref[...] + g1_ref[0] * jnp.dot(tmix.astype(BF16), wo_ref[...],
                                                   preferred_element_type=F32)


def _merge(ys2, u2, cols, ya2, x2d, mod3, mod_row, norm_w, w_bs, w_ba, w_o, *, tm):
    rows = x2d.shape[0]
    d = D_MODEL
    row_blk = lambda cb: pl.BlockSpec((tm, d), lambda i: (i, cb))
    full = lambda shape: pl.BlockSpec(shape, lambda i: (0, 0))
    return pl.pallas_call(
        _merge_kernel,
        grid=(rows // tm,),
        in_specs=[
            row_blk(0), row_blk(cols["z"] // d), row_blk(0),
            row_blk(cols["gs"] // d), row_blk(cols["ga"] // d), row_blk(0),
            pl.BlockSpec((1, 1, d), lambda i: (mod_row(i), 0, 2)),
            full((1, d)), full((d, d)), full((d, d)), full((d, d)),
        ],
        out_specs=row_blk(0),
        out_shape=jax.ShapeDtypeStruct((rows, d), F32),
        compiler_params=_cparams(("parallel",)),
        name="branch_merge",
    )(ys2, u2, ya2, u2, u2, x2d, mod3, norm_w, w_bs, w_ba, w_o)


def _ffn_kernel(x_ref, xp_ref, xn_ref, sh_ref, sc_ref, g2_ref, nw_ref, wu_ref, cw_ref, cb_ref,
                wd_ref, fw_ref, o_ref, h_s, u_s, acc_s, *, tm, tf, tiles_per_seg, final_norm):
    i = pl.program_id(0)
    halo = BF16_ROWS
    seg_first = (i % tiles_per_seg) == 0
    seg_last = (i % tiles_per_seg) == tiles_per_seg - 1

    def norm_mod(x):
        r = lax.rsqrt(jnp.mean(x * x, axis=-1, keepdims=True) + EPS)
        return ((x * r * nw_ref[...]) * (1.0 + sc_ref[0]) + sh_ref[0]).astype(BF16)

    h_s[0:halo, :] = norm_mod(xp_ref[...])
    h_s[halo:halo + tm, :] = norm_mod(x_ref[...])
    h_s[halo + tm:2 * halo + tm, :] = norm_mod(xn_ref[...])
    def up_proj(j):
        us = u_s.at[j % 2]
        for half in range(2):
            c0 = half * D_FF + j * tf
            us[:, half * tf:(half + 1) * tf] = jnp.dot(h_s[...], wu_ref[:, c0:c0 + tf],
                                                       preferred_element_type=F32)
        us[halo - 1:halo, :] = jnp.where(seg_first, 0.0, us[halo - 1:halo, :])
        us[halo + tm:halo + tm + 1, :] = jnp.where(seg_last, 0.0, us[halo + tm:halo + tm + 1, :])

    n_f = D_FF // tf
    up_proj(0)
    for j in range(n_f):
        if j + 1 < n_f:
            up_proj(j + 1)
        us = u_s.at[j % 2]
        w = jnp.concatenate([cw_ref[:, j * tf:(j + 1) * tf],
                             cw_ref[:, D_FF + j * tf:D_FF + (j + 1) * tf]], axis=1)
        b = jnp.concatenate([cb_ref[:, j * tf:(j + 1) * tf],
                             cb_ref[:, D_FF + j * tf:D_FF + (j + 1) * tf]], axis=1)
        uc = (us[halo - 1:halo - 1 + tm, :] * w[0:1] + us[halo:halo + tm, :] * w[1:2]
              + us[halo + 1:halo + 1 + tm, :] * w[2:3] + b)
        act = (_silu(uc[:, :tf]) * uc[:, tf:]).astype(BF16)
        down = jnp.dot(act, wd_ref[j * tf:(j + 1) * tf, :], preferred_element_type=F32)
        if j == 0:
            acc_s[...] = down
        else:
            acc_s[...] += down
    y = x_ref[...] + g2_ref[0] * acc_s[...]
    if final_norm:
        y = y * lax.rsqrt(jnp.mean(y * y, axis=-1, keepdims=True) + EPS) * fw_ref[...]
    o_ref[...] = y


def _conv_ffn(x2d, mod3, mod_row, norm_w, w_up, conv_w, conv_b, w_down, final_w, *,
              tm, tf, seg_len, final_norm):
    rows = x2d.shape[0]
    d = D_MODEL
    halo = BF16_ROWS
    hb = tm // halo
    last_blk = rows // halo - 1
    assert seg_len % tm == 0
    resident = lambda shape: pl.BlockSpec(shape, lambda i: (0, 0), pipeline_mode=pl.Buffered(1))
    return pl.pallas_call(
        functools.partial(_ffn_kernel, tm=tm, tf=tf, tiles_per_seg=seg_len // tm, final_norm=final_norm),
        grid=(rows // tm,),
        in_specs=[
            pl.BlockSpec((tm, d), lambda i: (i, 0)),
            pl.BlockSpec((halo, d), lambda i: (jnp.maximum(i * hb - 1, 0), 0)),
            pl.BlockSpec((halo, d), lambda i: (jnp.minimum((i + 1) * hb, last_blk), 0)),
            pl.BlockSpec((1, 1, d), lambda i: (mod_row(i), 0, 3)),
            pl.BlockSpec((1, 1, d), lambda i: (mod_row(i), 0, 4)),
            pl.BlockSpec((1, 1, d), lambda i: (mod_row(i), 0, 5)),
            resident((1, d)),
            resident((d, 2 * D_FF)),
            resident((3, 2 * D_FF)),
            resident((1, 2 * D_FF)),
            resident((D_FF, d)),
            resident((1, d)),
        ],
        out_specs=pl.BlockSpec((tm, d), lambda i: (i, 0)),
        out_shape=jax.ShapeDtypeStruct((rows, d), F32),
        scratch_shapes=[
            pltpu.VMEM((tm + 2 * halo, d), BF16),
            pltpu.VMEM((2, tm + 2 * halo, 2 * tf), F32),
            pltpu.VMEM((tm, d), F32),
        ],
        compiler_params=_cparams(("parallel",)),
        name="conv_ffn",
    )(x2d, x2d, x2d, mod3, mod3, mod3, norm_w, w_up, conv_w, conv_b, w_down, final_w)


def _split_in_weight(w):
    sizes = (SSD_WIDTH, SSD_CONV_CH, N_DT, ATT_QK_WIDTH, ATT_QK_WIDTH, ATT_WIDTH, 2 * D_MODEL)
    starts = np.concatenate([[0], np.cumsum(sizes)])
    z, xbc, dt, q, k, v, gates = (w[:, int(starts[n]):int(starts[n + 1])] for n in range(7))
    main = jnp.concatenate([z, q, gates, xbc, k, v], axis=1).astype(BF16)
    return main, _dt_param_order(dt.reshape(-1, 2, SSD_HEADS))


def _dt_param_order(p2h):
    lead = p2h.shape[:-2]
    p = p2h.reshape(*lead, 2, SSD_GROUPS, SSD_HEADS_PER_GROUP)
    return jnp.swapaxes(p, -3, -2).reshape(*lead, N_DT)


def _rope_tables(n_tokens):
    rows = n_tokens // GRID_W
    inv_freq = ROPE_BASE ** (-jnp.arange(ROPE_PAIRS, dtype=F32) / ROPE_PAIRS)
    ang_r = jnp.broadcast_to(jnp.arange(rows, dtype=F32)[:, None, None] * inv_freq, (rows, GRID_W, ROPE_PAIRS))
    ang_c = jnp.broadcast_to(jnp.arange(GRID_W, dtype=F32)[None, :, None] * inv_freq, (rows, GRID_W, ROPE_PAIRS))
    ang = jnp.stack([ang_r, ang_c], axis=2).reshape(n_tokens, 2, 1, ROPE_PAIRS)
    cos = jnp.broadcast_to(jnp.cos(ang), (n_tokens, 2, 2, ROPE_PAIRS))
    sin = jnp.broadcast_to(jnp.sin(ang), (n_tokens, 2, 2, ROPE_PAIRS))
    zero = jnp.zeros_like(sin[:, :, :1])
    sin_a = jnp.concatenate([-sin[:, :, :1], zero], axis=2)
    sin_b = jnp.concatenate([zero, sin[:, :, 1:]], axis=2)
    tile = lambda a: jnp.tile(a.reshape(n_tokens, ATT_HEAD_DIM), (1, 2))
    return tile(cos), tile(sin_a), tile(sin_b)


def kernel(x, c, ctx, c_ctx, w_mod, b_mod, norm1_w, w_in, ssd_conv_w, ssd_conv_b, ssd_a_log,
           ssd_dt_bias, ssd_d, ssd_norm_w, diff_lambda, att_subln_w, w_br_ssd, w_br_att, w_out,
           norm2_w, w_up, ffn_conv_w, ffn_conv_b, w_down, final_norm_w):
    bsz, n_lat, d = x.shape
    n_ctx = ctx.shape[1]
    depth = w_mod.shape[0]
    assert d == D_MODEL and bsz + 1 <= 16
    ctx_row = bsz

    tm_in = min(1024, n_lat)
    tm_in_c = min(1024, bsz * n_ctx)
    tn_in = 2048
    tm_merge = min(512, n_lat)
    tm_merge_c = min(512, bsz * n_ctx)
    tm_ffn = min(512, n_lat)
    tf = 256
    tq = min(1024, n_lat)

    cc = jnp.zeros((16, d), F32).at[:bsz].set(c).at[ctx_row].set(c_ctx)
    mod = _modulation(cc, w_mod, b_mod)

    rope_tabs = _rope_tables(n_lat)
    cols_full = dict(z=COL_Z, q=COL_Q, gs=COL_GS, ga=COL_GA, xs=COL_XS, b=COL_B, c=COL_C, k=COL_K, v=COL_V)

    xl = x.reshape(bsz * n_lat, d)
    xc = ctx.reshape(bsz * n_ctx, d)
    lat_row = lambda tm: (lambda i: (i * tm) // n_lat)
    ctx_rowf = lambda i: ctx_row

    for li in range(depth):
        with_ctx = li < depth - 1
        mod3 = mod[li].reshape(16, 1, N_MOD * d)
        w_main, w_dt32 = _split_in_weight(w_in[li])
        w_dt = jnp.pad(w_dt32, ((0, 0), (0, LANES - N_DT))).astype(BF16)
        w_dtt = w_dt32.T.astype(BF16)
        n1 = norm1_w[li].reshape(1, d)

        ul, dt_l, dtt_l = _in_proj(xl, mod3, lat_row(tm_in), n1, w_main, w_dt, w_dtt,
                                   tm=tm_in, tn=tn_in, col0=0)
        col0_c = 0 if with_ctx else COL_XS
        uc, dt_c, dtt_c = _in_proj(xc, mod3, ctx_rowf, n1, w_main, w_dt, w_dtt,
                                   tm=tm_in_c, tn=tn_in, col0=col0_c)
        cols_c = {k_: v_ - col0_c for k_, v_ in cols_full.items()}
        ul3 = ul.reshape(bsz, n_lat, -1)
        uc3 = uc.reshape(bsz, n_ctx, -1)

        ssd_p = dict(
            conv_w=ssd_conv_w[li], conv_b=ssd_conv_b[li].reshape(1, SSD_CONV_CH),
            bias_row=jnp.zeros((1, LANES), F32).at[0, :N_DT].set(_dt_param_order(ssd_dt_bias[li])),
            bias_col=_dt_param_order(ssd_dt_bias[li]).reshape(N_DT, 1),
            alog_col=_dt_param_order(ssd_a_log[li]).reshape(N_DT, 1),
            alog_x=jnp.repeat(ssd_a_log[li].reshape(2, SSD_GROUPS, SSD_HEADS_PER_GROUP).transpose(1, 0, 2)
                              .reshape(SSD_GROUPS, 2 * SSD_HEADS_PER_GROUP), SSD_HEAD_DIM, axis=1)
            .reshape(SSD_GROUPS, 1, 2 * GROUP_X),
            dskip_x=jnp.repeat(ssd_d[li], SSD_HEAD_DIM).reshape(SSD_GROUPS, 1, GROUP_X),
        )
        ys_l, ys_c = _ssd(ul3, uc3, cols_full, cols_c, dt_l, dt_c, dtt_l, dtt_c, ssd_p)

        lam_init = 0.8 - 0.6 * math.exp(-0.3 * li)
        sw = att_subln_w[li].reshape(1, ATT_V_DIM)
        ya_l = _attention(ul3, COL_Q, uc3, cols_c, ul3, cols_full, rope_tabs, diff_lambda[li], sw,
                          lam_init=lam_init, tq=tq)

        w_bs = w_br_ssd[li].astype(BF16)
        w_ba = w_br_att[li].astype(BF16)
        w_o = w_out[li].astype(BF16)
        sn = ssd_norm_w[li].reshape(1, SSD_WIDTH)
        n2 = norm2_w[li].reshape(1, d)
        w_u = w_up[li].astype(BF16)
        cw = ffn_conv_w[li]
        cb = ffn_conv_b[li].reshape(1, 2 * D_FF)
        w_d = w_down[li].astype(BF16)
        fw = final_norm_w.reshape(1, d)

        if with_ctx:
            ya_c = _attention(uc3, cols_c["q"], uc3, cols_c, None, None, None, diff_lambda[li], sw,
                              lam_init=lam_init, tq=n_ctx)
            xc = _merge(ys_c.reshape(-1, SSD_WIDTH), uc, cols_c, ya_c.reshape(-1, ATT_WIDTH), xc, mod3,
                        ctx_rowf, sn, w_bs, w_ba, w_o, tm=tm_merge_c)
            xc = _conv_ffn(xc, mod3, ctx_rowf, n2, w_u, cw, cb, w_d, fw, tm=n_ctx, tf=tf,
                           seg_len=n_ctx, final_norm=False)

        xl = _merge(ys_l.reshape(-1, SSD_WIDTH), ul, cols_full, ya_l.reshape(-1, ATT_WIDTH), xl, mod3,
                    lat_row(tm_merge), sn, w_bs, w_ba, w_o, tm=tm_merge)
        xl = _conv_ffn(xl, mod3, lat_row(tm_ffn), n2, w_u, cw, cb, w_d, fw, tm=tm_ffn, tf=tf,
                       seg_len=n_lat, final_norm=not with_ctx)

    return xl.reshape(bsz, n_lat, d)
```

```python
import functools
import math

import jax
import jax.numpy as jnp
import numpy as np
from jax import lax
from jax.experimental import pallas as pl
from jax.experimental.pallas import tpu as pltpu

F32 = jnp.float32
BF16 = jnp.bfloat16

D_MODEL = 1024
EPS = 1e-6
N_MOD = 6
GRID_W = 64

SSD_HEADS = 16
SSD_HEAD_DIM = 64
SSD_WIDTH = SSD_HEADS * SSD_HEAD_DIM
SSD_GROUPS = 4
SSD_HEADS_PER_GROUP = SSD_HEADS // SSD_GROUPS
SSD_STATE = 128
SSD_CHUNK = 128
SSD_BC_WIDTH = SSD_GROUPS * SSD_STATE
SSD_CONV_CH = SSD_WIDTH + 2 * SSD_BC_WIDTH
GROUP_X = SSD_HEADS_PER_GROUP * SSD_HEAD_DIM
SSD_GROUP_CHUNKS = 8

ATT_HEADS = 8
ATT_HEAD_DIM = 64
ATT_V_DIM = 2 * ATT_HEAD_DIM
ATT_QK_WIDTH = ATT_HEADS * 2 * ATT_HEAD_DIM
ATT_WIDTH = ATT_HEADS * ATT_V_DIM
ATT_SCALE = ATT_HEAD_DIM ** -0.5
LOG2_E = math.log2(math.e)
ATT_SUB_ROWS = 128
ATT_CHAINS_AHEAD = 2
ROPE_BASE = 10000.0
ROPE_PAIRS = ATT_HEAD_DIM // 4

D_FF = 2816
N_DT = 2 * SSD_HEADS

COL_Z = 0
COL_Q = COL_Z + SSD_WIDTH
COL_GS = COL_Q + ATT_QK_WIDTH
COL_GA = COL_GS + D_MODEL
COL_XS = COL_GA + D_MODEL
COL_B = COL_XS + SSD_WIDTH
COL_C = COL_B + SSD_BC_WIDTH
COL_K = COL_C + SSD_BC_WIDTH
COL_V = COL_K + ATT_QK_WIDTH
N_COLS = COL_V + ATT_WIDTH

LANES = 128
BF16_ROWS = 16
VMEM_LIMIT = 56 * 1024 * 1024


def _sigmoid(x):
    return 0.5 * jnp.tanh(0.5 * x) + 0.5


def _silu(x):
    return x * _sigmoid(x)


def _softplus(x):
    return jnp.maximum(x, 0.0) + jnp.log1p(jnp.exp(-jnp.abs(x)))


def _bdot(a, b):
    return jnp.dot(a.astype(BF16), b.astype(BF16), preferred_element_type=F32)


def _bdot_nt(a, b):
    return lax.dot_general(a.astype(BF16), b.astype(BF16), (((1,), (1,)), ((), ())),
                           preferred_element_type=F32)


def _split3(x):
    hi = x.astype(BF16)
    r1 = x - hi.astype(F32)
    mid = r1.astype(BF16)
    lo = (r1 - mid.astype(F32)).astype(BF16)
    return hi, mid, lo


def _dot2_rhs(a_bf16, x):
    hi = x.astype(BF16)
    lo = (x - hi.astype(F32)).astype(BF16)
    return (jnp.dot(a_bf16, hi, preferred_element_type=F32)
            + jnp.dot(a_bf16, lo, preferred_element_type=F32))


def _dot2_lhs(x, b_bf16):
    hi = x.astype(BF16)
    lo = (x - hi.astype(F32)).astype(BF16)
    return (jnp.dot(hi, b_bf16, preferred_element_type=F32)
            + jnp.dot(lo, b_bf16, preferred_element_type=F32))


def _cparams(sem):
    return pltpu.CompilerParams(dimension_semantics=sem, vmem_limit_bytes=VMEM_LIMIT)


def _mod_kernel(c_ref, w_ref, b_ref, o_ref):
    a = _silu(c_ref[...])
    o_ref[...] = _dot3_both(a, w_ref[...]) + b_ref[...]


def _dot3_both(a, w):
    ah, am, al = _split3(a)
    wh, wm, wl = _split3(w)
    d = lambda p, q: jnp.dot(p, q, preferred_element_type=F32)
    return (d(ah, wh) + (d(ah, wm) + d(am, wh))
            + (d(ah, wl) + d(am, wm) + d(al, wh)))


def _modulation(cc, w_mod, b_mod):
    depth = w_mod.shape[0]
    tn = 1024
    return pl.pallas_call(
        _mod_kernel,
        grid=(depth, N_MOD * D_MODEL // tn),
        in_specs=[
            pl.BlockSpec((16, D_MODEL), lambda l, j: (0, 0)),
            pl.BlockSpec((None, D_MODEL, tn), lambda l, j: (l, 0, j)),
            pl.BlockSpec((None, 1, tn), lambda l, j: (l, 0, j)),
        ],
        out_specs=pl.BlockSpec((None, 16, tn), lambda l, j: (l, 0, j)),
        out_shape=jax.ShapeDtypeStruct((depth, 16, N_MOD * D_MODEL), F32),
        compiler_params=_cparams(("parallel", "parallel")),
        name="modulation",
    )(cc, w_mod, b_mod.reshape(depth, 1, N_MOD * D_MODEL))


def _rope(x, cos, sin_a, sin_b):
    return x * cos + pltpu.roll(x, LANES - ROPE_PAIRS, axis=1) * sin_a + pltpu.roll(x, ROPE_PAIRS, axis=1) * sin_b


def _in_proj_kernel(x_ref, sh_ref, sc_ref, nw_ref, w_ref, wdt_ref, wdtt_ref,
                    u_ref, dt_ref, dtt_ref, h_ref):
    @pl.when(pl.program_id(1) == 0)
    def _():
        x = x_ref[...]
        r = lax.rsqrt(jnp.mean(x * x, axis=-1, keepdims=True) + EPS)
        h = (x * r * nw_ref[...]) * (1.0 + sc_ref[0]) + sh_ref[0]
        hb = h.astype(BF16)
        h_ref[...] = hb
        dt_ref[...] = jnp.dot(hb, wdt_ref[...], preferred_element_type=F32)
        dtt_ref[...] = lax.dot_general(wdtt_ref[...], hb, (((1,), (1,)), ((), ())),
                                       preferred_element_type=F32)

    u_ref[...] = jnp.dot(h_ref[...], w_ref[...], preferred_element_type=F32).astype(BF16)


def _in_proj(x2d, mod3, mod_row, norm_w, w_main, w_dt, w_dtt, *, tm, tn, col0):
    rows = x2d.shape[0]
    n_out = w_main.shape[1] - col0
    cb0 = col0 // tn
    return pl.pallas_call(
        _in_proj_kernel,
        grid=(rows // tm, n_out // tn),
        in_specs=[
            pl.BlockSpec((tm, D_MODEL), lambda i, j: (i, 0)),
            pl.BlockSpec((1, 1, D_MODEL), lambda i, j: (mod_row(i), 0, 0)),
            pl.BlockSpec((1, 1, D_MODEL), lambda i, j: (mod_row(i), 0, 1)),
            pl.BlockSpec((1, D_MODEL), lambda i, j: (0, 0)),
            pl.BlockSpec((D_MODEL, tn), lambda i, j: (0, j + cb0)),
            pl.BlockSpec((D_MODEL, LANES), lambda i, j: (0, 0)),
            pl.BlockSpec((N_DT, D_MODEL), lambda i, j: (0, 0)),
        ],
        out_specs=[
            pl.BlockSpec((tm, tn), lambda i, j: (i, j)),
            pl.BlockSpec((tm, LANES), lambda i, j: (i, 0)),
            pl.BlockSpec((N_DT, tm), lambda i, j: (0, i)),
        ],
        out_shape=[
            jax.ShapeDtypeStruct((rows, n_out), BF16),
            jax.ShapeDtypeStruct((rows, LANES), F32),
            jax.ShapeDtypeStruct((N_DT, rows), F32),
        ],
        scratch_shapes=[pltpu.VMEM((tm, D_MODEL), BF16)],
        compiler_params=_cparams(("parallel", "arbitrary")),
        name="in_proj",
    )(x2d, mod3, mod3, norm_w, w_main, w_dt, w_dtt)


def _ssd_kernel(xl_ref, bl_ref, cl_ref, xc_ref, bc_ref, cc_ref,
                dtl_ref, dtc_ref, dttl_ref, dttc_ref,
                cwx_ref, cwb_ref, cwc_ref, cbx_ref, cbb_ref, cbc_ref,
                biasr_ref, biasc_ref, alogx_ref, alogc_ref, dskip_ref,
                yl_ref, yc_ref,
                cm_s, y_s, ex_s, st_s, dec_s, hin_s, *, n_lat, n_ctx):
    t = SSD_CHUNK
    g = pl.program_id(1)
    nc_ctx = n_ctx // t
    nc_lat = n_lat // t
    n_chunks = nc_ctx + nc_lat
    gx = GROUP_X

    ri = lax.broadcasted_iota(jnp.int32, (t, t), 0)
    ci = lax.broadcasted_iota(jnp.int32, (t, t), 1)
    lower = ri >= ci
    tril = lower.astype(BF16)
    triu = (ri <= ci).astype(BF16)
    er = lax.broadcasted_iota(jnp.int32, (LANES, 2 * gx), 0)
    ec = lax.broadcasted_iota(jnp.int32, (LANES, 2 * gx), 1)
    expand = (er == g * (2 * SSD_HEADS_PER_GROUP) + ec // SSD_HEAD_DIM).astype(BF16)
    head_of_lane = lax.broadcasted_iota(jnp.int32, (t, gx), 1) // SSD_HEAD_DIM
    tt = jnp.concatenate([triu, tril], axis=1)
    a_x = -jnp.exp(alogx_ref[0]) * LOG2_E
    a_c = -jnp.exp(alogc_ref[...]) * LOG2_E

    si = lax.broadcasted_iota(jnp.int32, (2 * t, t + 2 * BF16_ROWS), 0)
    sj = lax.broadcasted_iota(jnp.int32, (2 * t, t + 2 * BF16_ROWS), 1)
    src = jnp.where(si < t, jnp.where(si == 0, t + BF16_ROWS - 1, si - 1),
                    jnp.where(si == 2 * t - 1, t + BF16_ROWS, si - t + 1))
    shift_m = (sj == src).astype(BF16)
    conv_w = jnp.concatenate([cwx_ref[...], cwb_ref[...], cwc_ref[...]], axis=1)
    conv_b = jnp.concatenate([cbx_ref[...], cbb_ref[...], cbc_ref[...]], axis=1)

    def phase_a(cs, x_ref, b_ref, c_ref, dt_ref, dtt_ref, n_seg_chunks, chunk0):
        seg_rows = n_seg_chunks * t
        n8 = 2 * SSD_HEADS_PER_GROUP
        st = [dict(c=c, r0=pl.multiple_of(c * t, t), o0=pl.multiple_of((c + chunk0) * t, t)) for c in cs]

        for s in st:
            c, r0 = s["c"], s["r0"]
            lo = pl.multiple_of(jnp.maximum(r0 - BF16_ROWS, 0), BF16_ROWS)
            hi = pl.multiple_of(jnp.minimum(r0 + t, seg_rows - BF16_ROWS), BF16_ROWS)
            rows = lambda a, n: jnp.concatenate(
                [x_ref[0, pl.ds(a, n), :], b_ref[0, pl.ds(a, n), :], c_ref[0, pl.ds(a, n), :]], axis=1)
            cur = rows(r0, t)
            zero = jnp.zeros((BF16_ROWS, cur.shape[1]), BF16)
            before = jnp.where(c > 0, rows(lo, BF16_ROWS), zero)
            after = jnp.where(c < n_seg_chunks - 1, rows(hi, BF16_ROWS), zero)
            s["cur"] = cur
            s["sh"] = jnp.dot(shift_m, jnp.concatenate([cur, before, after], axis=0),
                              preferred_element_type=F32)
            dt = _softplus(dt_ref[0, pl.ds(r0, t), :] + biasr_ref[...])
            s["dtx"] = _dot2_lhs(dt, expand)
            at = _softplus(dtt_ref[:, pl.ds(r0, t)] + biasc_ref[...]) * a_c
            hi3, mid3, lo3 = _split3(at)
            s["ct3"] = jnp.dot(jnp.concatenate([hi3, mid3, lo3], axis=0), tt, preferred_element_type=F32)

        for s in st:
            ax = s["dtx"] * a_x
            s["csf"] = _dot2_rhs(tril, ax[:, :gx])
            s["csb"] = _dot2_rhs(triu, ax[:, gx:])
            sh = s["sh"]
            xbc = _silu(sh[:t] * conv_w[0:1] + s["cur"].astype(F32) * conv_w[1:2]
                        + sh[t:] * conv_w[2:3] + conv_b)
            s["xs"] = xbc[:, :gx]
            s["bm"] = xbc[:, gx:gx + SSD_STATE].astype(BF16)
            s["cm"] = xbc[:, gx + SSD_STATE:].astype(BF16)
            cm_s[pl.ds(s["o0"], t), :] = s["cm"]
            s["cb"] = _bdot_nt(s["cm"], s["bm"])

        for s in st:
            ct3 = s["ct3"]
            cst = ct3[0:n8] + ct3[n8:2 * n8] + ct3[2 * n8:3 * n8]
            xs, dtx, cb = s["xs"], s["dtx"], s["cb"]
            y = dskip_ref[0] * xs
            for d, cs_d in ((0, s["csf"]), (1, s["csb"])):
                mask = lower if d == 0 else (ri <= ci)
                xd = (xs * dtx[:, d * gx:(d + 1) * gx]).astype(BF16)
                ms, blocks = [], []
                for r in range(SSD_HEADS_PER_GROUP):
                    col = cs_d[:, r * SSD_HEAD_DIM:r * SSD_HEAD_DIM + 1]
                    k = d * SSD_HEADS_PER_GROUP + r
                    row = cst[k:k + 1, d * t:(d + 1) * t]
                    seg = jnp.exp2(jnp.where(mask, col - row, -jnp.inf))
                    ms.append((cb * seg).astype(BF16))
                    blocks.append(jnp.where(head_of_lane == r, xd, jnp.zeros_like(xd)))
                y = y + jnp.dot(jnp.concatenate(ms, axis=1), jnp.concatenate(blocks, axis=0),
                                preferred_element_type=F32)
            y_s[pl.ds(s["o0"], t), :] = y

        for s in st:
            xs, dtx, csf, csb = s["xs"], s["dtx"], s["csf"], s["csb"]
            endf = csf[t - 1:t, :]
            endb = csb[0:1, :]
            ex_s[pl.ds(s["o0"], t), :] = jnp.exp2(jnp.concatenate([csf, csb], axis=1))
            xdd = jnp.concatenate([xs * dtx[:, :gx] * jnp.exp2(endf - csf),
                                   xs * dtx[:, gx:] * jnp.exp2(endb - csb)], axis=1).astype(BF16)
            bt = jnp.transpose(s["bm"].astype(F32)).astype(BF16)
            st_s[s["c"] + chunk0] = jnp.dot(bt, xdd, preferred_element_type=F32)
            dec_s[pl.ds(s["c"] + chunk0, 1), :] = jnp.exp2(jnp.concatenate([endf, endb], axis=1))

    def run_phase_a(n_seg_chunks, chunk0, refs):
        group = math.gcd(n_seg_chunks, SSD_GROUP_CHUNKS)

        def body(i, carry):
            phase_a([i * group + k for k in range(group)], *refs, n_seg_chunks, chunk0)
            return carry

        lax.fori_loop(0, n_seg_chunks // group, body, 0)

    run_phase_a(nc_ctx, 0, (xc_ref, bc_ref, cc_ref, dtc_ref, dttc_ref))
    run_phase_a(nc_lat, nc_ctx, (xl_ref, bl_ref, cl_ref, dtl_ref, dttl_ref))

    order_f = list(range(n_chunks))
    order_b = list(range(nc_ctx - 1, -1, -1)) + list(range(n_chunks - 1, nc_ctx - 1, -1))
    for d, order in ((0, order_f), (1, order_b)):
        h = jnp.zeros((SSD_STATE, gx), F32)
        for c in order:
            hin_s[c, :, d * gx:(d + 1) * gx] = h.astype(BF16)
            h = dec_s[c:c + 1, d * gx:(d + 1) * gx] * h + st_s[c, :, d * gx:(d + 1) * gx]

    def phase_c(c, carry):
        o0 = pl.multiple_of(c * t, t)
        yo = jnp.dot(cm_s[pl.ds(o0, t), :], hin_s[c], preferred_element_type=F32) * ex_s[pl.ds(o0, t), :]
        y_s[pl.ds(o0, t), :] = y_s[pl.ds(o0, t), :] + yo[:, :gx] + yo[:, gx:]
        return carry

    lax.fori_loop(0, n_chunks, phase_c, 0, unroll=2)
    yc_ref[0] = y_s[0:n_ctx, :].astype(BF16)
    yl_ref[0] = y_s[n_ctx:n_ctx + n_lat, :].astype(BF16)


def _ssd(ul3, uc3, cols_l, cols_c, dt_l, dt_c, dtt_l, dtt_c, p):
    bsz, n_lat, _ = ul3.shape
    n_ctx = uc3.shape[1]
    gx = GROUP_X
    n_chunks = (n_lat + n_ctx) // SSD_CHUNK
    n_tot = n_lat + n_ctx

    def seq_specs(cols, n):
        xb, bb, cb = cols["xs"] // gx, cols["b"] // SSD_STATE, cols["c"] // SSD_STATE
        return [
            pl.BlockSpec((1, n, gx), lambda b, g: (b, 0, xb + g)),
            pl.BlockSpec((1, n, SSD_STATE), lambda b, g: (b, 0, bb + g)),
            pl.BlockSpec((1, n, SSD_STATE), lambda b, g: (b, 0, cb + g)),
        ]

    xoff = 0
    boff = SSD_WIDTH // SSD_STATE
    coff = (SSD_WIDTH + SSD_BC_WIDTH) // SSD_STATE
    n8 = 2 * SSD_HEADS_PER_GROUP
    in_specs = (
        seq_specs(cols_l, n_lat) + seq_specs(cols_c, n_ctx) + [
            pl.BlockSpec((1, n_lat, LANES), lambda b, g: (b, 0, 0)),
            pl.BlockSpec((1, n_ctx, LANES), lambda b, g: (b, 0, 0)),
            pl.BlockSpec((n8, n_lat), lambda b, g: (g, b)),
            pl.BlockSpec((n8, n_ctx), lambda b, g: (g, b)),
            pl.BlockSpec((3, gx), lambda b, g: (0, xoff + g)),
            pl.BlockSpec((3, SSD_STATE), lambda b, g: (0, boff + g)),
            pl.BlockSpec((3, SSD_STATE), lambda b, g: (0, coff + g)),
            pl.BlockSpec((1, gx), lambda b, g: (0, xoff + g)),
            pl.BlockSpec((1, SSD_STATE), lambda b, g: (0, boff + g)),
            pl.BlockSpec((1, SSD_STATE), lambda b, g: (0, coff + g)),
            pl.BlockSpec((1, LANES), lambda b, g: (0, 0)),
            pl.BlockSpec((n8, 1), lambda b, g: (g, 0)),
            pl.BlockSpec((1, 1, 2 * gx), lambda b, g: (g, 0, 0)),
            pl.BlockSpec((n8, 1), lambda b, g: (g, 0)),
            pl.BlockSpec((1, 1, gx), lambda b, g: (g, 0, 0)),
        ])
    return pl.pallas_call(
        functools.partial(_ssd_kernel, n_lat=n_lat, n_ctx=n_ctx),
        grid=(bsz, SSD_GROUPS),
        in_specs=in_specs,
        out_specs=[
            pl.BlockSpec((1, n_lat, gx), lambda b, g: (b, 0, g)),
            pl.BlockSpec((1, n_ctx, gx), lambda b, g: (b, 0, g)),
        ],
        out_shape=[
            jax.ShapeDtypeStruct((bsz, n_lat, SSD_WIDTH), BF16),
            jax.ShapeDtypeStruct((bsz, n_ctx, SSD_WIDTH), BF16),
        ],
        scratch_shapes=[
            pltpu.VMEM((n_tot, SSD_STATE), BF16),
            pltpu.VMEM((n_tot, gx), F32),
            pltpu.VMEM((n_tot, 2 * gx), F32),
            pltpu.VMEM((n_chunks, SSD_STATE, 2 * gx), F32),
            pltpu.VMEM((n_chunks + (-n_chunks) % 8, 2 * gx), F32),
            pltpu.VMEM((n_chunks, SSD_STATE, 2 * gx), BF16),
        ],
        compiler_params=_cparams(("parallel", "parallel")),
        name="ssd_scan",
    )(ul3, ul3, ul3, uc3, uc3, uc3,
      dt_l.reshape(bsz, n_lat, LANES), dt_c.reshape(bsz, n_ctx, LANES), dtt_l, dtt_c,
      p["conv_w"], p["conv_w"], p["conv_w"], p["conv_b"], p["conv_b"], p["conv_b"],
      p["bias_row"], p["bias_col"], p["alog_x"], p["alog_col"], p["dskip_x"])


def _attn_kernel(*refs, n_ctx, n_lat, tq, lam_init):
    if n_lat:
        (q_ref, kc_ref, vc_ref, kl_ref, vl_ref, cos_ref, sa_ref, sb_ref, lam_ref, sw_ref,
         o_ref, kl_s) = refs
    else:
        q_ref, kc_ref, vc_ref, lam_ref, sw_ref, o_ref = refs
    qi = pl.program_id(2)
    hd = ATT_HEAD_DIM

    if n_lat:
        @pl.when(qi == 0)
        def _():
            kl_s[...] = _rope(kl_ref[0].astype(F32), cos_ref[...], sa_ref[...], sb_ref[...]).astype(BF16)

    lf = lam_ref[...]
    lam = (jnp.exp(jnp.sum(lf[0:1] * lf[1:2], axis=-1, keepdims=True))
           - jnp.exp(jnp.sum(lf[2:3] * lf[3:4], axis=-1, keepdims=True)) + lam_init)

    q = q_ref[0].astype(F32)
    if n_lat:
        r0 = pl.multiple_of(qi * tq, tq)
        q = _rope(q, cos_ref[pl.ds(r0, tq), :], sa_ref[pl.ds(r0, tq), :], sb_ref[pl.ds(r0, tq), :])
    q = (q * (ATT_SCALE * LOG2_E)).astype(BF16)
    lane = lax.broadcasted_iota(jnp.int32, (1, ATT_V_DIM), 1)
    q_comp = [jnp.where(lane < hd, q, jnp.zeros_like(q)), jnp.where(lane >= hd, q, jnp.zeros_like(q))]

    sub = min(ATT_SUB_ROWS, tq)
    chains = [(r, c) for r in range(tq // sub) for c in range(2)]
    nt = (((1,), (1,)), ((), ()))

    def scores(r, c):
        qc = q_comp[c][r * sub:(r + 1) * sub, :]
        s = lax.dot_general(qc, kc_ref[0], nt, preferred_element_type=F32)
        if n_lat:
            s = jnp.concatenate([s, lax.dot_general(qc, kl_s[...], nt, preferred_element_type=F32)],
                                axis=1)
        return s

    def weighted_values(e):
        pv = jnp.dot(e[:, :n_ctx], vc_ref[0], preferred_element_type=F32)
        if n_lat:
            pv = pv + jnp.dot(e[:, n_ctx:], vl_ref[0], preferred_element_type=F32)
        return pv

    ahead = ATT_CHAINS_AHEAD
    pending = [scores(*ch) for ch in chains[:ahead]]
    outs = {}
    for n, (r, c) in enumerate(chains):
        s = pending.pop(0)
        if n + ahead < len(chains):
            pending.append(scores(*chains[n + ahead]))
        e = jnp.exp2(s - jnp.max(s, axis=-1, keepdims=True))
        inv = 1.0 / jnp.sum(e, axis=-1, keepdims=True)
        outs[c] = weighted_values(e.astype(BF16)) * inv
        if c == 1:
            o = outs[0] - lam * outs[1]
            on = o * lax.rsqrt(jnp.mean(o * o, axis=-1, keepdims=True) + EPS) * sw_ref[...]
            o_ref[0, r * sub:(r + 1) * sub, :] = (on * (1.0 - lam_init)).astype(BF16)


def _attention(uq3, col_q, uc3, cols_c, ul3, cols_l, rope_tabs, lam_p, subln_w, *, lam_init, tq):
    bsz, n_q, _ = uq3.shape
    n_ctx = uc3.shape[1]
    n_lat = 0 if ul3 is None else ul3.shape[1]
    vd = ATT_V_DIM
    qb = col_q // vd
    in_specs = [
        pl.BlockSpec((1, tq, vd), lambda b, h, i: (b, i, qb + h)),
        pl.BlockSpec((1, n_ctx, vd), lambda b, h, i: (b, 0, cols_c["k"] // vd + h)),
        pl.BlockSpec((1, n_ctx, vd), lambda b, h, i: (b, 0, cols_c["v"] // vd + h)),
    ]
    args = [uq3, uc3, uc3]
    if n_lat:
        in_specs += [
            pl.BlockSpec((1, n_lat, vd), lambda b, h, i: (b, 0, cols_l["k"] // vd + h)),
            pl.BlockSpec((1, n_lat, vd), lambda b, h, i: (b, 0, cols_l["v"] // vd + h)),
            pl.BlockSpec((n_lat, vd), lambda b, h, i: (0, 0)),
            pl.BlockSpec((n_lat, vd), lambda b, h, i: (0, 0)),
            pl.BlockSpec((n_lat, vd), lambda b, h, i: (0, 0)),
        ]
        args += [ul3, ul3, *rope_tabs]
    in_specs += [
        pl.BlockSpec((4, ATT_HEAD_DIM), lambda b, h, i: (0, 0)),
        pl.BlockSpec((1, vd), lambda b, h, i: (0, 0)),
    ]
    args += [lam_p, subln_w]
    return pl.pallas_call(
        functools.partial(_attn_kernel, n_ctx=n_ctx, n_lat=n_lat, tq=tq, lam_init=lam_init),
        grid=(bsz, ATT_HEADS, n_q // tq),
        in_specs=in_specs,
        out_specs=pl.BlockSpec((1, tq, vd), lambda b, h, i: (b, i, h)),
        out_shape=jax.ShapeDtypeStruct((bsz, n_q, ATT_WIDTH), BF16),
        scratch_shapes=[pltpu.VMEM((n_lat, vd), BF16)] if n_lat else [],
        compiler_params=_cparams(("parallel", "parallel", "arbitrary")),
        name="diff_attn",
    )(*args)


def _merge_kernel(ys_ref, z_ref, ya_ref, gs_ref, ga_ref, x_ref, g1_ref, nw_ref,
                  wbs_ref, wba_ref, wo_ref, o_ref):
    ta = _sigmoid(ga_ref[...].astype(F32)) * jnp.dot(ya_ref[...], wba_ref[...], preferred_element_type=F32)
    yz = ys_ref[...].astype(F32) * _silu(z_ref[...].astype(F32))
    ysn = yz * lax.rsqrt(jnp.mean(yz * yz, axis=-1, keepdims=True) + EPS) * nw_ref[...]
    ts = jnp.dot(ysn.astype(BF16), wbs_ref[...], preferred_element_type=F32)
    tmix = _sigmoid(gs_ref[...].astype(F32)) * ts + ta
    o_ref[...] = x_ref[...] + g1_ref[0] * jnp.dot(tmix.astype(BF16), wo_ref[...],
                                                   preferred_element_type=F32)


def _merge(ys2, u2, cols, ya2, x2d, mod3, mod_row, norm_w, w_bs, w_ba, w_o, *, tm):
    rows = x2d.shape[0]
    d = D_MODEL
    row_blk = lambda cb: pl.BlockSpec((tm, d), lambda i: (i, cb))
    full = lambda shape: pl.BlockSpec(shape, lambda i: (0, 0))
    return pl.pallas_call(
        _merge_kernel,
        grid=(rows // tm,),
        in_specs=[
            row_blk(0), row_blk(cols["z"] // d), row_blk(0),
            row_blk(cols["gs"] // d), row_blk(cols["ga"] // d), row_blk(0),
            pl.BlockSpec((1, 1, d), lambda i: (mod_row(i), 0, 2)),
            full((1, d)), full((d, d)), full((d, d)), full((d, d)),
        ],
        out_specs=row_blk(0),
        out_shape=jax.ShapeDtypeStruct((rows, d), F32),
        compiler_params=_cparams(("parallel",)),
        name="branch_merge",
    )(ys2, u2, ya2, u2, u2, x2d, mod3, norm_w, w_bs, w_ba, w_o)


def _ffn_kernel(x_ref, xp_ref, xn_ref, sh_ref, sc_ref, g2_ref, nw_ref, wu_ref, cw_ref, cb_ref,
                wd_ref, fw_ref, o_ref, h_s, u_s, act_s, acc_s, *, tm, tf, tiles_per_seg, final_norm):
    i = pl.program_id(0)
    halo = BF16_ROWS
    seg_first = (i % tiles_per_seg) == 0
    seg_last = (i % tiles_per_seg) == tiles_per_seg - 1

    def norm_mod(x):
        r = lax.rsqrt(jnp.mean(x * x, axis=-1, keepdims=True) + EPS)
        return ((x * r * nw_ref[...]) * (1.0 + sc_ref[0]) + sh_ref[0]).astype(BF16)

    h_s[0:halo, :] = norm_mod(xp_ref[...])
    h_s[halo:halo + tm, :] = norm_mod(x_ref[...])
    h_s[halo + tm:2 * halo + tm, :] = norm_mod(xn_ref[...])

    def up_proj(j):
        us = u_s.at[j % 2]
        for half in range(2):
            c0 = half * D_FF + j * tf
            us[:, half * tf:(half + 1) * tf] = jnp.dot(h_s[...], wu_ref[:, c0:c0 + tf],
                                                       preferred_element_type=F32)
        us[halo - 1:halo, :] = jnp.where(seg_first, 0.0, us[halo - 1:halo, :])
        us[halo + tm:halo + tm + 1, :] = jnp.where(seg_last, 0.0, us[halo + tm:halo + tm + 1, :])

    def down_proj(j):
        down = jnp.dot(act_s[j % 2], wd_ref[j * tf:(j + 1) * tf, :], preferred_element_type=F32)
        if j == 0:
            acc_s[...] = down
        else:
            acc_s[...] += down

    n_f = D_FF // tf
    up_proj(0)
    for j in range(n_f):
        if j + 1 < n_f:
            up_proj(j + 1)
        if j > 0:
            down_proj(j - 1)
        us = u_s.at[j % 2]
        w = jnp.concatenate([cw_ref[:, j * tf:(j + 1) * tf],
                             cw_ref[:, D_FF + j * tf:D_FF + (j + 1) * tf]], axis=1)
        b = jnp.concatenate([cb_ref[:, j * tf:(j + 1) * tf],
                             cb_ref[:, D_FF + j * tf:D_FF + (j + 1) * tf]], axis=1)
        uc = (us[halo - 1:halo - 1 + tm, :] * w[0:1] + us[halo:halo + tm, :] * w[1:2]
              + us[halo + 1:halo + 1 + tm, :] * w[2:3] + b)
        act_s[j % 2] = (_silu(uc[:, :tf]) * uc[:, tf:]).astype(BF16)
    down_proj(n_f - 1)
    y = x_ref[...] + g2_ref[0] * acc_s[...]
    if final_norm:
        y = y * lax.rsqrt(jnp.mean(y * y, axis=-1, keepdims=True) + EPS) * fw_ref[...]
    o_ref[...] = y


def _conv_ffn(x2d, mod3, mod_row, norm_w, w_up, conv_w, conv_b, w_down, final_w, *,
              tm, tf, seg_len, final_norm):
    rows = x2d.shape[0]
    d = D_MODEL
    halo = BF16_ROWS
    hb = tm // halo
    last_blk = rows // halo - 1
    assert seg_len % tm == 0
    resident = lambda shape: pl.BlockSpec(shape, lambda i: (0, 0), pipeline_mode=pl.Buffered(1))
    return pl.pallas_call(
        functools.partial(_ffn_kernel, tm=tm, tf=tf, tiles_per_seg=seg_len // tm, final_norm=final_norm),
        grid=(rows // tm,),
        in_specs=[
            pl.BlockSpec((tm, d), lambda i: (i, 0)),
            pl.BlockSpec((halo, d), lambda i: (jnp.maximum(i * hb - 1, 0), 0)),
            pl.BlockSpec((halo, d), lambda i: (jnp.minimum((i + 1) * hb, last_blk), 0)),
            pl.BlockSpec((1, 1, d), lambda i: (mod_row(i), 0, 3)),
            pl.BlockSpec((1, 1, d), lambda i: (mod_row(i), 0, 4)),
            pl.BlockSpec((1, 1, d), lambda i: (mod_row(i), 0, 5)),
            resident((1, d)),
            resident((d, 2 * D_FF)),
            resident((3, 2 * D_FF)),
            resident((1, 2 * D_FF)),
            resident((D_FF, d)),
            resident((1, d)),
        ],
        out_specs=pl.BlockSpec((tm, d), lambda i: (i, 0)),
        out_shape=jax.ShapeDtypeStruct((rows, d), F32),
        scratch_shapes=[
            pltpu.VMEM((tm + 2 * halo, d), BF16),
            pltpu.VMEM((2, tm + 2 * halo, 2 * tf), F32),
            pltpu.VMEM((2, tm, tf), BF16),
            pltpu.VMEM((tm, d), F32),
        ],
        compiler_params=_cparams(("parallel",)),
        name="conv_ffn",
    )(x2d, x2d, x2d, mod3, mod3, mod3, norm_w, w_up, conv_w, conv_b, w_down, final_w)


def _split_in_weight(w):
    sizes = (SSD_WIDTH, SSD_CONV_CH, N_DT, ATT_QK_WIDTH, ATT_QK_WIDTH, ATT_WIDTH, 2 * D_MODEL)
    starts = np.concatenate([[0], np.cumsum(sizes)])
    z, xbc, dt, q, k, v, gates = (w[:, int(starts[n]):int(starts[n + 1])] for n in range(7))
    main = jnp.concatenate([z, q, gates, xbc, k, v], axis=1).astype(BF16)
    return main, _dt_param_order(dt.reshape(-1, 2, SSD_HEADS))


def _dt_param_order(p2h):
    lead = p2h.shape[:-2]
    p = p2h.reshape(*lead, 2, SSD_GROUPS, SSD_HEADS_PER_GROUP)
    return jnp.swapaxes(p, -3, -2).reshape(*lead, N_DT)


def _rope_tables(n_tokens):
    rows = n_tokens // GRID_W
    inv_freq = ROPE_BASE ** (-jnp.arange(ROPE_PAIRS, dtype=F32) / ROPE_PAIRS)
    ang_r = jnp.broadcast_to(jnp.arange(rows, dtype=F32)[:, None, None] * inv_freq, (rows, GRID_W, ROPE_PAIRS))
    ang_c = jnp.broadcast_to(jnp.arange(GRID_W, dtype=F32)[None, :, None] * inv_freq, (rows, GRID_W, ROPE_PAIRS))
    ang = jnp.stack([ang_r, ang_c], axis=2).reshape(n_tokens, 2, 1, ROPE_PAIRS)
    cos = jnp.broadcast_to(jnp.cos(ang), (n_tokens, 2, 2, ROPE_PAIRS))
    sin = jnp.broadcast_to(jnp.sin(ang), (n_tokens, 2, 2, ROPE_PAIRS))
    zero = jnp.zeros_like(sin[:, :, :1])
    sin_a = jnp.concatenate([-sin[:, :, :1], zero], axis=2)
    sin_b = jnp.concatenate([zero, sin[:, :, 1:]], axis=2)
    tile = lambda a: jnp.tile(a.reshape(n_tokens, ATT_HEAD_DIM), (1, 2))
    return tile(cos), tile(sin_a), tile(sin_b)


def kernel(x, c, ctx, c_ctx, w_mod, b_mod, norm1_w, w_in, ssd_conv_w, ssd_conv_b, ssd_a_log,
           ssd_dt_bias, ssd_d, ssd_norm_w, diff_lambda, att_subln_w, w_br_ssd, w_br_att, w_out,
           norm2_w, w_up, ffn_conv_w, ffn_conv_b, w_down, final_norm_w):
    bsz, n_lat, d = x.shape
    n_ctx = ctx.shape[1]
    depth = w_mod.shape[0]
    assert d == D_MODEL and bsz + 1 <= 16
    ctx_row = bsz

    tm_in = min(1024, n_lat)
    tm_in_c = min(1024, bsz * n_ctx)
    tn_in = 2048
    tm_merge = min(512, n_lat)
    tm_merge_c = min(512, bsz * n_ctx)
    tm_ffn = min(512, n_lat)
    tf = 256
    tq = min(1024, n_lat)

    cc = jnp.zeros((16, d), F32).at[:bsz].set(c).at[ctx_row].set(c_ctx)
    mod = _modulation(cc, w_mod, b_mod)

    rope_tabs = _rope_tables(n_lat)
    cols_full = dict(z=COL_Z, q=COL_Q, gs=COL_GS, ga=COL_GA, xs=COL_XS, b=COL_B, c=COL_C, k=COL_K, v=COL_V)

    xl = x.reshape(bsz * n_lat, d)
    xc = ctx.reshape(bsz * n_ctx, d)
    lat_row = lambda tm: (lambda i: (i * tm) // n_lat)
    ctx_rowf = lambda i: ctx_row

    for li in range(depth):
        with_ctx = li < depth - 1
        mod3 = mod[li].reshape(16, 1, N_MOD * d)
        w_main, w_dt32 = _split_in_weight(w_in[li])
        w_dt = jnp.pad(w_dt32, ((0, 0), (0, LANES - N_DT))).astype(BF16)
        w_dtt = w_dt32.T.astype(BF16)
        n1 = norm1_w[li].reshape(1, d)

        ul, dt_l, dtt_l = _in_proj(xl, mod3, lat_row(tm_in), n1, w_main, w_dt, w_dtt,
                                   tm=tm_in, tn=tn_in, col0=0)
        col0_c = 0 if with_ctx else COL_XS
        uc, dt_c, dtt_c = _in_proj(xc, mod3, ctx_rowf, n1, w_main, w_dt, w_dtt,
                                   tm=tm_in_c, tn=tn_in, col0=col0_c)
        cols_c = {k_: v_ - col0_c for k_, v_ in cols_full.items()}
        ul3 = ul.reshape(bsz, n_lat, -1)
        uc3 = uc.reshape(bsz, n_ctx, -1)

        ssd_p = dict(
            conv_w=ssd_conv_w[li], conv_b=ssd_conv_b[li].reshape(1, SSD_CONV_CH),
            bias_row=jnp.zeros((1, LANES), F32).at[0, :N_DT].set(_dt_param_order(ssd_dt_bias[li])),
            bias_col=_dt_param_order(ssd_dt_bias[li]).reshape(N_DT, 1),
            alog_col=_dt_param_order(ssd_a_log[li]).reshape(N_DT, 1),
            alog_x=jnp.repeat(ssd_a_log[li].reshape(2, SSD_GROUPS, SSD_HEADS_PER_GROUP).transpose(1, 0, 2)
                              .reshape(SSD_GROUPS, 2 * SSD_HEADS_PER_GROUP), SSD_HEAD_DIM, axis=1)
            .reshape(SSD_GROUPS, 1, 2 * GROUP_X),
            dskip_x=jnp.repeat(ssd_d[li], SSD_HEAD_DIM).reshape(SSD_GROUPS, 1, GROUP_X),
        )
        ys_l, ys_c = _ssd(ul3, uc3, cols_full, cols_c, dt_l, dt_c, dtt_l, dtt_c, ssd_p)

        lam_init = 0.8 - 0.6 * math.exp(-0.3 * li)
        sw = att_subln_w[li].reshape(1, ATT_V_DIM)
        ya_l = _attention(ul3, COL_Q, uc3, cols_c, ul3, cols_full, rope_tabs, diff_lambda[li], sw,
                          lam_init=lam_init, tq=tq)

        w_bs = w_br_ssd[li].astype(BF16)
        w_ba = w_br_att[li].astype(BF16)
        w_o = w_out[li].astype(BF16)
        sn = ssd_norm_w[li].reshape(1, SSD_WIDTH)
        n2 = norm2_w[li].reshape(1, d)
        w_u = w_up[li].astype(BF16)
        cw = ffn_conv_w[li]
        cb = ffn_conv_b[li].reshape(1, 2 * D_FF)
        w_d = w_down[li].astype(BF16)
        fw = final_norm_w.reshape(1, d)

        if with_ctx:
            ya_c = _attention(uc3, cols_c["q"], uc3, cols_c, None, None, None, diff_lambda[li], sw,
                              lam_init=lam_init, tq=n_ctx)
            xc = _merge(ys_c.reshape(-1, SSD_WIDTH), uc, cols_c, ya_c.reshape(-1, ATT_WIDTH), xc, mod3,
                        ctx_rowf, sn, w_bs, w_ba, w_o, tm=tm_merge_c)
            xc = _conv_ffn(xc, mod3, ctx_rowf, n2, w_u, cw, cb, w_d, fw, tm=n_ctx, tf=tf,
                           seg_len=n_ctx, final_norm=False)

        xl = _merge(ys_l.reshape(-1, SSD_WIDTH), ul, cols_full, ya_l.reshape(-1, ATT_WIDTH), xl, mod3,
                    lat_row(tm_merge), sn, w_bs, w_ba, w_o, tm=tm_merge)
        xl = _conv_ffn(xl, mod3, lat_row(tm_ffn), n2, w_u, cw, cb, w_d, fw, tm=tm_ffn, tf=tf,
                       seg_len=n_lat, final_norm=not with_ctx)

    return xl.reshape(bsz, n_lat, d)
```

```python
import functools
import math

import jax
import jax.numpy as jnp
import numpy as np
from jax import lax
from jax.experimental import pallas as pl
from jax.experimental.pallas import tpu as pltpu

F32 = jnp.float32
BF16 = jnp.bfloat16

D_MODEL = 1024
EPS = 1e-6
N_MOD = 6
GRID_W = 64

SSD_HEADS = 16
SSD_HEAD_DIM = 64
SSD_WIDTH = SSD_HEADS * SSD_HEAD_DIM
SSD_GROUPS = 4
SSD_HEADS_PER_GROUP = SSD_HEADS // SSD_GROUPS
SSD_STATE = 128
SSD_CHUNK = 128
SSD_BC_WIDTH = SSD_GROUPS * SSD_STATE
SSD_CONV_CH = SSD_WIDTH + 2 * SSD_BC_WIDTH
GROUP_X = SSD_HEADS_PER_GROUP * SSD_HEAD_DIM
SSD_GROUP_CHUNKS = 8

ATT_HEADS = 8
ATT_HEAD_DIM = 64
ATT_V_DIM = 2 * ATT_HEAD_DIM
ATT_QK_WIDTH = ATT_HEADS * 2 * ATT_HEAD_DIM
ATT_WIDTH = ATT_HEADS * ATT_V_DIM
ATT_SCALE = ATT_HEAD_DIM ** -0.5
LOG2_E = math.log2(math.e)
ATT_SUB_ROWS = 128
ATT_CHAINS_AHEAD = 2
ROPE_BASE = 10000.0
ROPE_PAIRS = ATT_HEAD_DIM // 4

D_FF = 2816
N_DT = 2 * SSD_HEADS

COL_Z = 0
COL_Q = COL_Z + SSD_WIDTH
COL_GS = COL_Q + ATT_QK_WIDTH
COL_GA = COL_GS + D_MODEL
COL_XS = COL_GA + D_MODEL
COL_B = COL_XS + SSD_WIDTH
COL_C = COL_B + SSD_BC_WIDTH
COL_K = COL_C + SSD_BC_WIDTH
COL_V = COL_K + ATT_QK_WIDTH
N_COLS = COL_V + ATT_WIDTH

LANES = 128
BF16_ROWS = 16
VMEM_LIMIT = 56 * 1024 * 1024


def _sigmoid(x):
    return 0.5 * jnp.tanh(0.5 * x) + 0.5


def _silu(x):
    return x * _sigmoid(x)


def _softplus(x):
    return jnp.maximum(x, 0.0) + jnp.log1p(jnp.exp(-jnp.abs(x)))


def _bdot(a, b):
    return jnp.dot(a.astype(BF16), b.astype(BF16), preferred_element_type=F32)


def _bdot_nt(a, b):
    return lax.dot_general(a.astype(BF16), b.astype(BF16), (((1,), (1,)), ((), ())),
                           preferred_element_type=F32)


def _split3(x):
    hi = x.astype(BF16)
    r1 = x - hi.astype(F32)
    mid = r1.astype(BF16)
    lo = (r1 - mid.astype(F32)).astype(BF16)
    return hi, mid, lo


def _dot2_rhs(a_bf16, x):
    hi = x.astype(BF16)
    lo = (x - hi.astype(F32)).astype(BF16)
    return (jnp.dot(a_bf16, hi, preferred_element_type=F32)
            + jnp.dot(a_bf16, lo, preferred_element_type=F32))


def _dot2_lhs(x, b_bf16):
    hi = x.astype(BF16)
    lo = (x - hi.astype(F32)).astype(BF16)
    return (jnp.dot(hi, b_bf16, preferred_element_type=F32)
            + jnp.dot(lo, b_bf16, preferred_element_type=F32))


def _cparams(sem):
    return pltpu.CompilerParams(dimension_semantics=sem, vmem_limit_bytes=VMEM_LIMIT)


def _mod_kernel(c_ref, w_ref, b_ref, o_ref):
    a = _silu(c_ref[...])
    o_ref[...] = _dot3_both(a, w_ref[...]) + b_ref[...]


def _dot3_both(a, w):
    ah, am, al = _split3(a)
    wh, wm, wl = _split3(w)
    d = lambda p, q: jnp.dot(p, q, preferred_element_type=F32)
    return (d(ah, wh) + (d(ah, wm) + d(am, wh))
            + (d(ah, wl) + d(am, wm) + d(al, wh)))


def _modulation(cc, w_mod, b_mod):
    depth = w_mod.shape[0]
    tn = 1024
    return pl.pallas_call(
        _mod_kernel,
        grid=(depth, N_MOD * D_MODEL // tn),
        in_specs=[
            pl.BlockSpec((16, D_MODEL), lambda l, j: (0, 0)),
            pl.BlockSpec((None, D_MODEL, tn), lambda l, j: (l, 0, j)),
            pl.BlockSpec((None, 1, tn), lambda l, j: (l, 0, j)),
        ],
        out_specs=pl.BlockSpec((None, 16, tn), lambda l, j: (l, 0, j)),
        out_shape=jax.ShapeDtypeStruct((depth, 16, N_MOD * D_MODEL), F32),
        compiler_params=_cparams(("parallel", "parallel")),
        name="modulation",
    )(cc, w_mod, b_mod.reshape(depth, 1, N_MOD * D_MODEL))


def _rope(x, cos, sin_a, sin_b):
    return x * cos + pltpu.roll(x, LANES - ROPE_PAIRS, axis=1) * sin_a + pltpu.roll(x, ROPE_PAIRS, axis=1) * sin_b


def _in_proj_kernel(x_ref, sh_ref, sc_ref, nw_ref, w_ref, wdt_ref, wdtt_ref,
                    u_ref, dt_ref, dtt_ref, h_ref):
    @pl.when(pl.program_id(1) == 0)
    def _():
        x = x_ref[...]
        r = lax.rsqrt(jnp.mean(x * x, axis=-1, keepdims=True) + EPS)
        h = (x * r * nw_ref[...]) * (1.0 + sc_ref[0]) + sh_ref[0]
        hb = h.astype(BF16)
        h_ref[...] = hb
        dt_ref[...] = jnp.dot(hb, wdt_ref[...], preferred_element_type=F32)
        dtt_ref[...] = lax.dot_general(wdtt_ref[...], hb, (((1,), (1,)), ((), ())),
                                       preferred_element_type=F32)

    u_ref[...] = jnp.dot(h_ref[...], w_ref[...], preferred_element_type=F32).astype(BF16)


def _in_proj(x2d, mod3, mod_row, norm_w, w_main, w_dt, w_dtt, *, tm, tn, col0):
    rows = x2d.shape[0]
    n_out = w_main.shape[1] - col0
    cb0 = col0 // tn
    return pl.pallas_call(
        _in_proj_kernel,
        grid=(rows // tm, n_out // tn),
        in_specs=[
            pl.BlockSpec((tm, D_MODEL), lambda i, j: (i, 0)),
            pl.BlockSpec((1, 1, D_MODEL), lambda i, j: (mod_row(i), 0, 0)),
            pl.BlockSpec((1, 1, D_MODEL), lambda i, j: (mod_row(i), 0, 1)),
            pl.BlockSpec((1, D_MODEL), lambda i, j: (0, 0)),
            pl.BlockSpec((D_MODEL, tn), lambda i, j: (0, j + cb0)),
            pl.BlockSpec((D_MODEL, LANES), lambda i, j: (0, 0)),
            pl.BlockSpec((N_DT, D_MODEL), lambda i, j: (0, 0)),
        ],
        out_specs=[
            pl.BlockSpec((tm, tn), lambda i, j: (i, j)),
            pl.BlockSpec((tm, LANES), lambda i, j: (i, 0)),
            pl.BlockSpec((N_DT, tm), lambda i, j: (0, i)),
        ],
        out_shape=[
            jax.ShapeDtypeStruct((rows, n_out), BF16),
            jax.ShapeDtypeStruct((rows, LANES), F32),
            jax.ShapeDtypeStruct((N_DT, rows), F32),
        ],
        scratch_shapes=[pltpu.VMEM((tm, D_MODEL), BF16)],
        compiler_params=_cparams(("parallel", "arbitrary")),
        name="in_proj",
    )(x2d, mod3, mod3, norm_w, w_main, w_dt, w_dtt)


def _ssd_kernel(xl_ref, bl_ref, cl_ref, xc_ref, bc_ref, cc_ref,
                dtl_ref, dtc_ref, dttl_ref, dttc_ref,
                cwx_ref, cwb_ref, cwc_ref, cbx_ref, cbb_ref, cbc_ref,
                biasr_ref, biasc_ref, alogx_ref, alogc_ref, dskip_ref,
                yl_ref, yc_ref,
                cm_s, y_s, ex_s, st_s, dec_s, hin_s, *, n_lat, n_ctx):
    t = SSD_CHUNK
    g = pl.program_id(1)
    nc_ctx = n_ctx // t
    nc_lat = n_lat // t
    n_chunks = nc_ctx + nc_lat
    gx = GROUP_X

    ri = lax.broadcasted_iota(jnp.int32, (t, t), 0)
    ci = lax.broadcasted_iota(jnp.int32, (t, t), 1)
    lower = ri >= ci
    tril = lower.astype(BF16)
    triu = (ri <= ci).astype(BF16)
    er = lax.broadcasted_iota(jnp.int32, (LANES, 2 * gx), 0)
    ec = lax.broadcasted_iota(jnp.int32, (LANES, 2 * gx), 1)
    expand = (er == g * (2 * SSD_HEADS_PER_GROUP) + ec // SSD_HEAD_DIM).astype(BF16)
    head_of_lane = lax.broadcasted_iota(jnp.int32, (t, gx), 1) // SSD_HEAD_DIM
    tt = jnp.concatenate([triu, tril], axis=1)
    a_x = -jnp.exp(alogx_ref[0]) * LOG2_E
    a_c = -jnp.exp(alogc_ref[...]) * LOG2_E

    si = lax.broadcasted_iota(jnp.int32, (2 * t, t + 2 * BF16_ROWS), 0)
    sj = lax.broadcasted_iota(jnp.int32, (2 * t, t + 2 * BF16_ROWS), 1)
    src = jnp.where(si < t, jnp.where(si == 0, t + BF16_ROWS - 1, si - 1),
                    jnp.where(si == 2 * t - 1, t + BF16_ROWS, si - t + 1))
    shift_m = (sj == src).astype(BF16)
    conv_w = jnp.concatenate([cwx_ref[...], cwb_ref[...], cwc_ref[...]], axis=1)
    conv_b = jnp.concatenate([cbx_ref[...], cbb_ref[...], cbc_ref[...]], axis=1)

    def phase_a(cs, x_ref, b_ref, c_ref, dt_ref, dtt_ref, n_seg_chunks, chunk0):
        seg_rows = n_seg_chunks * t
        n8 = 2 * SSD_HEADS_PER_GROUP
        st = [dict(c=c, r0=pl.multiple_of(c * t, t), o0=pl.multiple_of((c + chunk0) * t, t)) for c in cs]

        for s in st:
            c, r0 = s["c"], s["r0"]
            lo = pl.multiple_of(jnp.maximum(r0 - BF16_ROWS, 0), BF16_ROWS)
            hi = pl.multiple_of(jnp.minimum(r0 + t, seg_rows - BF16_ROWS), BF16_ROWS)
            rows = lambda a, n: jnp.concatenate(
                [x_ref[0, pl.ds(a, n), :], b_ref[0, pl.ds(a, n), :], c_ref[0, pl.ds(a, n), :]], axis=1)
            cur = rows(r0, t)
            zero = jnp.zeros((BF16_ROWS, cur.shape[1]), BF16)
            before = jnp.where(c > 0, rows(lo, BF16_ROWS), zero)
            after = jnp.where(c < n_seg_chunks - 1, rows(hi, BF16_ROWS), zero)
            s["cur"] = cur
            s["sh"] = jnp.dot(shift_m, jnp.concatenate([cur, before, after], axis=0),
                              preferred_element_type=F32)
            dt = _softplus(dt_ref[0, pl.ds(r0, t), :] + biasr_ref[...])
            s["dtx"] = _dot2_lhs(dt, expand)
            at = _softplus(dtt_ref[:, pl.ds(r0, t)] + biasc_ref[...]) * a_c
            hi3, mid3, lo3 = _split3(at)
            s["ct3"] = jnp.dot(jnp.concatenate([hi3, mid3, lo3], axis=0), tt, preferred_element_type=F32)

        for s in st:
            ax = s["dtx"] * a_x
            s["csf"] = _dot2_rhs(tril, ax[:, :gx])
            s["csb"] = _dot2_rhs(triu, ax[:, gx:])
            sh = s["sh"]
            xbc = _silu(sh[:t] * conv_w[0:1] + s["cur"].astype(F32) * conv_w[1:2]
                        + sh[t:] * conv_w[2:3] + conv_b)
            s["xs"] = xbc[:, :gx]
            s["bm"] = xbc[:, gx:gx + SSD_STATE].astype(BF16)
            s["cm"] = xbc[:, gx + SSD_STATE:].astype(BF16)
            cm_s[pl.ds(s["o0"], t), :] = s["cm"]
            s["cb"] = _bdot_nt(s["cm"], s["bm"])

        for s in st:
            ct3 = s["ct3"]
            cst = ct3[0:n8] + ct3[n8:2 * n8] + ct3[2 * n8:3 * n8]
            xs, dtx, cb = s["xs"], s["dtx"], s["cb"]
            y = dskip_ref[0] * xs
            for d, cs_d in ((0, s["csf"]), (1, s["csb"])):
                mask = lower if d == 0 else (ri <= ci)
                xd = (xs * dtx[:, d * gx:(d + 1) * gx]).astype(BF16)
                ms, blocks = [], []
                for r in range(SSD_HEADS_PER_GROUP):
                    col = cs_d[:, r * SSD_HEAD_DIM:r * SSD_HEAD_DIM + 1]
                    k = d * SSD_HEADS_PER_GROUP + r
                    row = cst[k:k + 1, d * t:(d + 1) * t]
                    seg = jnp.exp2(jnp.where(mask, col - row, -jnp.inf))
                    ms.append((cb * seg).astype(BF16))
                    blocks.append(jnp.where(head_of_lane == r, xd, jnp.zeros_like(xd)))
                y = y + jnp.dot(jnp.concatenate(ms, axis=1), jnp.concatenate(blocks, axis=0),
                                preferred_element_type=F32)
            y_s[pl.ds(s["o0"], t), :] = y

        for s in st:
            xs, dtx, csf, csb = s["xs"], s["dtx"], s["csf"], s["csb"]
            endf = csf[t - 1:t, :]
            endb = csb[0:1, :]
            ex_s[pl.ds(s["o0"], t), :] = jnp.exp2(jnp.concatenate([csf, csb], axis=1))
            xdd = jnp.concatenate([xs * dtx[:, :gx] * jnp.exp2(endf - csf),
                                   xs * dtx[:, gx:] * jnp.exp2(endb - csb)], axis=1).astype(BF16)
            bt = jnp.transpose(s["bm"].astype(F32)).astype(BF16)
            st_s[s["c"] + chunk0] = jnp.dot(bt, xdd, preferred_element_type=F32)
            dec_s[pl.ds(s["c"] + chunk0, 1), :] = jnp.exp2(jnp.concatenate([endf, endb], axis=1))

    def run_phase_a(n_seg_chunks, chunk0, refs):
        group = math.gcd(n_seg_chunks, SSD_GROUP_CHUNKS)

        def body(i, carry):
            phase_a([i * group + k for k in range(group)], *refs, n_seg_chunks, chunk0)
            return carry

        lax.fori_loop(0, n_seg_chunks // group, body, 0)

    run_phase_a(nc_ctx, 0, (xc_ref, bc_ref, cc_ref, dtc_ref, dttc_ref))
    run_phase_a(nc_lat, nc_ctx, (xl_ref, bl_ref, cl_ref, dtl_ref, dttl_ref))

    order_f = list(range(n_chunks))
    order_b = list(range(nc_ctx - 1, -1, -1)) + list(range(n_chunks - 1, nc_ctx - 1, -1))
    for d, order in ((0, order_f), (1, order_b)):
        h = jnp.zeros((SSD_STATE, gx), F32)
        for c in order:
            hin_s[c, :, d * gx:(d + 1) * gx] = h.astype(BF16)
            h = dec_s[c:c + 1, d * gx:(d + 1) * gx] * h + st_s[c, :, d * gx:(d + 1) * gx]

    def phase_c(cs, out_ref, chunk0):
        offs = [pl.multiple_of((c + chunk0) * t, t) for c in cs]
        yos = [jnp.dot(cm_s[pl.ds(o0, t), :], hin_s[c + chunk0], preferred_element_type=F32)
               for c, o0 in zip(cs, offs)]
        for c, o0, yo in zip(cs, offs, yos):
            yo = yo * ex_s[pl.ds(o0, t), :]
            y = y_s[pl.ds(o0, t), :] + yo[:, :gx] + yo[:, gx:]
            out_ref[0, pl.ds(pl.multiple_of(c * t, t), t), :] = y.astype(BF16)

    def run_phase_c(n_seg_chunks, chunk0, out_ref):
        group = math.gcd(n_seg_chunks, SSD_GROUP_CHUNKS)

        def body(i, carry):
            phase_c([i * group + k for k in range(group)], out_ref, chunk0)
            return carry

        lax.fori_loop(0, n_seg_chunks // group, body, 0)

    run_phase_c(nc_ctx, 0, yc_ref)
    run_phase_c(nc_lat, nc_ctx, yl_ref)


def _ssd(ul3, uc3, cols_l, cols_c, dt_l, dt_c, dtt_l, dtt_c, p):
    bsz, n_lat, _ = ul3.shape
    n_ctx = uc3.shape[1]
    gx = GROUP_X
    n_chunks = (n_lat + n_ctx) // SSD_CHUNK
    n_tot = n_lat + n_ctx

    def seq_specs(cols, n):
        xb, bb, cb = cols["xs"] // gx, cols["b"] // SSD_STATE, cols["c"] // SSD_STATE
        return [
            pl.BlockSpec((1, n, gx), lambda b, g: (b, 0, xb + g)),
            pl.BlockSpec((1, n, SSD_STATE), lambda b, g: (b, 0, bb + g)),
            pl.BlockSpec((1, n, SSD_STATE), lambda b, g: (b, 0, cb + g)),
        ]

    xoff = 0
    boff = SSD_WIDTH // SSD_STATE
    coff = (SSD_WIDTH + SSD_BC_WIDTH) // SSD_STATE
    n8 = 2 * SSD_HEADS_PER_GROUP
    in_specs = (
        seq_specs(cols_l, n_lat) + seq_specs(cols_c, n_ctx) + [
            pl.BlockSpec((1, n_lat, LANES), lambda b, g: (b, 0, 0)),
            pl.BlockSpec((1, n_ctx, LANES), lambda b, g: (b, 0, 0)),
            pl.BlockSpec((n8, n_lat), lambda b, g: (g, b)),
            pl.BlockSpec((n8, n_ctx), lambda b, g: (g, b)),
            pl.BlockSpec((3, gx), lambda b, g: (0, xoff + g)),
            pl.BlockSpec((3, SSD_STATE), lambda b, g: (0, boff + g)),
            pl.BlockSpec((3, SSD_STATE), lambda b, g: (0, coff + g)),
            pl.BlockSpec((1, gx), lambda b, g: (0, xoff + g)),
            pl.BlockSpec((1, SSD_STATE), lambda b, g: (0, boff + g)),
            pl.BlockSpec((1, SSD_STATE), lambda b, g: (0, coff + g)),
            pl.BlockSpec((1, LANES), lambda b, g: (0, 0)),
            pl.BlockSpec((n8, 1), lambda b, g: (g, 0)),
            pl.BlockSpec((1, 1, 2 * gx), lambda b, g: (g, 0, 0)),
            pl.BlockSpec((n8, 1), lambda b, g: (g, 0)),
            pl.BlockSpec((1, 1, gx), lambda b, g: (g, 0, 0)),
        ])
    return pl.pallas_call(
        functools.partial(_ssd_kernel, n_lat=n_lat, n_ctx=n_ctx),
        grid=(bsz, SSD_GROUPS),
        in_specs=in_specs,
        out_specs=[
            pl.BlockSpec((1, n_lat, gx), lambda b, g: (b, 0, g)),
            pl.BlockSpec((1, n_ctx, gx), lambda b, g: (b, 0, g)),
        ],
        out_shape=[
            jax.ShapeDtypeStruct((bsz, n_lat, SSD_WIDTH), BF16),
            jax.ShapeDtypeStruct((bsz, n_ctx, SSD_WIDTH), BF16),
        ],
        scratch_shapes=[
            pltpu.VMEM((n_tot, SSD_STATE), BF16),
            pltpu.VMEM((n_tot, gx), F32),
            pltpu.VMEM((n_tot, 2 * gx), F32),
            pltpu.VMEM((n_chunks, SSD_STATE, 2 * gx), F32),
            pltpu.VMEM((n_chunks + (-n_chunks) % 8, 2 * gx), F32),
            pltpu.VMEM((n_chunks, SSD_STATE, 2 * gx), BF16),
        ],
        compiler_params=_cparams(("parallel", "parallel")),
        name="ssd_scan",
    )(ul3, ul3, ul3, uc3, uc3, uc3,
      dt_l.reshape(bsz, n_lat, LANES), dt_c.reshape(bsz, n_ctx, LANES), dtt_l, dtt_c,
      p["conv_w"], p["conv_w"], p["conv_w"], p["conv_b"], p["conv_b"], p["conv_b"],
      p["bias_row"], p["bias_col"], p["alog_x"], p["alog_col"], p["dskip_x"])


def _attn_kernel(*refs, n_ctx, n_lat, tq, lam_init):
    if n_lat:
        (q_ref, kc_ref, vc_ref, kl_ref, vl_ref, cos_ref, sa_ref, sb_ref, lam_ref, sw_ref,
         o_ref, kl_s) = refs
    else:
        q_ref, kc_ref, vc_ref, lam_ref, sw_ref, o_ref = refs
    qi = pl.program_id(2)
    hd = ATT_HEAD_DIM

    if n_lat:
        @pl.when(qi == 0)
        def _():
            kl_s[...] = _rope(kl_ref[0].astype(F32), cos_ref[...], sa_ref[...], sb_ref[...]).astype(BF16)

    lf = lam_ref[...]
    lam = (jnp.exp(jnp.sum(lf[0:1] * lf[1:2], axis=-1, keepdims=True))
           - jnp.exp(jnp.sum(lf[2:3] * lf[3:4], axis=-1, keepdims=True)) + lam_init)

    q = q_ref[0].astype(F32)
    if n_lat:
        r0 = pl.multiple_of(qi * tq, tq)
        q = _rope(q, cos_ref[pl.ds(r0, tq), :], sa_ref[pl.ds(r0, tq), :], sb_ref[pl.ds(r0, tq), :])
    q = (q * (ATT_SCALE * LOG2_E)).astype(BF16)
    lane = lax.broadcasted_iota(jnp.int32, (1, ATT_V_DIM), 1)
    q_comp = [jnp.where(lane < hd, q, jnp.zeros_like(q)), jnp.where(lane >= hd, q, jnp.zeros_like(q))]

    sub = min(ATT_SUB_ROWS, tq)
    chains = [(r, c) for r in range(tq // sub) for c in range(2)]
    nt = (((1,), (1,)), ((), ()))

    def scores(r, c):
        qc = q_comp[c][r * sub:(r + 1) * sub, :]
        s = lax.dot_general(qc, kc_ref[0], nt, preferred_element_type=F32)
        if n_lat:
            s = jnp.concatenate([s, lax.dot_general(qc, kl_s[...], nt, preferred_element_type=F32)],
                                axis=1)
        return s

    def weighted_values(e):
        pv = jnp.dot(e[:, :n_ctx], vc_ref[0], preferred_element_type=F32)
        if n_lat:
            pv = pv + jnp.dot(e[:, n_ctx:], vl_ref[0], preferred_element_type=F32)
        return pv

    ahead = ATT_CHAINS_AHEAD
    pending = [scores(*ch) for ch in chains[:ahead]]
    outs = {}
    for n, (r, c) in enumerate(chains):
        s = pending.pop(0)
        if n + ahead < len(chains):
            pending.append(scores(*chains[n + ahead]))
        e = jnp.exp2(s - jnp.max(s, axis=-1, keepdims=True))
        inv = 1.0 / jnp.sum(e, axis=-1, keepdims=True)
        outs[c] = weighted_values(e.astype(BF16)) * inv
        if c == 1:
            o = outs[0] - lam * outs[1]
            on = o * lax.rsqrt(jnp.mean(o * o, axis=-1, keepdims=True) + EPS) * sw_ref[...]
            o_ref[0, r * sub:(r + 1) * sub, :] = (on * (1.0 - lam_init)).astype(BF16)


def _attention(uq3, col_q, uc3, cols_c, ul3, cols_l, rope_tabs, lam_p, subln_w, *, lam_init, tq):
    bsz, n_q, _ = uq3.shape
    n_ctx = uc3.shape[1]
    n_lat = 0 if ul3 is None else ul3.shape[1]
    vd = ATT_V_DIM
    qb = col_q // vd
    in_specs = [
        pl.BlockSpec((1, tq, vd), lambda b, h, i: (b, i, qb + h)),
        pl.BlockSpec((1, n_ctx, vd), lambda b, h, i: (b, 0, cols_c["k"] // vd + h)),
        pl.BlockSpec((1, n_ctx, vd), lambda b, h, i: (b, 0, cols_c["v"] // vd + h)),
    ]
    args = [uq3, uc3, uc3]
    if n_lat:
        in_specs += [
            pl.BlockSpec((1, n_lat, vd), lambda b, h, i: (b, 0, cols_l["k"] // vd + h)),
            pl.BlockSpec((1, n_lat, vd), lambda b, h, i: (b, 0, cols_l["v"] // vd + h)),
            pl.BlockSpec((n_lat, vd), lambda b, h, i: (0, 0)),
            pl.BlockSpec((n_lat, vd), lambda b, h, i: (0, 0)),
            pl.BlockSpec((n_lat, vd), lambda b, h, i: (0, 0)),
        ]
        args += [ul3, ul3, *rope_tabs]
    in_specs += [
        pl.BlockSpec((4, ATT_HEAD_DIM), lambda b, h, i: (0, 0)),
        pl.BlockSpec((1, vd), lambda b, h, i: (0, 0)),
    ]
    args += [lam_p, subln_w]
    return pl.pallas_call(
        functools.partial(_attn_kernel, n_ctx=n_ctx, n_lat=n_lat, tq=tq, lam_init=lam_init),
        grid=(bsz, ATT_HEADS, n_q // tq),
        in_specs=in_specs,
        out_specs=pl.BlockSpec((1, tq, vd), lambda b, h, i: (b, i, h)),
        out_shape=jax.ShapeDtypeStruct((bsz, n_q, ATT_WIDTH), BF16),
        scratch_shapes=[pltpu.VMEM((n_lat, vd), BF16)] if n_lat else [],
        compiler_params=_cparams(("parallel", "parallel", "arbitrary")),
        name="diff_attn",
    )(*args)


def _merge_kernel(ys_ref, z_ref, ya_ref, gs_ref, ga_ref, x_ref, g1_ref, nw_ref,
                  wbs_ref, wba_ref, wo_ref, o_ref):
    ta = _sigmoid(ga_ref[...].astype(F32)) * jnp.dot(ya_ref[...], wba_ref[...], preferred_element_type=F32)
    yz = ys_ref[...].astype(F32) * _silu(z_ref[...].astype(F32))
    ysn = yz * lax.rsqrt(jnp.mean(yz * yz, axis=-1, keepdims=True) + EPS) * nw_ref[...]
    ts = jnp.dot(ysn.astype(BF16), wbs_ref[...], preferred_element_type=F32)
    tmix = _sigmoid(gs_ref[...].astype(F32)) * ts + ta
    o_ref[...] = x_ref[...] + g1_ref[0] * jnp.dot(tmix.astype(BF16), wo_ref[...],
                                                   preferred_element_type=F32)


def _merge(ys2, u2, cols, ya2, x2d, mod3, mod_row, norm_w, w_bs, w_ba, w_o, *, tm):
    rows = x2d.shape[0]
    d = D_MODEL
    row_blk = lambda cb: pl.BlockSpec((tm, d), lambda i: (i, cb))
    full = lambda shape: pl.BlockSpec(shape, lambda i: (0, 0))
    return pl.pallas_call(
        _merge_kernel,
        grid=(rows // tm,),
        in_specs=[
            row_blk(0), row_blk(cols["z"] // d), row_blk(0),
            row_blk(cols["gs"] // d), row_blk(cols["ga"] // d), row_blk(0),
            pl.BlockSpec((1, 1, d), lambda i: (mod_row(i), 0, 2)),
            full((1, d)), full((d, d)), full((d, d)), full((d, d)),
        ],
        out_specs=row_blk(0),
        out_shape=jax.ShapeDtypeStruct((rows, d), F32),
        compiler_params=_cparams(("parallel",)),
        name="branch_merge",
    )(ys2, u2, ya2, u2, u2, x2d, mod3, norm_w, w_bs, w_ba, w_o)


def _ffn_kernel(x_ref, xp_ref, xn_ref, sh_ref, sc_ref, g2_ref, nw_ref, wu_ref, cw_ref, cb_ref,
                wd_ref, fw_ref, o_ref, h_s, u_s, act_s, acc_s, *, tm, tf, tiles_per_seg, final_norm):
    i = pl.program_id(0)
    halo = BF16_ROWS
    seg_first = (i % tiles_per_seg) == 0
    seg_last = (i % tiles_per_seg) == tiles_per_seg - 1

    def norm_mod(x):
        r = lax.rsqrt(jnp.mean(x * x, axis=-1, keepdims=True) + EPS)
        return ((x * r * nw_ref[...]) * (1.0 + sc_ref[0]) + sh_ref[0]).astype(BF16)

    h_s[0:halo, :] = norm_mod(xp_ref[...])
    h_s[halo:halo + tm, :] = norm_mod(x_ref[...])
    h_s[halo + tm:2 * halo + tm, :] = norm_mod(xn_ref[...])

    def up_proj(j):
        us = u_s.at[j % 2]
        for half in range(2):
            c0 = half * D_FF + j * tf
            us[:, half * tf:(half + 1) * tf] = jnp.dot(h_s[...], wu_ref[:, c0:c0 + tf],
                                                       preferred_element_type=F32)
        us[halo - 1:halo, :] = jnp.where(seg_first, 0.0, us[halo - 1:halo, :])
        us[halo + tm:halo + tm + 1, :] = jnp.where(seg_last, 0.0, us[halo + tm:halo + tm + 1, :])

    def down_proj(j):
        down = jnp.dot(act_s[j % 2], wd_ref[j * tf:(j + 1) * tf, :], preferred_element_type=F32)
        if j == 0:
            acc_s[...] = down
        else:
            acc_s[...] += down

    n_f = D_FF // tf
    up_proj(0)
    for j in range(n_f):
        if j + 1 < n_f:
            up_proj(j + 1)
        if j > 0:
            down_proj(j - 1)
        us = u_s.at[j % 2]
        w = jnp.concatenate([cw_ref[:, j * tf:(j + 1) * tf],
                             cw_ref[:, D_FF + j * tf:D_FF + (j + 1) * tf]], axis=1)
        b = jnp.concatenate([cb_ref[:, j * tf:(j + 1) * tf],
                             cb_ref[:, D_FF + j * tf:D_FF + (j + 1) * tf]], axis=1)
        uc = (us[halo - 1:halo - 1 + tm, :] * w[0:1] + us[halo:halo + tm, :] * w[1:2]
              + us[halo + 1:halo + 1 + tm, :] * w[2:3] + b)
        act_s[j % 2] = (_silu(uc[:, :tf]) * uc[:, tf:]).astype(BF16)
    down_proj(n_f - 1)
    y = x_ref[...] + g2_ref[0] * acc_s[...]
    if final_norm:
        y = y * lax.rsqrt(jnp.mean(y * y, axis=-1, keepdims=True) + EPS) * fw_ref[...]
    o_ref[...] = y


def _conv_ffn(x2d, mod3, mod_row, norm_w, w_up, conv_w, conv_b, w_down, final_w, *,
              tm, tf, seg_len, final_norm):
    rows = x2d.shape[0]
    d = D_MODEL
    halo = BF16_ROWS
    hb = tm // halo
    last_blk = rows // halo - 1
    assert seg_len % tm == 0
    resident = lambda shape: pl.BlockSpec(shape, lambda i: (0, 0), pipeline_mode=pl.Buffered(1))
    return pl.pallas_call(
        functools.partial(_ffn_kernel, tm=tm, tf=tf, tiles_per_seg=seg_len // tm, final_norm=final_norm),
        grid=(rows // tm,),
        in_specs=[
            pl.BlockSpec((tm, d), lambda i: (i, 0)),
            pl.BlockSpec((halo, d), lambda i: (jnp.maximum(i * hb - 1, 0), 0)),
            pl.BlockSpec((halo, d), lambda i: (jnp.minimum((i + 1) * hb, last_blk), 0)),
            pl.BlockSpec((1, 1, d), lambda i: (mod_row(i), 0, 3)),
            pl.BlockSpec((1, 1, d), lambda i: (mod_row(i), 0, 4)),
            pl.BlockSpec((1, 1, d), lambda i: (mod_row(i), 0, 5)),
            resident((1, d)),
            resident((d, 2 * D_FF)),
            resident((3, 2 * D_FF)),
            resident((1, 2 * D_FF)),
            resident((D_FF, d)),
            resident((1, d)),
        ],
        out_specs=pl.BlockSpec((tm, d), lambda i: (i, 0)),
        out_shape=jax.ShapeDtypeStruct((rows, d), F32),
        scratch_shapes=[
            pltpu.VMEM((tm + 2 * halo, d), BF16),
            pltpu.VMEM((2, tm + 2 * halo, 2 * tf), F32),
            pltpu.VMEM((2, tm, tf), BF16),
            pltpu.VMEM((tm, d), F32),
        ],
        compiler_params=_cparams(("parallel",)),
        name="conv_ffn",
    )(x2d, x2d, x2d, mod3, mod3, mod3, norm_w, w_up, conv_w, conv_b, w_down, final_w)


def _split_in_weight(w):
    sizes = (SSD_WIDTH, SSD_CONV_CH, N_DT, ATT_QK_WIDTH, ATT_QK_WIDTH, ATT_WIDTH, 2 * D_MODEL)
    starts = np.concatenate([[0], np.cumsum(sizes)])
    z, xbc, dt, q, k, v, gates = (w[:, int(starts[n]):int(starts[n + 1])] for n in range(7))
    main = jnp.concatenate([z, q, gates, xbc, k, v], axis=1).astype(BF16)
    return main, _dt_param_order(dt.reshape(-1, 2, SSD_HEADS))


def _dt_param_order(p2h):
    lead = p2h.shape[:-2]
    p = p2h.reshape(*lead, 2, SSD_GROUPS, SSD_HEADS_PER_GROUP)
    return jnp.swapaxes(p, -3, -2).reshape(*lead, N_DT)


def _rope_tables(n_tokens):
    rows = n_tokens // GRID_W
    inv_freq = ROPE_BASE ** (-jnp.arange(ROPE_PAIRS, dtype=F32) / ROPE_PAIRS)
    ang_r = jnp.broadcast_to(jnp.arange(rows, dtype=F32)[:, None, None] * inv_freq, (rows, GRID_W, ROPE_PAIRS))
    ang_c = jnp.broadcast_to(jnp.arange(GRID_W, dtype=F32)[None, :, None] * inv_freq, (rows, GRID_W, ROPE_PAIRS))
    ang = jnp.stack([ang_r, ang_c], axis=2).reshape(n_tokens, 2, 1, ROPE_PAIRS)
    cos = jnp.broadcast_to(jnp.cos(ang), (n_tokens, 2, 2, ROPE_PAIRS))
    sin = jnp.broadcast_to(jnp.sin(ang), (n_tokens, 2, 2, ROPE_PAIRS))
    zero = jnp.zeros_like(sin[:, :, :1])
    sin_a = jnp.concatenate([-sin[:, :, :1], zero], axis=2)
    sin_b = jnp.concatenate([zero, sin[:, :, 1:]], axis=2)
    tile = lambda a: jnp.tile(a.reshape(n_tokens, ATT_HEAD_DIM), (1, 2))
    return tile(cos), tile(sin_a), tile(sin_b)


def kernel(x, c, ctx, c_ctx, w_mod, b_mod, norm1_w, w_in, ssd_conv_w, ssd_conv_b, ssd_a_log,
           ssd_dt_bias, ssd_d, ssd_norm_w, diff_lambda, att_subln_w, w_br_ssd, w_br_att, w_out,
           norm2_w, w_up, ffn_conv_w, ffn_conv_b, w_down, final_norm_w):
    bsz, n_lat, d = x.shape
    n_ctx = ctx.shape[1]
    depth = w_mod.shape[0]
    assert d == D_MODEL and bsz + 1 <= 16
    ctx_row = bsz

    tm_in = min(1024, n_lat)
    tm_in_c = min(1024, bsz * n_ctx)
    tn_in = 4096
    tm_merge = min(512, n_lat)
    tm_merge_c = min(512, bsz * n_ctx)
    tm_ffn = min(512, n_lat)
    tf = 256
    tq = min(2048, n_lat)

    cc = jnp.zeros((16, d), F32).at[:bsz].set(c).at[ctx_row].set(c_ctx)
    mod = _modulation(cc, w_mod, b_mod)

    rope_tabs = _rope_tables(n_lat)
    cols_full = dict(z=COL_Z, q=COL_Q, gs=COL_GS, ga=COL_GA, xs=COL_XS, b=COL_B, c=COL_C, k=COL_K, v=COL_V)

    xl = x.reshape(bsz * n_lat, d)
    xc = ctx.reshape(bsz * n_ctx, d)
    lat_row = lambda tm: (lambda i: (i * tm) // n_lat)
    ctx_rowf = lambda i: ctx_row

    for li in range(depth):
        with_ctx = li < depth - 1
        mod3 = mod[li].reshape(16, 1, N_MOD * d)
        w_main, w_dt32 = _split_in_weight(w_in[li])
        w_dt = jnp.pad(w_dt32, ((0, 0), (0, LANES - N_DT))).astype(BF16)
        w_dtt = w_dt32.T.astype(BF16)
        n1 = norm1_w[li].reshape(1, d)

        ul, dt_l, dtt_l = _in_proj(xl, mod3, lat_row(tm_in), n1, w_main, w_dt, w_dtt,
                                   tm=tm_in, tn=tn_in, col0=0)
        col0_c = 0 if with_ctx else COL_XS
        uc, dt_c, dtt_c = _in_proj(xc, mod3, ctx_rowf, n1, w_main, w_dt, w_dtt,
                                   tm=tm_in_c, tn=tn_in, col0=col0_c)
        cols_c = {k_: v_ - col0_c for k_, v_ in cols_full.items()}
        ul3 = ul.reshape(bsz, n_lat, -1)
        uc3 = uc.reshape(bsz, n_ctx, -1)

        ssd_p = dict(
            conv_w=ssd_conv_w[li], conv_b=ssd_conv_b[li].reshape(1, SSD_CONV_CH),
            bias_row=jnp.zeros((1, LANES), F32).at[0, :N_DT].set(_dt_param_order(ssd_dt_bias[li])),
            bias_col=_dt_param_order(ssd_dt_bias[li]).reshape(N_DT, 1),
            alog_col=_dt_param_order(ssd_a_log[li]).reshape(N_DT, 1),
            alog_x=jnp.repeat(ssd_a_log[li].reshape(2, SSD_GROUPS, SSD_HEADS_PER_GROUP).transpose(1, 0, 2)
                              .reshape(SSD_GROUPS, 2 * SSD_HEADS_PER_GROUP), SSD_HEAD_DIM, axis=1)
            .reshape(SSD_GROUPS, 1, 2 * GROUP_X),
            dskip_x=jnp.repeat(ssd_d[li], SSD_HEAD_DIM).reshape(SSD_GROUPS, 1, GROUP_X),
        )
        ys_l, ys_c = _ssd(ul3, uc3, cols_full, cols_c, dt_l, dt_c, dtt_l, dtt_c, ssd_p)

        lam_init = 0.8 - 0.6 * math.exp(-0.3 * li)
        sw = att_subln_w[li].reshape(1, ATT_V_DIM)
        ya_l = _attention(ul3, COL_Q, uc3, cols_c, ul3, cols_full, rope_tabs, diff_lambda[li], sw,
                          lam_init=lam_init, tq=tq)

        w_bs = w_br_ssd[li].astype(BF16)
        w_ba = w_br_att[li].astype(BF16)
        w_o = w_out[li].astype(BF16)
        sn = ssd_norm_w[li].reshape(1, SSD_WIDTH)
        n2 = norm2_w[li].reshape(1, d)
        w_u = w_up[li].astype(BF16)
        cw = ffn_conv_w[li]
        cb = ffn_conv_b[li].reshape(1, 2 * D_FF)
        w_d = w_down[li].astype(BF16)
        fw = final_norm_w.reshape(1, d)

        if with_ctx:
            ya_c = _attention(uc3, cols_c["q"], uc3, cols_c, None, None, None, diff_lambda[li], sw,
                              lam_init=lam_init, tq=n_ctx)
            xc = _merge(ys_c.reshape(-1, SSD_WIDTH), uc, cols_c, ya_c.reshape(-1, ATT_WIDTH), xc, mod3,
                        ctx_rowf, sn, w_bs, w_ba, w_o, tm=tm_merge_c)
            xc = _conv_ffn(xc, mod3, ctx_rowf, n2, w_u, cw, cb, w_d, fw, tm=n_ctx, tf=tf,
                           seg_len=n_ctx, final_norm=False)

        xl = _merge(ys_l.reshape(-1, SSD_WIDTH), ul, cols_full, ya_l.reshape(-1, ATT_WIDTH), xl, mod3,
                    lat_row(tm_merge), sn, w_bs, w_ba, w_o, tm=tm_merge)
        xl = _conv_ffn(xl, mod3, lat_row(tm_ffn), n2, w_u, cw, cb, w_d, fw, tm=tm_ffn, tf=tf,
                       seg_len=n_lat, final_norm=not with_ctx)

    return xl.reshape(bsz, n_lat, d)
```

```python
import functools
import math

import jax
import jax.numpy as jnp
import numpy as np
from jax import lax
from jax.experimental import pallas as pl
from jax.experimental.pallas import tpu as pltpu

F32 = jnp.float32
BF16 = jnp.bfloat16

D_MODEL = 1024
EPS = 1e-6
N_MOD = 6
GRID_W = 64

SSD_HEADS = 16
SSD_HEAD_DIM = 64
SSD_WIDTH = SSD_HEADS * SSD_HEAD_DIM
SSD_GROUPS = 4
SSD_HEADS_PER_GROUP = SSD_HEADS // SSD_GROUPS
SSD_STATE = 128
SSD_CHUNK = 128
SSD_BC_WIDTH = SSD_GROUPS * SSD_STATE
SSD_CONV_CH = SSD_WIDTH + 2 * SSD_BC_WIDTH
GROUP_X = SSD_HEADS_PER_GROUP * SSD_HEAD_DIM
SSD_GROUP_CHUNKS = 8

ATT_HEADS = 8
ATT_HEAD_DIM = 64
ATT_V_DIM = 2 * ATT_HEAD_DIM
ATT_QK_WIDTH = ATT_HEADS * 2 * ATT_HEAD_DIM
ATT_WIDTH = ATT_HEADS * ATT_V_DIM
ATT_SCALE = ATT_HEAD_DIM ** -0.5
LOG2_E = math.log2(math.e)
ATT_SUB_ROWS = 256
ATT_CHAINS_AHEAD = 2
ROPE_BASE = 10000.0
ROPE_PAIRS = ATT_HEAD_DIM // 4

D_FF = 2816
FFN_DOWN_GROUP = 4
N_DT = 2 * SSD_HEADS

COL_Z = 0
COL_Q = COL_Z + SSD_WIDTH
COL_GS = COL_Q + ATT_QK_WIDTH
COL_GA = COL_GS + D_MODEL
COL_XS = COL_GA + D_MODEL
COL_B = COL_XS + SSD_WIDTH
COL_C = COL_B + SSD_BC_WIDTH
COL_K = COL_C + SSD_BC_WIDTH
COL_V = COL_K + ATT_QK_WIDTH
N_COLS = COL_V + ATT_WIDTH

LANES = 128
BF16_ROWS = 16
VMEM_LIMIT = 56 * 1024 * 1024


def _sigmoid(x):
    return 0.5 * jnp.tanh(0.5 * x) + 0.5


def _silu(x):
    return x * _sigmoid(x)


def _softplus(x):
    return jnp.maximum(x, 0.0) + jnp.log1p(jnp.exp(-jnp.abs(x)))


def _bdot(a, b):
    return jnp.dot(a.astype(BF16), b.astype(BF16), preferred_element_type=F32)


def _bdot_nt(a, b):
    return lax.dot_general(a.astype(BF16), b.astype(BF16), (((1,), (1,)), ((), ())),
                           preferred_element_type=F32)


def _split3(x):
    hi = x.astype(BF16)
    r1 = x - hi.astype(F32)
    mid = r1.astype(BF16)
    lo = (r1 - mid.astype(F32)).astype(BF16)
    return hi, mid, lo


def _dot2_rhs(a_bf16, x):
    hi = x.astype(BF16)
    lo = (x - hi.astype(F32)).astype(BF16)
    return (jnp.dot(a_bf16, hi, preferred_element_type=F32)
            + jnp.dot(a_bf16, lo, preferred_element_type=F32))


def _dot2_lhs(x, b_bf16):
    hi = x.astype(BF16)
    lo = (x - hi.astype(F32)).astype(BF16)
    return (jnp.dot(hi, b_bf16, preferred_element_type=F32)
            + jnp.dot(lo, b_bf16, preferred_element_type=F32))


def _cparams(sem):
    return pltpu.CompilerParams(dimension_semantics=sem, vmem_limit_bytes=VMEM_LIMIT)


def _mod_kernel(c_ref, w_ref, b_ref, o_ref):
    a = _silu(c_ref[...])
    o_ref[...] = _dot3_both(a, w_ref[...]) + b_ref[...]


def _dot3_both(a, w):
    ah, am, al = _split3(a)
    wh, wm, wl = _split3(w)
    d = lambda p, q: jnp.dot(p, q, preferred_element_type=F32)
    return (d(ah, wh) + (d(ah, wm) + d(am, wh))
            + (d(ah, wl) + d(am, wm) + d(al, wh)))


def _modulation(cc, w_mod, b_mod):
    depth = w_mod.shape[0]
    tn = 1024
    return pl.pallas_call(
        _mod_kernel,
        grid=(depth, N_MOD * D_MODEL // tn),
        in_specs=[
            pl.BlockSpec((16, D_MODEL), lambda l, j: (0, 0)),
            pl.BlockSpec((None, D_MODEL, tn), lambda l, j: (l, 0, j)),
            pl.BlockSpec((None, 1, tn), lambda l, j: (l, 0, j)),
        ],
        out_specs=pl.BlockSpec((None, 16, tn), lambda l, j: (l, 0, j)),
        out_shape=jax.ShapeDtypeStruct((depth, 16, N_MOD * D_MODEL), F32),
        compiler_params=_cparams(("parallel", "parallel")),
        name="modulation",
    )(cc, w_mod, b_mod.reshape(depth, 1, N_MOD * D_MODEL))


def _rope(x, cos, sin_a, sin_b):
    return x * cos + pltpu.roll(x, LANES - ROPE_PAIRS, axis=1) * sin_a + pltpu.roll(x, ROPE_PAIRS, axis=1) * sin_b


def _in_proj_kernel(x_ref, sh_ref, sc_ref, nw_ref, w_ref, wdt_ref, wdtt_ref,
                    u_ref, dt_ref, dtt_ref, h_ref):
    @pl.when(pl.program_id(1) == 0)
    def _():
        x = x_ref[...]
        r = lax.rsqrt(jnp.mean(x * x, axis=-1, keepdims=True) + EPS)
        h = (x * r * nw_ref[...]) * (1.0 + sc_ref[0]) + sh_ref[0]
        hb = h.astype(BF16)
        h_ref[...] = hb
        dt_ref[...] = jnp.dot(hb, wdt_ref[...], preferred_element_type=F32)
        dtt_ref[...] = lax.dot_general(wdtt_ref[...], hb, (((1,), (1,)), ((), ())),
                                       preferred_element_type=F32)

    u_ref[...] = jnp.dot(h_ref[...], w_ref[...], preferred_element_type=F32).astype(BF16)


def _in_proj(x2d, mod3, mod_row, norm_w, w_main, w_dt, w_dtt, *, tm, tn, col0):
    rows = x2d.shape[0]
    n_out = w_main.shape[1] - col0
    cb0 = col0 // tn
    return pl.pallas_call(
        _in_proj_kernel,
        grid=(rows // tm, n_out // tn),
        in_specs=[
            pl.BlockSpec((tm, D_MODEL), lambda i, j: (i, 0)),
            pl.BlockSpec((1, 1, D_MODEL), lambda i, j: (mod_row(i), 0, 0)),
            pl.BlockSpec((1, 1, D_MODEL), lambda i, j: (mod_row(i), 0, 1)),
            pl.BlockSpec((1, D_MODEL), lambda i, j: (0, 0)),
            pl.BlockSpec((D_MODEL, tn), lambda i, j: (0, j + cb0)),
            pl.BlockSpec((D_MODEL, LANES), lambda i, j: (0, 0)),
            pl.BlockSpec((N_DT, D_MODEL), lambda i, j: (0, 0)),
        ],
        out_specs=[
            pl.BlockSpec((tm, tn), lambda i, j: (i, j)),
            pl.BlockSpec((tm, LANES), lambda i, j: (i, 0)),
            pl.BlockSpec((N_DT, tm), lambda i, j: (0, i)),
        ],
        out_shape=[
            jax.ShapeDtypeStruct((rows, n_out), BF16),
            jax.ShapeDtypeStruct((rows, LANES), F32),
            jax.ShapeDtypeStruct((N_DT, rows), F32),
        ],
        scratch_shapes=[pltpu.VMEM((tm, D_MODEL), BF16)],
        compiler_params=_cparams(("parallel", "arbitrary")),
        name="in_proj",
    )(x2d, mod3, mod3, norm_w, w_main, w_dt, w_dtt)


def _ssd_kernel(xl_ref, bl_ref, cl_ref, xc_ref, bc_ref, cc_ref,
                dtl_ref, dtc_ref, dttl_ref, dttc_ref,
                cwx_ref, cwb_ref, cwc_ref, cbx_ref, cbb_ref, cbc_ref,
                biasr_ref, biasc_ref, alogx_ref, alogc_ref, dskip_ref,
                yl_ref, yc_ref,
                cm_s, y_s, ex_s, st_s, dec_s, hin_s, *, n_lat, n_ctx):
    t = SSD_CHUNK
    g = pl.program_id(1)
    nc_ctx = n_ctx // t
    nc_lat = n_lat // t
    n_chunks = nc_ctx + nc_lat
    gx = GROUP_X

    ri = lax.broadcasted_iota(jnp.int32, (t, t), 0)
    ci = lax.broadcasted_iota(jnp.int32, (t, t), 1)
    lower = ri >= ci
    tril = lower.astype(BF16)
    triu = (ri <= ci).astype(BF16)
    er = lax.broadcasted_iota(jnp.int32, (LANES, 2 * gx), 0)
    ec = lax.broadcasted_iota(jnp.int32, (LANES, 2 * gx), 1)
    expand = (er == g * (2 * SSD_HEADS_PER_GROUP) + ec // SSD_HEAD_DIM).astype(BF16)
    head_of_lane = lax.broadcasted_iota(jnp.int32, (t, gx), 1) // SSD_HEAD_DIM
    tt = jnp.concatenate([triu, tril], axis=1)
    a_x = -jnp.exp(alogx_ref[0]) * LOG2_E
    a_c = -jnp.exp(alogc_ref[...]) * LOG2_E

    si = lax.broadcasted_iota(jnp.int32, (2 * t, t + 2 * BF16_ROWS), 0)
    sj = lax.broadcasted_iota(jnp.int32, (2 * t, t + 2 * BF16_ROWS), 1)
    src = jnp.where(si < t, jnp.where(si == 0, t + BF16_ROWS - 1, si - 1),
                    jnp.where(si == 2 * t - 1, t + BF16_ROWS, si - t + 1))
    shift_m = (sj == src).astype(BF16)
    conv_w = jnp.concatenate([cwx_ref[...], cwb_ref[...], cwc_ref[...]], axis=1)
    conv_b = jnp.concatenate([cbx_ref[...], cbb_ref[...], cbc_ref[...]], axis=1)

    def phase_a(cs, x_ref, b_ref, c_ref, dt_ref, dtt_ref, n_seg_chunks, chunk0):
        seg_rows = n_seg_chunks * t
        n8 = 2 * SSD_HEADS_PER_GROUP
        st = [dict(c=c, r0=pl.multiple_of(c * t, t), o0=pl.multiple_of((c + chunk0) * t, t)) for c in cs]

        for s in st:
            c, r0 = s["c"], s["r0"]
            lo = pl.multiple_of(jnp.maximum(r0 - BF16_ROWS, 0), BF16_ROWS)
            hi = pl.multiple_of(jnp.minimum(r0 + t, seg_rows - BF16_ROWS), BF16_ROWS)
            rows = lambda a, n: jnp.concatenate(
                [x_ref[0, pl.ds(a, n), :], b_ref[0, pl.ds(a, n), :], c_ref[0, pl.ds(a, n), :]], axis=1)
            cur = rows(r0, t)
            zero = jnp.zeros((BF16_ROWS, cur.shape[1]), BF16)
            before = jnp.where(c > 0, rows(lo, BF16_ROWS), zero)
            after = jnp.where(c < n_seg_chunks - 1, rows(hi, BF16_ROWS), zero)
            s["cur"] = cur
            s["sh"] = jnp.dot(shift_m, jnp.concatenate([cur, before, after], axis=0),
                              preferred_element_type=F32)
            dt = _softplus(dt_ref[0, pl.ds(r0, t), :] + biasr_ref[...])
            s["dtx"] = _dot2_lhs(dt, expand)
            at = _softplus(dtt_ref[:, pl.ds(r0, t)] + biasc_ref[...]) * a_c
            hi3, mid3, lo3 = _split3(at)
            s["ct3"] = jnp.dot(jnp.concatenate([hi3, mid3, lo3], axis=0), tt, preferred_element_type=F32)

        for s in st:
            ax = s["dtx"] * a_x
            s["csf"] = _dot2_rhs(tril, ax[:, :gx])
            s["csb"] = _dot2_rhs(triu, ax[:, gx:])
            sh = s["sh"]
            xbc = _silu(sh[:t] * conv_w[0:1] + s["cur"].astype(F32) * conv_w[1:2]
                        + sh[t:] * conv_w[2:3] + conv_b)
            s["xs"] = xbc[:, :gx]
            s["bm"] = xbc[:, gx:gx + SSD_STATE].astype(BF16)
            s["cm"] = xbc[:, gx + SSD_STATE:].astype(BF16)
            cm_s[pl.ds(s["o0"], t), :] = s["cm"]
            s["cb"] = _bdot_nt(s["cm"], s["bm"])

        for s in st:
            ct3 = s["ct3"]
            cst = ct3[0:n8] + ct3[n8:2 * n8] + ct3[2 * n8:3 * n8]
            xs, dtx, cb = s["xs"], s["dtx"], s["cb"]
            y = dskip_ref[0] * xs
            for d, cs_d in ((0, s["csf"]), (1, s["csb"])):
                mask = lower if d == 0 else (ri <= ci)
                xd = (xs * dtx[:, d * gx:(d + 1) * gx]).astype(BF16)
                ms, blocks = [], []
                for r in range(SSD_HEADS_PER_GROUP):
                    col = cs_d[:, r * SSD_HEAD_DIM:r * SSD_HEAD_DIM + 1]
                    k = d * SSD_HEADS_PER_GROUP + r
                    row = cst[k:k + 1, d * t:(d + 1) * t]
                    seg = jnp.exp2(jnp.where(mask, col - row, -jnp.inf))
                    ms.append((cb * seg).astype(BF16))
                    blocks.append(jnp.where(head_of_lane == r, xd, jnp.zeros_like(xd)))
                y = y + jnp.dot(jnp.concatenate(ms, axis=1), jnp.concatenate(blocks, axis=0),
                                preferred_element_type=F32)
            y_s[pl.ds(s["o0"], t), :] = y

        for s in st:
            xs, dtx, csf, csb = s["xs"], s["dtx"], s["csf"], s["csb"]
            endf = csf[t - 1:t, :]
            endb = csb[0:1, :]
            ex_s[pl.ds(s["o0"], t), :] = jnp.exp2(jnp.concatenate([csf, csb], axis=1))
            xdd = jnp.concatenate([xs * dtx[:, :gx] * jnp.exp2(endf - csf),
                                   xs * dtx[:, gx:] * jnp.exp2(endb - csb)], axis=1).astype(BF16)
            bt = jnp.transpose(s["bm"].astype(F32)).astype(BF16)
            st_s[s["c"] + chunk0] = jnp.dot(bt, xdd, preferred_element_type=F32)
            dec_s[pl.ds(s["c"] + chunk0, 1), :] = jnp.exp2(jnp.concatenate([endf, endb], axis=1))

    def run_phase_a(n_seg_chunks, chunk0, refs):
        group = math.gcd(n_seg_chunks, SSD_GROUP_CHUNKS)

        def body(i, carry):
            phase_a([i * group + k for k in range(group)], *refs, n_seg_chunks, chunk0)
            return carry

        lax.fori_loop(0, n_seg_chunks // group, body, 0)

    run_phase_a(nc_ctx, 0, (xc_ref, bc_ref, cc_ref, dtc_ref, dttc_ref))
    run_phase_a(nc_lat, nc_ctx, (xl_ref, bl_ref, cl_ref, dtl_ref, dttl_ref))

    order_f = list(range(n_chunks))
    order_b = list(range(nc_ctx - 1, -1, -1)) + list(range(n_chunks - 1, nc_ctx - 1, -1))
    for d, order in ((0, order_f), (1, order_b)):
        h = jnp.zeros((SSD_STATE, gx), F32)
        for c in order:
            hin_s[c, :, d * gx:(d + 1) * gx] = h.astype(BF16)
            h = dec_s[c:c + 1, d * gx:(d + 1) * gx] * h + st_s[c, :, d * gx:(d + 1) * gx]

    def phase_c(cs, out_ref, chunk0):
        offs = [pl.multiple_of((c + chunk0) * t, t) for c in cs]
        yos = [jnp.dot(cm_s[pl.ds(o0, t), :], hin_s[c + chunk0], preferred_element_type=F32)
               for c, o0 in zip(cs, offs)]
        for c, o0, yo in zip(cs, offs, yos):
            yo = yo * ex_s[pl.ds(o0, t), :]
            y = y_s[pl.ds(o0, t), :] + yo[:, :gx] + yo[:, gx:]
            out_ref[0, pl.ds(pl.multiple_of(c * t, t), t), :] = y.astype(BF16)

    def run_phase_c(n_seg_chunks, chunk0, out_ref):
        group = math.gcd(n_seg_chunks, SSD_GROUP_CHUNKS)

        def body(i, carry):
            phase_c([i * group + k for k in range(group)], out_ref, chunk0)
            return carry

        lax.fori_loop(0, n_seg_chunks // group, body, 0)

    run_phase_c(nc_ctx, 0, yc_ref)
    run_phase_c(nc_lat, nc_ctx, yl_ref)


def _ssd(ul3, uc3, cols_l, cols_c, dt_l, dt_c, dtt_l, dtt_c, p):
    bsz, n_lat, _ = ul3.shape
    n_ctx = uc3.shape[1]
    gx = GROUP_X
    n_chunks = (n_lat + n_ctx) // SSD_CHUNK
    n_tot = n_lat + n_ctx

    def seq_specs(cols, n):
        xb, bb, cb = cols["xs"] // gx, cols["b"] // SSD_STATE, cols["c"] // SSD_STATE
        return [
            pl.BlockSpec((1, n, gx), lambda b, g: (b, 0, xb + g)),
            pl.BlockSpec((1, n, SSD_STATE), lambda b, g: (b, 0, bb + g)),
            pl.BlockSpec((1, n, SSD_STATE), lambda b, g: (b, 0, cb + g)),
        ]

    xoff = 0
    boff = SSD_WIDTH // SSD_STATE
    coff = (SSD_WIDTH + SSD_BC_WIDTH) // SSD_STATE
    n8 = 2 * SSD_HEADS_PER_GROUP
    in_specs = (
        seq_specs(cols_l, n_lat) + seq_specs(cols_c, n_ctx) + [
            pl.BlockSpec((1, n_lat, LANES), lambda b, g: (b, 0, 0)),
            pl.BlockSpec((1, n_ctx, LANES), lambda b, g: (b, 0, 0)),
            pl.BlockSpec((n8, n_lat), lambda b, g: (g, b)),
            pl.BlockSpec((n8, n_ctx), lambda b, g: (g, b)),
            pl.BlockSpec((3, gx), lambda b, g: (0, xoff + g)),
            pl.BlockSpec((3, SSD_STATE), lambda b, g: (0, boff + g)),
            pl.BlockSpec((3, SSD_STATE), lambda b, g: (0, coff + g)),
            pl.BlockSpec((1, gx), lambda b, g: (0, xoff + g)),
            pl.BlockSpec((1, SSD_STATE), lambda b, g: (0, boff + g)),
            pl.BlockSpec((1, SSD_STATE), lambda b, g: (0, coff + g)),
            pl.BlockSpec((1, LANES), lambda b, g: (0, 0)),
            pl.BlockSpec((n8, 1), lambda b, g: (g, 0)),
            pl.BlockSpec((1, 1, 2 * gx), lambda b, g: (g, 0, 0)),
            pl.BlockSpec((n8, 1), lambda b, g: (g, 0)),
            pl.BlockSpec((1, 1, gx), lambda b, g: (g, 0, 0)),
        ])
    return pl.pallas_call(
        functools.partial(_ssd_kernel, n_lat=n_lat, n_ctx=n_ctx),
        grid=(bsz, SSD_GROUPS),
        in_specs=in_specs,
        out_specs=[
            pl.BlockSpec((1, n_lat, gx), lambda b, g: (b, 0, g)),
            pl.BlockSpec((1, n_ctx, gx), lambda b, g: (b, 0, g)),
        ],
        out_shape=[
            jax.ShapeDtypeStruct((bsz, n_lat, SSD_WIDTH), BF16),
            jax.ShapeDtypeStruct((bsz, n_ctx, SSD_WIDTH), BF16),
        ],
        scratch_shapes=[
            pltpu.VMEM((n_tot, SSD_STATE), BF16),
            pltpu.VMEM((n_tot, gx), F32),
            pltpu.VMEM((n_tot, 2 * gx), F32),
            pltpu.VMEM((n_chunks, SSD_STATE, 2 * gx), F32),
            pltpu.VMEM((n_chunks + (-n_chunks) % 8, 2 * gx), F32),
            pltpu.VMEM((n_chunks, SSD_STATE, 2 * gx), BF16),
        ],
        compiler_params=_cparams(("parallel", "parallel")),
        name="ssd_scan",
    )(ul3, ul3, ul3, uc3, uc3, uc3,
      dt_l.reshape(bsz, n_lat, LANES), dt_c.reshape(bsz, n_ctx, LANES), dtt_l, dtt_c,
      p["conv_w"], p["conv_w"], p["conv_w"], p["conv_b"], p["conv_b"], p["conv_b"],
      p["bias_row"], p["bias_col"], p["alog_x"], p["alog_col"], p["dskip_x"])


def _attn_kernel(*refs, n_ctx, n_lat, tq, lam_init):
    if n_lat:
        (q_ref, kc_ref, vc_ref, kl_ref, vl_ref, cos_ref, sa_ref, sb_ref, lam_ref, sw_ref,
         o_ref, kl_s) = refs
    else:
        q_ref, kc_ref, vc_ref, lam_ref, sw_ref, o_ref = refs
    qi = pl.program_id(2)
    hd = ATT_HEAD_DIM

    if n_lat:
        @pl.when(qi == 0)
        def _():
            kl_s[...] = _rope(kl_ref[0].astype(F32), cos_ref[...], sa_ref[...], sb_ref[...]).astype(BF16)

    lf = lam_ref[...]
    lam = (jnp.exp(jnp.sum(lf[0:1] * lf[1:2], axis=-1, keepdims=True))
           - jnp.exp(jnp.sum(lf[2:3] * lf[3:4], axis=-1, keepdims=True)) + lam_init)

    q = q_ref[0].astype(F32)
    if n_lat:
        r0 = pl.multiple_of(qi * tq, tq)
        q = _rope(q, cos_ref[pl.ds(r0, tq), :], sa_ref[pl.ds(r0, tq), :], sb_ref[pl.ds(r0, tq), :])
    q = (q * (ATT_SCALE * LOG2_E)).astype(BF16)
    lane = lax.broadcasted_iota(jnp.int32, (1, ATT_V_DIM), 1)
    q_comp = [jnp.where(lane < hd, q, jnp.zeros_like(q)), jnp.where(lane >= hd, q, jnp.zeros_like(q))]

    sub = min(ATT_SUB_ROWS, tq)
    chains = [(r, c) for r in range(tq // sub) for c in range(2)]
    nt = (((1,), (1,)), ((), ()))

    def scores(r, c):
        qc = q_comp[c][r * sub:(r + 1) * sub, :]
        s = lax.dot_general(qc, kc_ref[0], nt, preferred_element_type=F32)
        if n_lat:
            s = jnp.concatenate([s, lax.dot_general(qc, kl_s[...], nt, preferred_element_type=F32)],
                                axis=1)
        return s

    def weighted_values(e):
        pv = jnp.dot(e[:, :n_ctx], vc_ref[0], preferred_element_type=F32)
        if n_lat:
            pv = pv + jnp.dot(e[:, n_ctx:], vl_ref[0], preferred_element_type=F32)
        return pv

    ahead = ATT_CHAINS_AHEAD
    pending = [scores(*ch) for ch in chains[:ahead]]
    outs = {}
    for n, (r, c) in enumerate(chains):
        s = pending.pop(0)
        if n + ahead < len(chains):
            pending.append(scores(*chains[n + ahead]))
        e = jnp.exp2(s - jnp.max(s, axis=-1, keepdims=True))
        inv = 1.0 / jnp.sum(e, axis=-1, keepdims=True)
        outs[c] = weighted_values(e.astype(BF16)) * inv
        if c == 1:
            o = outs[0] - lam * outs[1]
            on = o * lax.rsqrt(jnp.mean(o * o, axis=-1, keepdims=True) + EPS) * sw_ref[...]
            o_ref[0, r * sub:(r + 1) * sub, :] = (on * (1.0 - lam_init)).astype(BF16)


def _attention(uq3, col_q, uc3, cols_c, ul3, cols_l, rope_tabs, lam_p, subln_w, *, lam_init, tq):
    bsz, n_q, _ = uq3.shape
    n_ctx = uc3.shape[1]
    n_lat = 0 if ul3 is None else ul3.shape[1]
    vd = ATT_V_DIM
    qb = col_q // vd
    in_specs = [
        pl.BlockSpec((1, tq, vd), lambda b, h, i: (b, i, qb + h)),
        pl.BlockSpec((1, n_ctx, vd), lambda b, h, i: (b, 0, cols_c["k"] // vd + h)),
        pl.BlockSpec((1, n_ctx, vd), lambda b, h, i: (b, 0, cols_c["v"] // vd + h)),
    ]
    args = [uq3, uc3, uc3]
    if n_lat:
        in_specs += [
            pl.BlockSpec((1, n_lat, vd), lambda b, h, i: (b, 0, cols_l["k"] // vd + h)),
            pl.BlockSpec((1, n_lat, vd), lambda b, h, i: (b, 0, cols_l["v"] // vd + h)),
            pl.BlockSpec((n_lat, vd), lambda b, h, i: (0, 0)),
            pl.BlockSpec((n_lat, vd), lambda b, h, i: (0, 0)),
            pl.BlockSpec((n_lat, vd), lambda b, h, i: (0, 0)),
        ]
        args += [ul3, ul3, *rope_tabs]
    in_specs += [
        pl.BlockSpec((4, ATT_HEAD_DIM), lambda b, h, i: (0, 0)),
        pl.BlockSpec((1, vd), lambda b, h, i: (0, 0)),
    ]
    args += [lam_p, subln_w]
    return pl.pallas_call(
        functools.partial(_attn_kernel, n_ctx=n_ctx, n_lat=n_lat, tq=tq, lam_init=lam_init),
        grid=(bsz, ATT_HEADS, n_q // tq),
        in_specs=in_specs,
        out_specs=pl.BlockSpec((1, tq, vd), lambda b, h, i: (b, i, h)),
        out_shape=jax.ShapeDtypeStruct((bsz, n_q, ATT_WIDTH), BF16),
        scratch_shapes=[pltpu.VMEM((n_lat, vd), BF16)] if n_lat else [],
        compiler_params=_cparams(("parallel", "parallel", "arbitrary")),
        name="diff_attn",
    )(*args)


def _merge_kernel(ys_ref, z_ref, ya_ref, gs_ref, ga_ref, x_ref, g1_ref, nw_ref,
                  wbs_ref, wba_ref, wo_ref, o_ref):
    ta = _sigmoid(ga_ref[...].astype(F32)) * jnp.dot(ya_ref[...], wba_ref[...], preferred_element_type=F32)
    yz = ys_ref[...].astype(F32) * _silu(z_ref[...].astype(F32))
    ysn = yz * lax.rsqrt(jnp.mean(yz * yz, axis=-1, keepdims=True) + EPS) * nw_ref[...]
    ts = jnp.dot(ysn.astype(BF16), wbs_ref[...], preferred_element_type=F32)
    tmix = _sigmoid(gs_ref[...].astype(F32)) * ts + ta
    o_ref[...] = x_ref[...] + g1_ref[0] * jnp.dot(tmix.astype(BF16), wo_ref[...],
                                                   preferred_element_type=F32)


def _merge(ys2, u2, cols, ya2, x2d, mod3, mod_row, norm_w, w_bs, w_ba, w_o, *, tm):
    rows = x2d.shape[0]
    d = D_MODEL
    row_blk = lambda cb: pl.BlockSpec((tm, d), lambda i: (i, cb))
    full = lambda shape: pl.BlockSpec(shape, lambda i: (0, 0))
    return pl.pallas_call(
        _merge_kernel,
        grid=(rows // tm,),
        in_specs=[
            row_blk(0), row_blk(cols["z"] // d), row_blk(0),
            row_blk(cols["gs"] // d), row_blk(cols["ga"] // d), row_blk(0),
            pl.BlockSpec((1, 1, d), lambda i: (mod_row(i), 0, 2)),
            full((1, d)), full((d, d)), full((d, d)), full((d, d)),
        ],
        out_specs=row_blk(0),
        out_shape=jax.ShapeDtypeStruct((rows, d), F32),
        compiler_params=_cparams(("parallel",)),
        name="branch_merge",
    )(ys2, u2, ya2, u2, u2, x2d, mod3, norm_w, w_bs, w_ba, w_o)


def _ffn_kernel(x_ref, xp_ref, xn_ref, sh_ref, sc_ref, g2_ref, nw_ref, wu_ref, cw_ref, cb_ref,
                wd_ref, fw_ref, o_ref, h_s, u_s, act_s, acc_s, *, tm, tf, tiles_per_seg, final_norm):
    i = pl.program_id(0)
    halo = BF16_ROWS
    seg_first = (i % tiles_per_seg) == 0
    seg_last = (i % tiles_per_seg) == tiles_per_seg - 1

    def norm_mod(x):
        r = lax.rsqrt(jnp.mean(x * x, axis=-1, keepdims=True) + EPS)
        return ((x * r * nw_ref[...]) * (1.0 + sc_ref[0]) + sh_ref[0]).astype(BF16)

    h_s[0:halo, :] = norm_mod(xp_ref[...])
    h_s[halo:halo + tm, :] = norm_mod(x_ref[...])
    h_s[halo + tm:2 * halo + tm, :] = norm_mod(xn_ref[...])

    def up_proj(j):
        us = u_s.at[j % 2]
        for half in range(2):
            c0 = half * D_FF + j * tf
            us[:, half * tf:(half + 1) * tf] = jnp.dot(h_s[...], wu_ref[:, c0:c0 + tf],
                                                       preferred_element_type=F32)
        us[halo - 1:halo, :] = jnp.where(seg_first, 0.0, us[halo - 1:halo, :])
        us[halo + tm:halo + tm + 1, :] = jnp.where(seg_last, 0.0, us[halo + tm:halo + tm + 1, :])

    n_f = D_FF // tf
    bounds = [0]
    while bounds[-1] < n_f:
        left = n_f - bounds[-1]
        bounds.append(bounds[-1] + (FFN_DOWN_GROUP if left > FFN_DOWN_GROUP + 2 else (left + 1) // 2))
    group_of = {j: g for g in range(len(bounds) - 1) for j in range(bounds[g], bounds[g + 1])}

    def down_proj(g):
        j0, j1 = bounds[g], bounds[g + 1]
        down = jnp.dot(act_s[g % 2, :, 0:(j1 - j0) * tf], wd_ref[j0 * tf:j1 * tf, :],
                       preferred_element_type=F32)
        if g == 0:
            acc_s[...] = down
        else:
            acc_s[...] += down

    up_proj(0)
    for j in range(n_f):
        if j + 1 < n_f:
            up_proj(j + 1)
        if j in bounds[1:]:
            down_proj(group_of[j] - 1)
        us = u_s.at[j % 2]
        w = jnp.concatenate([cw_ref[:, j * tf:(j + 1) * tf],
                             cw_ref[:, D_FF + j * tf:D_FF + (j + 1) * tf]], axis=1)
        b = jnp.concatenate([cb_ref[:, j * tf:(j + 1) * tf],
                             cb_ref[:, D_FF + j * tf:D_FF + (j + 1) * tf]], axis=1)
        uc = (us[halo - 1:halo - 1 + tm, :] * w[0:1] + us[halo:halo + tm, :] * w[1:2]
              + us[halo + 1:halo + 1 + tm, :] * w[2:3] + b)
        g, k = group_of[j], j - bounds[group_of[j]]
        act_s[g % 2, :, k * tf:(k + 1) * tf] = (_silu(uc[:, :tf]) * uc[:, tf:]).astype(BF16)
    down_proj(len(bounds) - 2)
    y = x_ref[...] + g2_ref[0] * acc_s[...]
    if final_norm:
        y = y * lax.rsqrt(jnp.mean(y * y, axis=-1, keepdims=True) + EPS) * fw_ref[...]
    o_ref[...] = y


def _conv_ffn(x2d, mod3, mod_row, norm_w, w_up, conv_w, conv_b, w_down, final_w, *,
              tm, tf, seg_len, final_norm):
    rows = x2d.shape[0]
    d = D_MODEL
    halo = BF16_ROWS
    hb = tm // halo
    last_blk = rows // halo - 1
    assert seg_len % tm == 0
    resident = lambda shape: pl.BlockSpec(shape, lambda i: (0, 0), pipeline_mode=pl.Buffered(1))
    return pl.pallas_call(
        functools.partial(_ffn_kernel, tm=tm, tf=tf, tiles_per_seg=seg_len // tm, final_norm=final_norm),
        grid=(rows // tm,),
        in_specs=[
            pl.BlockSpec((tm, d), lambda i: (i, 0)),
            pl.BlockSpec((halo, d), lambda i: (jnp.maximum(i * hb - 1, 0), 0)),
            pl.BlockSpec((halo, d), lambda i: (jnp.minimum((i + 1) * hb, last_blk), 0)),
            pl.BlockSpec((1, 1, d), lambda i: (mod_row(i), 0, 3)),
            pl.BlockSpec((1, 1, d), lambda i: (mod_row(i), 0, 4)),
            pl.BlockSpec((1, 1, d), lambda i: (mod_row(i), 0, 5)),
            resident((1, d)),
            resident((d, 2 * D_FF)),
            resident((3, 2 * D_FF)),
            resident((1, 2 * D_FF)),
            resident((D_FF, d)),
            resident((1, d)),
        ],
        out_specs=pl.BlockSpec((tm, d), lambda i: (i, 0)),
        out_shape=jax.ShapeDtypeStruct((rows, d), F32),
        scratch_shapes=[
            pltpu.VMEM((tm + 2 * halo, d), BF16),
            pltpu.VMEM((2, tm + 2 * halo, 2 * tf), F32),
            pltpu.VMEM((2, tm, FFN_DOWN_GROUP * tf), BF16),
            pltpu.VMEM((tm, d), F32),
        ],
        compiler_params=_cparams(("parallel",)),
        name="conv_ffn",
    )(x2d, x2d, x2d, mod3, mod3, mod3, norm_w, w_up, conv_w, conv_b, w_down, final_w)


def _split_in_weight(w):
    sizes = (SSD_WIDTH, SSD_CONV_CH, N_DT, ATT_QK_WIDTH, ATT_QK_WIDTH, ATT_WIDTH, 2 * D_MODEL)
    starts = np.concatenate([[0], np.cumsum(sizes)])
    z, xbc, dt, q, k, v, gates = (w[:, int(starts[n]):int(starts[n + 1])] for n in range(7))
    main = jnp.concatenate([z, q, gates, xbc, k, v], axis=1).astype(BF16)
    return main, _dt_param_order(dt.reshape(-1, 2, SSD_HEADS))


def _dt_param_order(p2h):
    lead = p2h.shape[:-2]
    p = p2h.reshape(*lead, 2, SSD_GROUPS, SSD_HEADS_PER_GROUP)
    return jnp.swapaxes(p, -3, -2).reshape(*lead, N_DT)


def _rope_tables(n_tokens):
    rows = n_tokens // GRID_W
    inv_freq = ROPE_BASE ** (-jnp.arange(ROPE_PAIRS, dtype=F32) / ROPE_PAIRS)
    ang_r = jnp.broadcast_to(jnp.arange(rows, dtype=F32)[:, None, None] * inv_freq, (rows, GRID_W, ROPE_PAIRS))
    ang_c = jnp.broadcast_to(jnp.arange(GRID_W, dtype=F32)[None, :, None] * inv_freq, (rows, GRID_W, ROPE_PAIRS))
    ang = jnp.stack([ang_r, ang_c], axis=2).reshape(n_tokens, 2, 1, ROPE_PAIRS)
    cos = jnp.broadcast_to(jnp.cos(ang), (n_tokens, 2, 2, ROPE_PAIRS))
    sin = jnp.broadcast_to(jnp.sin(ang), (n_tokens, 2, 2, ROPE_PAIRS))
    zero = jnp.zeros_like(sin[:, :, :1])
    sin_a = jnp.concatenate([-sin[:, :, :1], zero], axis=2)
    sin_b = jnp.concatenate([zero, sin[:, :, 1:]], axis=2)
    tile = lambda a: jnp.tile(a.reshape(n_tokens, ATT_HEAD_DIM), (1, 2))
    return tile(cos), tile(sin_a), tile(sin_b)


def kernel(x, c, ctx, c_ctx, w_mod, b_mod, norm1_w, w_in, ssd_conv_w, ssd_conv_b, ssd_a_log,
           ssd_dt_bias, ssd_d, ssd_norm_w, diff_lambda, att_subln_w, w_br_ssd, w_br_att, w_out,
           norm2_w, w_up, ffn_conv_w, ffn_conv_b, w_down, final_norm_w):
    bsz, n_lat, d = x.shape
    n_ctx = ctx.shape[1]
    depth = w_mod.shape[0]
    assert d == D_MODEL and bsz + 1 <= 16
    ctx_row = bsz

    tm_in = min(1024, n_lat)
    tm_in_c = min(1024, bsz * n_ctx)
    tn_in = 4096
    tm_merge = min(512, n_lat)
    tm_merge_c = min(512, bsz * n_ctx)
    tm_ffn = min(512, n_lat)
    tf = 256
    tq = min(2048, n_lat)

    cc = jnp.zeros((16, d), F32).at[:bsz].set(c).at[ctx_row].set(c_ctx)
    mod = _modulation(cc, w_mod, b_mod)

    rope_tabs = _rope_tables(n_lat)
    cols_full = dict(z=COL_Z, q=COL_Q, gs=COL_GS, ga=COL_GA, xs=COL_XS, b=COL_B, c=COL_C, k=COL_K, v=COL_V)

    xl = x.reshape(bsz * n_lat, d)
    xc = ctx.reshape(bsz * n_ctx, d)
    lat_row = lambda tm: (lambda i: (i * tm) // n_lat)
    ctx_rowf = lambda i: ctx_row

    for li in range(depth):
        with_ctx = li < depth - 1
        mod3 = mod[li].reshape(16, 1, N_MOD * d)
        w_main, w_dt32 = _split_in_weight(w_in[li])
        w_dt = jnp.pad(w_dt32, ((0, 0), (0, LANES - N_DT))).astype(BF16)
        w_dtt = w_dt32.T.astype(BF16)
        n1 = norm1_w[li].reshape(1, d)

        ul, dt_l, dtt_l = _in_proj(xl, mod3, lat_row(tm_in), n1, w_main, w_dt, w_dtt,
                                   tm=tm_in, tn=tn_in, col0=0)
        col0_c = 0 if with_ctx else COL_XS
        uc, dt_c, dtt_c = _in_proj(xc, mod3, ctx_rowf, n1, w_main, w_dt, w_dtt,
                                   tm=tm_in_c, tn=tn_in, col0=col0_c)
        cols_c = {k_: v_ - col0_c for k_, v_ in cols_full.items()}
        ul3 = ul.reshape(bsz, n_lat, -1)
        uc3 = uc.reshape(bsz, n_ctx, -1)

        ssd_p = dict(
            conv_w=ssd_conv_w[li], conv_b=ssd_conv_b[li].reshape(1, SSD_CONV_CH),
            bias_row=jnp.zeros((1, LANES), F32).at[0, :N_DT].set(_dt_param_order(ssd_dt_bias[li])),
            bias_col=_dt_param_order(ssd_dt_bias[li]).reshape(N_DT, 1),
            alog_col=_dt_param_order(ssd_a_log[li]).reshape(N_DT, 1),
            alog_x=jnp.repeat(ssd_a_log[li].reshape(2, SSD_GROUPS, SSD_HEADS_PER_GROUP).transpose(1, 0, 2)
                              .reshape(SSD_GROUPS, 2 * SSD_HEADS_PER_GROUP), SSD_HEAD_DIM, axis=1)
            .reshape(SSD_GROUPS, 1, 2 * GROUP_X),
            dskip_x=jnp.repeat(ssd_d[li], SSD_HEAD_DIM).reshape(SSD_GROUPS, 1, GROUP_X),
        )
        ys_l, ys_c = _ssd(ul3, uc3, cols_full, cols_c, dt_l, dt_c, dtt_l, dtt_c, ssd_p)

        lam_init = 0.8 - 0.6 * math.exp(-0.3 * li)
        sw = att_subln_w[li].reshape(1, ATT_V_DIM)
        ya_l = _attention(ul3, COL_Q, uc3, cols_c, ul3, cols_full, rope_tabs, diff_lambda[li], sw,
                          lam_init=lam_init, tq=tq)

        w_bs = w_br_ssd[li].astype(BF16)
        w_ba = w_br_att[li].astype(BF16)
        w_o = w_out[li].astype(BF16)
        sn = ssd_norm_w[li].reshape(1, SSD_WIDTH)
        n2 = norm2_w[li].reshape(1, d)
        w_u = w_up[li].astype(BF16)
        cw = ffn_conv_w[li]
        cb = ffn_conv_b[li].reshape(1, 2 * D_FF)
        w_d = w_down[li].astype(BF16)
        fw = final_norm_w.reshape(1, d)

        if with_ctx:
            ya_c = _attention(uc3, cols_c["q"], uc3, cols_c, None, None, None, diff_lambda[li], sw,
                              lam_init=lam_init, tq=n_ctx)
            xc = _merge(ys_c.reshape(-1, SSD_WIDTH), uc, cols_c, ya_c.reshape(-1, ATT_WIDTH), xc, mod3,
                        ctx_rowf, sn, w_bs, w_ba, w_o, tm=tm_merge_c)
            xc = _conv_ffn(xc, mod3, ctx_rowf, n2, w_u, cw, cb, w_d, fw, tm=n_ctx, tf=tf,
                           seg_len=n_ctx, final_norm=False)

        xl = _merge(ys_l.reshape(-1, SSD_WIDTH), ul, cols_full, ya_l.reshape(-1, ATT_WIDTH), xl, mod3,
                    lat_row(tm_merge), sn, w_bs, w_ba, w_o, tm=tm_merge)
        xl = _conv_ffn(xl, mod3, lat_row(tm_ffn), n2, w_u, cw, cb, w_d, fw, tm=tm_ffn, tf=tf,
                       seg_len=n_lat, final_norm=not with_ctx)

    return xl.reshape(bsz, n_lat, d)
```

```python
import functools
import math

import jax
import jax.numpy as jnp
import numpy as np
from jax import lax
from jax.experimental import pallas as pl
from jax.experimental.pallas import tpu as pltpu

F32 = jnp.float32
BF16 = jnp.bfloat16

D_MODEL = 1024
EPS = 1e-6
N_MOD = 6
GRID_W = 64

SSD_HEADS = 16
SSD_HEAD_DIM = 64
SSD_WIDTH = SSD_HEADS * SSD_HEAD_DIM
SSD_GROUPS = 4
SSD_HEADS_PER_GROUP = SSD_HEADS // SSD_GROUPS
SSD_STATE = 128
SSD_CHUNK = 128
SSD_BC_WIDTH = SSD_GROUPS * SSD_STATE
SSD_CONV_CH = SSD_WIDTH + 2 * SSD_BC_WIDTH
GROUP_X = SSD_HEADS_PER_GROUP * SSD_HEAD_DIM
SSD_GROUP_CHUNKS = 8

ATT_HEADS = 8
ATT_HEAD_DIM = 64
ATT_V_DIM = 2 * ATT_HEAD_DIM
ATT_QK_WIDTH = ATT_HEADS * 2 * ATT_HEAD_DIM
ATT_WIDTH = ATT_HEADS * ATT_V_DIM
ATT_SCALE = ATT_HEAD_DIM ** -0.5
LOG2_E = math.log2(math.e)
ATT_SUB_ROWS = 512
ATT_CHAINS_AHEAD = 2
ROPE_BASE = 10000.0
ROPE_PAIRS = ATT_HEAD_DIM // 4

D_FF = 2816
FFN_DOWN_GROUP = 4
N_DT = 2 * SSD_HEADS

COL_Z = 0
COL_Q = COL_Z + SSD_WIDTH
COL_GS = COL_Q + ATT_QK_WIDTH
COL_GA = COL_GS + D_MODEL
COL_XS = COL_GA + D_MODEL
COL_B = COL_XS + SSD_WIDTH
COL_C = COL_B + SSD_BC_WIDTH
COL_K = COL_C + SSD_BC_WIDTH
COL_V = COL_K + ATT_QK_WIDTH
N_COLS = COL_V + ATT_WIDTH

LANES = 128
BF16_ROWS = 16
VMEM_LIMIT = 56 * 1024 * 1024


def _sigmoid(x):
    return 0.5 * jnp.tanh(0.5 * x) + 0.5


def _silu(x):
    return x * _sigmoid(x)


def _softplus(x):
    return jnp.maximum(x, 0.0) + jnp.log1p(jnp.exp(-jnp.abs(x)))


def _bdot(a, b):
    return jnp.dot(a.astype(BF16), b.astype(BF16), preferred_element_type=F32)


def _bdot_nt(a, b):
    return lax.dot_general(a.astype(BF16), b.astype(BF16), (((1,), (1,)), ((), ())),
                           preferred_element_type=F32)


def _split3(x):
    hi = x.astype(BF16)
    r1 = x - hi.astype(F32)
    mid = r1.astype(BF16)
    lo = (r1 - mid.astype(F32)).astype(BF16)
    return hi, mid, lo


def _dot2_rhs(a_bf16, x):
    hi = x.astype(BF16)
    lo = (x - hi.astype(F32)).astype(BF16)
    return (jnp.dot(a_bf16, hi, preferred_element_type=F32)
            + jnp.dot(a_bf16, lo, preferred_element_type=F32))


def _dot2_lhs(x, b_bf16):
    hi = x.astype(BF16)
    lo = (x - hi.astype(F32)).astype(BF16)
    return (jnp.dot(hi, b_bf16, preferred_element_type=F32)
            + jnp.dot(lo, b_bf16, preferred_element_type=F32))


def _cparams(sem):
    return pltpu.CompilerParams(dimension_semantics=sem, vmem_limit_bytes=VMEM_LIMIT)


def _mod_kernel(c_ref, w_ref, b_ref, o_ref):
    a = _silu(c_ref[...])
    o_ref[...] = _dot3_both(a, w_ref[...]) + b_ref[...]


def _dot3_both(a, w):
    ah, am, al = _split3(a)
    wh, wm, wl = _split3(w)
    d = lambda p, q: jnp.dot(p, q, preferred_element_type=F32)
    return (d(ah, wh) + (d(ah, wm) + d(am, wh))
            + (d(ah, wl) + d(am, wm) + d(al, wh)))


def _modulation(cc, w_mod, b_mod):
    depth = w_mod.shape[0]
    tn = 1024
    return pl.pallas_call(
        _mod_kernel,
        grid=(depth, N_MOD * D_MODEL // tn),
        in_specs=[
            pl.BlockSpec((16, D_MODEL), lambda l, j: (0, 0)),
            pl.BlockSpec((None, D_MODEL, tn), lambda l, j: (l, 0, j)),
            pl.BlockSpec((None, 1, tn), lambda l, j: (l, 0, j)),
        ],
        out_specs=pl.BlockSpec((None, 16, tn), lambda l, j: (l, 0, j)),
        out_shape=jax.ShapeDtypeStruct((depth, 16, N_MOD * D_MODEL), F32),
        compiler_params=_cparams(("parallel", "parallel")),
        name="modulation",
    )(cc, w_mod, b_mod.reshape(depth, 1, N_MOD * D_MODEL))


def _rope(x, cos, sin_a, sin_b):
    return x * cos + pltpu.roll(x, LANES - ROPE_PAIRS, axis=1) * sin_a + pltpu.roll(x, ROPE_PAIRS, axis=1) * sin_b


def _in_proj_kernel(x_ref, sh_ref, sc_ref, nw_ref, w_ref, wdt_ref, wdtt_ref,
                    u_ref, dt_ref, dtt_ref, h_ref):
    @pl.when(pl.program_id(1) == 0)
    def _():
        x = x_ref[...]
        r = lax.rsqrt(jnp.mean(x * x, axis=-1, keepdims=True) + EPS)
        h = (x * r * nw_ref[...]) * (1.0 + sc_ref[0]) + sh_ref[0]
        hb = h.astype(BF16)
        h_ref[...] = hb
        dt_ref[...] = jnp.dot(hb, wdt_ref[...], preferred_element_type=F32)
        dtt_ref[...] = lax.dot_general(wdtt_ref[...], hb, (((1,), (1,)), ((), ())),
                                       preferred_element_type=F32)

    u_ref[...] = jnp.dot(h_ref[...], w_ref[...], preferred_element_type=F32).astype(BF16)


def _in_proj(x2d, mod3, mod_row, norm_w, w_main, w_dt, w_dtt, *, tm, tn, col0):
    rows = x2d.shape[0]
    n_out = w_main.shape[1] - col0
    cb0 = col0 // tn
    return pl.pallas_call(
        _in_proj_kernel,
        grid=(rows // tm, n_out // tn),
        in_specs=[
            pl.BlockSpec((tm, D_MODEL), lambda i, j: (i, 0)),
            pl.BlockSpec((1, 1, D_MODEL), lambda i, j: (mod_row(i), 0, 0)),
            pl.BlockSpec((1, 1, D_MODEL), lambda i, j: (mod_row(i), 0, 1)),
            pl.BlockSpec((1, D_MODEL), lambda i, j: (0, 0)),
            pl.BlockSpec((D_MODEL, tn), lambda i, j: (0, j + cb0)),
            pl.BlockSpec((D_MODEL, LANES), lambda i, j: (0, 0)),
            pl.BlockSpec((N_DT, D_MODEL), lambda i, j: (0, 0)),
        ],
        out_specs=[
            pl.BlockSpec((tm, tn), lambda i, j: (i, j)),
            pl.BlockSpec((tm, LANES), lambda i, j: (i, 0)),
            pl.BlockSpec((N_DT, tm), lambda i, j: (0, i)),
        ],
        out_shape=[
            jax.ShapeDtypeStruct((rows, n_out), BF16),
            jax.ShapeDtypeStruct((rows, LANES), F32),
            jax.ShapeDtypeStruct((N_DT, rows), F32),
        ],
        scratch_shapes=[pltpu.VMEM((tm, D_MODEL), BF16)],
        compiler_params=_cparams(("parallel", "arbitrary")),
        name="in_proj",
    )(x2d, mod3, mod3, norm_w, w_main, w_dt, w_dtt)


def _ssd_kernel(xl_ref, bl_ref, cl_ref, xc_ref, bc_ref, cc_ref,
                dtl_ref, dtc_ref, dttl_ref, dttc_ref,
                cwx_ref, cwb_ref, cwc_ref, cbx_ref, cbb_ref, cbc_ref,
                biasr_ref, biasc_ref, alogx_ref, alogc_ref, dskip_ref,
                yl_ref, yc_ref,
                cm_s, y_s, ex_s, st_s, dec_s, hin_s, *, n_lat, n_ctx):
    t = SSD_CHUNK
    g = pl.program_id(1)
    nc_ctx = n_ctx // t
    nc_lat = n_lat // t
    n_chunks = nc_ctx + nc_lat
    gx = GROUP_X

    ri = lax.broadcasted_iota(jnp.int32, (t, t), 0)
    ci = lax.broadcasted_iota(jnp.int32, (t, t), 1)
    lower = ri >= ci
    tril = lower.astype(BF16)
    triu = (ri <= ci).astype(BF16)
    er = lax.broadcasted_iota(jnp.int32, (LANES, 2 * gx), 0)
    ec = lax.broadcasted_iota(jnp.int32, (LANES, 2 * gx), 1)
    expand = (er == g * (2 * SSD_HEADS_PER_GROUP) + ec // SSD_HEAD_DIM).astype(BF16)
    head_of_lane = lax.broadcasted_iota(jnp.int32, (t, gx), 1) // SSD_HEAD_DIM
    tt = jnp.concatenate([triu, tril], axis=1)
    a_x = -jnp.exp(alogx_ref[0]) * LOG2_E
    a_c = -jnp.exp(alogc_ref[...]) * LOG2_E

    si = lax.broadcasted_iota(jnp.int32, (2 * t, t + 2 * BF16_ROWS), 0)
    sj = lax.broadcasted_iota(jnp.int32, (2 * t, t + 2 * BF16_ROWS), 1)
    src = jnp.where(si < t, jnp.where(si == 0, t + BF16_ROWS - 1, si - 1),
                    jnp.where(si == 2 * t - 1, t + BF16_ROWS, si - t + 1))
    shift_m = (sj == src).astype(BF16)
    conv_w = jnp.concatenate([cwx_ref[...], cwb_ref[...], cwc_ref[...]], axis=1)
    conv_b = jnp.concatenate([cbx_ref[...], cbb_ref[...], cbc_ref[...]], axis=1)

    def phase_a(cs, x_ref, b_ref, c_ref, dt_ref, dtt_ref, n_seg_chunks, chunk0):
        seg_rows = n_seg_chunks * t
        n8 = 2 * SSD_HEADS_PER_GROUP
        st = [dict(c=c, r0=pl.multiple_of(c * t, t), o0=pl.multiple_of((c + chunk0) * t, t)) for c in cs]

        for s in st:
            c, r0 = s["c"], s["r0"]
            lo = pl.multiple_of(jnp.maximum(r0 - BF16_ROWS, 0), BF16_ROWS)
            hi = pl.multiple_of(jnp.minimum(r0 + t, seg_rows - BF16_ROWS), BF16_ROWS)
            rows = lambda a, n: jnp.concatenate(
                [x_ref[0, pl.ds(a, n), :], b_ref[0, pl.ds(a, n), :], c_ref[0, pl.ds(a, n), :]], axis=1)
            cur = rows(r0, t)
            zero = jnp.zeros((BF16_ROWS, cur.shape[1]), BF16)
            before = jnp.where(c > 0, rows(lo, BF16_ROWS), zero)
            after = jnp.where(c < n_seg_chunks - 1, rows(hi, BF16_ROWS), zero)
            s["cur"] = cur
            s["sh"] = jnp.dot(shift_m, jnp.concatenate([cur, before, after], axis=0),
                              preferred_element_type=F32)
            dt = _softplus(dt_ref[0, pl.ds(r0, t), :] + biasr_ref[...])
            s["dtx"] = _dot2_lhs(dt, expand)
            at = _softplus(dtt_ref[:, pl.ds(r0, t)] + biasc_ref[...]) * a_c
            hi3, mid3, lo3 = _split3(at)
            s["ct3"] = jnp.dot(jnp.concatenate([hi3, mid3, lo3], axis=0), tt, preferred_element_type=F32)

        for s in st:
            ax = s["dtx"] * a_x
            s["csf"] = _dot2_rhs(tril, ax[:, :gx])
            s["csb"] = _dot2_rhs(triu, ax[:, gx:])
            sh = s["sh"]
            xbc = _silu(sh[:t] * conv_w[0:1] + s["cur"].astype(F32) * conv_w[1:2]
                        + sh[t:] * conv_w[2:3] + conv_b)
            s["xs"] = xbc[:, :gx]
            s["bm"] = xbc[:, gx:gx + SSD_STATE].astype(BF16)
            s["cm"] = xbc[:, gx + SSD_STATE:].astype(BF16)
            cm_s[pl.ds(s["o0"], t), :] = s["cm"]
            s["cb"] = _bdot_nt(s["cm"], s["bm"])

        for s in st:
            ct3 = s["ct3"]
            cst = ct3[0:n8] + ct3[n8:2 * n8] + ct3[2 * n8:3 * n8]
            xs, dtx, cb = s["xs"], s["dtx"], s["cb"]
            y = dskip_ref[0] * xs
            for d, cs_d in ((0, s["csf"]), (1, s["csb"])):
                mask = lower if d == 0 else (ri <= ci)
                xd = (xs * dtx[:, d * gx:(d + 1) * gx]).astype(BF16)
                ms, blocks = [], []
                for r in range(SSD_HEADS_PER_GROUP):
                    col = cs_d[:, r * SSD_HEAD_DIM:r * SSD_HEAD_DIM + 1]
                    k = d * SSD_HEADS_PER_GROUP + r
                    row = cst[k:k + 1, d * t:(d + 1) * t]
                    seg = jnp.exp2(jnp.where(mask, col - row, -jnp.inf))
                    ms.append((cb * seg).astype(BF16))
                    blocks.append(jnp.where(head_of_lane == r, xd, jnp.zeros_like(xd)))
                y = y + jnp.dot(jnp.concatenate(ms, axis=1), jnp.concatenate(blocks, axis=0),
                                preferred_element_type=F32)
            y_s[pl.ds(s["o0"], t), :] = y

        for s in st:
            xs, dtx, csf, csb = s["xs"], s["dtx"], s["csf"], s["csb"]
            endf = csf[t - 1:t, :]
            endb = csb[0:1, :]
            ex_s[pl.ds(s["o0"], t), :] = jnp.exp2(jnp.concatenate([csf, csb], axis=1))
            xdd = jnp.concatenate([xs * dtx[:, :gx] * jnp.exp2(endf - csf),
                                   xs * dtx[:, gx:] * jnp.exp2(endb - csb)], axis=1).astype(BF16)
            bt = jnp.transpose(s["bm"].astype(F32)).astype(BF16)
            st_s[s["c"] + chunk0] = jnp.dot(bt, xdd, preferred_element_type=F32)
            dec_s[pl.ds(s["c"] + chunk0, 1), :] = jnp.exp2(jnp.concatenate([endf, endb], axis=1))

    def run_phase_a(n_seg_chunks, chunk0, refs):
        group = math.gcd(n_seg_chunks, SSD_GROUP_CHUNKS)

        def body(i, carry):
            phase_a([i * group + k for k in range(group)], *refs, n_seg_chunks, chunk0)
            return carry

        lax.fori_loop(0, n_seg_chunks // group, body, 0)

    run_phase_a(nc_ctx, 0, (xc_ref, bc_ref, cc_ref, dtc_ref, dttc_ref))
    run_phase_a(nc_lat, nc_ctx, (xl_ref, bl_ref, cl_ref, dtl_ref, dttl_ref))

    order_f = list(range(n_chunks))
    order_b = list(range(nc_ctx - 1, -1, -1)) + list(range(n_chunks - 1, nc_ctx - 1, -1))
    for d, order in ((0, order_f), (1, order_b)):
        h = jnp.zeros((SSD_STATE, gx), F32)
        for c in order:
            hin_s[c, :, d * gx:(d + 1) * gx] = h.astype(BF16)
            h = dec_s[c:c + 1, d * gx:(d + 1) * gx] * h + st_s[c, :, d * gx:(d + 1) * gx]

    def phase_c(cs, out_ref, chunk0):
        offs = [pl.multiple_of((c + chunk0) * t, t) for c in cs]
        yos = [jnp.dot(cm_s[pl.ds(o0, t), :], hin_s[c + chunk0], preferred_element_type=F32)
               for c, o0 in zip(cs, offs)]
        for c, o0, yo in zip(cs, offs, yos):
            yo = yo * ex_s[pl.ds(o0, t), :]
            y = y_s[pl.ds(o0, t), :] + yo[:, :gx] + yo[:, gx:]
            out_ref[0, pl.ds(pl.multiple_of(c * t, t), t), :] = y.astype(BF16)

    def run_phase_c(n_seg_chunks, chunk0, out_ref):
        group = math.gcd(n_seg_chunks, SSD_GROUP_CHUNKS)

        def body(i, carry):
            phase_c([i * group + k for k in range(group)], out_ref, chunk0)
            return carry

        lax.fori_loop(0, n_seg_chunks // group, body, 0)

    run_phase_c(nc_ctx, 0, yc_ref)
    run_phase_c(nc_lat, nc_ctx, yl_ref)


def _ssd(ul3, uc3, cols_l, cols_c, dt_l, dt_c, dtt_l, dtt_c, p):
    bsz, n_lat, _ = ul3.shape
    n_ctx = uc3.shape[1]
    gx = GROUP_X
    n_chunks = (n_lat + n_ctx) // SSD_CHUNK
    n_tot = n_lat + n_ctx

    def seq_specs(cols, n):
        xb, bb, cb = cols["xs"] // gx, cols["b"] // SSD_STATE, cols["c"] // SSD_STATE
        return [
            pl.BlockSpec((1, n, gx), lambda b, g: (b, 0, xb + g)),
            pl.BlockSpec((1, n, SSD_STATE), lambda b, g: (b, 0, bb + g)),
            pl.BlockSpec((1, n, SSD_STATE), lambda b, g: (b, 0, cb + g)),
        ]

    xoff = 0
    boff = SSD_WIDTH // SSD_STATE
    coff = (SSD_WIDTH + SSD_BC_WIDTH) // SSD_STATE
    n8 = 2 * SSD_HEADS_PER_GROUP
    in_specs = (
        seq_specs(cols_l, n_lat) + seq_specs(cols_c, n_ctx) + [
            pl.BlockSpec((1, n_lat, LANES), lambda b, g: (b, 0, 0)),
            pl.BlockSpec((1, n_ctx, LANES), lambda b, g: (b, 0, 0)),
            pl.BlockSpec((n8, n_lat), lambda b, g: (g, b)),
            pl.BlockSpec((n8, n_ctx), lambda b, g: (g, b)),
            pl.BlockSpec((3, gx), lambda b, g: (0, xoff + g)),
            pl.BlockSpec((3, SSD_STATE), lambda b, g: (0, boff + g)),
            pl.BlockSpec((3, SSD_STATE), lambda b, g: (0, coff + g)),
            pl.BlockSpec((1, gx), lambda b, g: (0, xoff + g)),
            pl.BlockSpec((1, SSD_STATE), lambda b, g: (0, boff + g)),
            pl.BlockSpec((1, SSD_STATE), lambda b, g: (0, coff + g)),
            pl.BlockSpec((1, LANES), lambda b, g: (0, 0)),
            pl.BlockSpec((n8, 1), lambda b, g: (g, 0)),
            pl.BlockSpec((1, 1, 2 * gx), lambda b, g: (g, 0, 0)),
            pl.BlockSpec((n8, 1), lambda b, g: (g, 0)),
            pl.BlockSpec((1, 1, gx), lambda b, g: (g, 0, 0)),
        ])
    return pl.pallas_call(
        functools.partial(_ssd_kernel, n_lat=n_lat, n_ctx=n_ctx),
        grid=(bsz, SSD_GROUPS),
        in_specs=in_specs,
        out_specs=[
            pl.BlockSpec((1, n_lat, gx), lambda b, g: (b, 0, g)),
            pl.BlockSpec((1, n_ctx, gx), lambda b, g: (b, 0, g)),
        ],
        out_shape=[
            jax.ShapeDtypeStruct((bsz, n_lat, SSD_WIDTH), BF16),
            jax.ShapeDtypeStruct((bsz, n_ctx, SSD_WIDTH), BF16),
        ],
        scratch_shapes=[
            pltpu.VMEM((n_tot, SSD_STATE), BF16),
            pltpu.VMEM((n_tot, gx), F32),
            pltpu.VMEM((n_tot, 2 * gx), F32),
            pltpu.VMEM((n_chunks, SSD_STATE, 2 * gx), F32),
            pltpu.VMEM((n_chunks + (-n_chunks) % 8, 2 * gx), F32),
            pltpu.VMEM((n_chunks, SSD_STATE, 2 * gx), BF16),
        ],
        compiler_params=_cparams(("parallel", "parallel")),
        name="ssd_scan",
    )(ul3, ul3, ul3, uc3, uc3, uc3,
      dt_l.reshape(bsz, n_lat, LANES), dt_c.reshape(bsz, n_ctx, LANES), dtt_l, dtt_c,
      p["conv_w"], p["conv_w"], p["conv_w"], p["conv_b"], p["conv_b"], p["conv_b"],
      p["bias_row"], p["bias_col"], p["alog_x"], p["alog_col"], p["dskip_x"])


def _attn_kernel(*refs, n_ctx, n_lat, tq, lam_init, heads):
    if n_lat:
        (q_ref, kc_ref, vc_ref, kl_ref, vl_ref, cos_ref, sa_ref, sb_ref, lam_ref, sw_ref,
         o_ref, kl_s) = refs
    else:
        q_ref, kc_ref, vc_ref, lam_ref, sw_ref, o_ref = refs
    qi = pl.program_id(2)

    if n_lat:
        @pl.when(qi == 0)
        def _():
            kl_s[...] = _rope(kl_ref[0].astype(F32), cos_ref[...], sa_ref[...], sb_ref[...]).astype(BF16)

    lf = lam_ref[...]
    lam = (jnp.exp(jnp.sum(lf[0:1] * lf[1:2], axis=-1, keepdims=True))
           - jnp.exp(jnp.sum(lf[2:3] * lf[3:4], axis=-1, keepdims=True)) + lam_init)

    for hh in range(heads):
        cols = slice(hh * ATT_V_DIM, (hh + 1) * ATT_V_DIM)
        _attend_head(q_ref.at[0, :, cols], kc_ref.at[0, :, cols], vc_ref.at[0, :, cols],
                     kl_s if n_lat else None, vl_ref.at[0] if n_lat else None,
                     (cos_ref, sa_ref, sb_ref) if n_lat else None, lam, sw_ref, o_ref.at[0, :, cols],
                     qi=qi, n_ctx=n_ctx, n_lat=n_lat, tq=tq, lam_init=lam_init)


def _attend_head(q_ref, kc_ref, vc_ref, kl_s, vl_ref, tabs, lam, sw_ref, o_ref, *,
                 qi, n_ctx, n_lat, tq, lam_init):
    hd = ATT_HEAD_DIM
    q = q_ref[...].astype(F32)
    if n_lat:
        cos_ref, sa_ref, sb_ref = tabs
    if n_lat:
        r0 = pl.multiple_of(qi * tq, tq)
        q = _rope(q, cos_ref[pl.ds(r0, tq), :], sa_ref[pl.ds(r0, tq), :], sb_ref[pl.ds(r0, tq), :])
    q = (q * (ATT_SCALE * LOG2_E)).astype(BF16)
    lane = lax.broadcasted_iota(jnp.int32, (1, ATT_V_DIM), 1)
    q_comp = [jnp.where(lane < hd, q, jnp.zeros_like(q)), jnp.where(lane >= hd, q, jnp.zeros_like(q))]

    sub = min(ATT_SUB_ROWS, tq)
    chains = [(r, c) for r in range(tq // sub) for c in range(2)]
    nt = (((1,), (1,)), ((), ()))

    def scores(r, c):
        qc = q_comp[c][r * sub:(r + 1) * sub, :]
        s = lax.dot_general(qc, kc_ref[...], nt, preferred_element_type=F32)
        if n_lat:
            s = jnp.concatenate([s, lax.dot_general(qc, kl_s[...], nt, preferred_element_type=F32)],
                                axis=1)
        return s

    def weighted_values(e):
        pv = jnp.dot(e[:, :n_ctx], vc_ref[...], preferred_element_type=F32)
        if n_lat:
            pv = pv + jnp.dot(e[:, n_ctx:], vl_ref[...], preferred_element_type=F32)
        return pv

    ahead = ATT_CHAINS_AHEAD
    pending = [scores(*ch) for ch in chains[:ahead]]
    outs = {}
    for n, (r, c) in enumerate(chains):
        s = pending.pop(0)
        if n + ahead < len(chains):
            pending.append(scores(*chains[n + ahead]))
        e = jnp.exp2(s - jnp.max(s, axis=-1, keepdims=True))
        inv = 1.0 / jnp.sum(e, axis=-1, keepdims=True)
        outs[c] = weighted_values(e.astype(BF16)) * inv
        if c == 1:
            o = outs[0] - lam * outs[1]
            on = o * lax.rsqrt(jnp.mean(o * o, axis=-1, keepdims=True) + EPS) * sw_ref[...]
            o_ref[r * sub:(r + 1) * sub, :] = (on * (1.0 - lam_init)).astype(BF16)


def _attention(uq3, col_q, uc3, cols_c, ul3, cols_l, rope_tabs, lam_p, subln_w, *, lam_init, tq):
    bsz, n_q, _ = uq3.shape
    n_ctx = uc3.shape[1]
    n_lat = 0 if ul3 is None else ul3.shape[1]
    vd = ATT_V_DIM
    heads = 1 if n_lat else ATT_HEADS
    hw = heads * vd
    in_specs = [
        pl.BlockSpec((1, tq, hw), lambda b, h, i: (b, i, col_q // hw + h)),
        pl.BlockSpec((1, n_ctx, hw), lambda b, h, i: (b, 0, cols_c["k"] // hw + h)),
        pl.BlockSpec((1, n_ctx, hw), lambda b, h, i: (b, 0, cols_c["v"] // hw + h)),
    ]
    args = [uq3, uc3, uc3]
    if n_lat:
        in_specs += [
            pl.BlockSpec((1, n_lat, vd), lambda b, h, i: (b, 0, cols_l["k"] // vd + h)),
            pl.BlockSpec((1, n_lat, vd), lambda b, h, i: (b, 0, cols_l["v"] // vd + h)),
            pl.BlockSpec((n_lat, vd), lambda b, h, i: (0, 0)),
            pl.BlockSpec((n_lat, vd), lambda b, h, i: (0, 0)),
            pl.BlockSpec((n_lat, vd), lambda b, h, i: (0, 0)),
        ]
        args += [ul3, ul3, *rope_tabs]
    in_specs += [
        pl.BlockSpec((4, ATT_HEAD_DIM), lambda b, h, i: (0, 0)),
        pl.BlockSpec((1, vd), lambda b, h, i: (0, 0)),
    ]
    args += [lam_p, subln_w]
    return pl.pallas_call(
        functools.partial(_attn_kernel, n_ctx=n_ctx, n_lat=n_lat, tq=tq, lam_init=lam_init,
                          heads=heads),
        grid=(bsz, ATT_HEADS // heads, n_q // tq),
        in_specs=in_specs,
        out_specs=pl.BlockSpec((1, tq, hw), lambda b, h, i: (b, i, h)),
        out_shape=jax.ShapeDtypeStruct((bsz, n_q, ATT_WIDTH), BF16),
        scratch_shapes=[pltpu.VMEM((n_lat, vd), BF16)] if n_lat else [],
        compiler_params=_cparams(("parallel", "parallel", "arbitrary")),
        name="diff_attn",
    )(*args)


def _merge_kernel(ys_ref, z_ref, ya_ref, gs_ref, ga_ref, x_ref, g1_ref, nw_ref,
                  wbs_ref, wba_ref, wo_ref, o_ref):
    ta = _sigmoid(ga_ref[...].astype(F32)) * jnp.dot(ya_ref[...], wba_ref[...], preferred_element_type=F32)
    yz = ys_ref[...].astype(F32) * _silu(z_ref[...].astype(F32))
    ysn = yz * lax.rsqrt(jnp.mean(yz * yz, axis=-1, keepdims=True) + EPS) * nw_ref[...]
    ts = jnp.dot(ysn.astype(BF16), wbs_ref[...], preferred_element_type=F32)
    tmix = _sigmoid(gs_ref[...].astype(F32)) * ts + ta
    o_ref[...] = x_ref[...] + g1_ref[0] * jnp.dot(tmix.astype(BF16), wo_ref[...],
                                                   preferred_element_type=F32)


def _merge(ys2, u2, cols, ya2, x2d, mod3, mod_row, norm_w, w_bs, w_ba, w_o, *, tm):
    rows = x2d.shape[0]
    d = D_MODEL
    row_blk = lambda cb: pl.BlockSpec((tm, d), lambda i: (i, cb))
    full = lambda shape: pl.BlockSpec(shape, lambda i: (0, 0))
    return pl.pallas_call(
        _merge_kernel,
        grid=(rows // tm,),
        in_specs=[
            row_blk(0), row_blk(cols["z"] // d), row_blk(0),
            row_blk(cols["gs"] // d), row_blk(cols["ga"] // d), row_blk(0),
            pl.BlockSpec((1, 1, d), lambda i: (mod_row(i), 0, 2)),
            full((1, d)), full((d, d)), full((d, d)), full((d, d)),
        ],
        out_specs=row_blk(0),
        out_shape=jax.ShapeDtypeStruct((rows, d), F32),
        compiler_params=_cparams(("parallel",)),
        name="branch_merge",
    )(ys2, u2, ya2, u2, u2, x2d, mod3, norm_w, w_bs, w_ba, w_o)


def _ffn_kernel(x_ref, xp_ref, xn_ref, sh_ref, sc_ref, g2_ref, nw_ref, wu_ref, cw_ref, cb_ref,
                wd_ref, fw_ref, o_ref, h_s, u_s, act_s, acc_s, *, tm, tf, tiles_per_seg, final_norm):
    i = pl.program_id(0)
    halo = BF16_ROWS
    seg_first = (i % tiles_per_seg) == 0
    seg_last = (i % tiles_per_seg) == tiles_per_seg - 1

    def norm_mod(x):
        r = lax.rsqrt(jnp.mean(x * x, axis=-1, keepdims=True) + EPS)
        return ((x * r * nw_ref[...]) * (1.0 + sc_ref[0]) + sh_ref[0]).astype(BF16)

    h_s[0:halo, :] = norm_mod(xp_ref[...])
    h_s[halo:halo + tm, :] = norm_mod(x_ref[...])
    h_s[halo + tm:2 * halo + tm, :] = norm_mod(xn_ref[...])

    def up_proj(j):
        us = u_s.at[j % 2]
        for half in range(2):
            c0 = half * D_FF + j * tf
            us[:, half * tf:(half + 1) * tf] = jnp.dot(h_s[...], wu_ref[:, c0:c0 + tf],
                                                       preferred_element_type=F32)
        us[halo - 1:halo, :] = jnp.where(seg_first, 0.0, us[halo - 1:halo, :])
        us[halo + tm:halo + tm + 1, :] = jnp.where(seg_last, 0.0, us[halo + tm:halo + tm + 1, :])

    n_f = D_FF // tf
    bounds = [0]
    while bounds[-1] < n_f:
        left = n_f - bounds[-1]
        bounds.append(bounds[-1] + (FFN_DOWN_GROUP if left > FFN_DOWN_GROUP + 2 else (left + 1) // 2))
    group_of = {j: g for g in range(len(bounds) - 1) for j in range(bounds[g], bounds[g + 1])}

    def down_proj(g):
        j0, j1 = bounds[g], bounds[g + 1]
        down = jnp.dot(act_s[g % 2, :, 0:(j1 - j0) * tf], wd_ref[j0 * tf:j1 * tf, :],
                       preferred_element_type=F32)
        if g == 0:
            acc_s[...] = down
        else:
            acc_s[...] += down

    up_proj(0)
    for j in range(n_f):
        if j + 1 < n_f:
            up_proj(j + 1)
        if j in bounds[1:]:
            down_proj(group_of[j] - 1)
        us = u_s.at[j % 2]
        w = jnp.concatenate([cw_ref[:, j * tf:(j + 1) * tf],
                             cw_ref[:, D_FF + j * tf:D_FF + (j + 1) * tf]], axis=1)
        b = jnp.concatenate([cb_ref[:, j * tf:(j + 1) * tf],
                             cb_ref[:, D_FF + j * tf:D_FF + (j + 1) * tf]], axis=1)
        uc = (us[halo - 1:halo - 1 + tm, :] * w[0:1] + us[halo:halo + tm, :] * w[1:2]
              + us[halo + 1:halo + 1 + tm, :] * w[2:3] + b)
        g, k = group_of[j], j - bounds[group_of[j]]
        act_s[g % 2, :, k * tf:(k + 1) * tf] = (_silu(uc[:, :tf]) * uc[:, tf:]).astype(BF16)
    down_proj(len(bounds) - 2)
    y = x_ref[...] + g2_ref[0] * acc_s[...]
    if final_norm:
        y = y * lax.rsqrt(jnp.mean(y * y, axis=-1, keepdims=True) + EPS) * fw_ref[...]
    o_ref[...] = y


def _conv_ffn(x2d, mod3, mod_row, norm_w, w_up, conv_w, conv_b, w_down, final_w, *,
              tm, tf, seg_len, final_norm):
    rows = x2d.shape[0]
    d = D_MODEL
    halo = BF16_ROWS
    hb = tm // halo
    last_blk = rows // halo - 1
    assert seg_len % tm == 0
    resident = lambda shape: pl.BlockSpec(shape, lambda i: (0, 0), pipeline_mode=pl.Buffered(1))
    return pl.pallas_call(
        functools.partial(_ffn_kernel, tm=tm, tf=tf, tiles_per_seg=seg_len // tm, final_norm=final_norm),
        grid=(rows // tm,),
        in_specs=[
            pl.BlockSpec((tm, d), lambda i: (i, 0)),
            pl.BlockSpec((halo, d), lambda i: (jnp.maximum(i * hb - 1, 0), 0)),
            pl.BlockSpec((halo, d), lambda i: (jnp.minimum((i + 1) * hb, last_blk), 0)),
            pl.BlockSpec((1, 1, d), lambda i: (mod_row(i), 0, 3)),
            pl.BlockSpec((1, 1, d), lambda i: (mod_row(i), 0, 4)),
            pl.BlockSpec((1, 1, d), lambda i: (mod_row(i), 0, 5)),
            resident((1, d)),
            resident((d, 2 * D_FF)),
            resident((3, 2 * D_FF)),
            resident((1, 2 * D_FF)),
            resident((D_FF, d)),
            resident((1, d)),
        ],
        out_specs=pl.BlockSpec((tm, d), lambda i: (i, 0)),
        out_shape=jax.ShapeDtypeStruct((rows, d), F32),
        scratch_shapes=[
            pltpu.VMEM((tm + 2 * halo, d), BF16),
            pltpu.VMEM((2, tm + 2 * halo, 2 * tf), F32),
            pltpu.VMEM((2, tm, FFN_DOWN_GROUP * tf), BF16),
            pltpu.VMEM((tm, d), F32),
        ],
        compiler_params=_cparams(("parallel",)),
        name="conv_ffn",
    )(x2d, x2d, x2d, mod3, mod3, mod3, norm_w, w_up, conv_w, conv_b, w_down, final_w)


def _split_in_weight(w):
    sizes = (SSD_WIDTH, SSD_CONV_CH, N_DT, ATT_QK_WIDTH, ATT_QK_WIDTH, ATT_WIDTH, 2 * D_MODEL)
    starts = np.concatenate([[0], np.cumsum(sizes)])
    z, xbc, dt, q, k, v, gates = (w[:, int(starts[n]):int(starts[n + 1])] for n in range(7))
    main = jnp.concatenate([z, q, gates, xbc, k, v], axis=1).astype(BF16)
    return main, _dt_param_order(dt.reshape(-1, 2, SSD_HEADS))


def _dt_param_order(p2h):
    lead = p2h.shape[:-2]
    p = p2h.reshape(*lead, 2, SSD_GROUPS, SSD_HEADS_PER_GROUP)
    return jnp.swapaxes(p, -3, -2).reshape(*lead, N_DT)


def _rope_tables(n_tokens):
    rows = n_tokens // GRID_W
    inv_freq = ROPE_BASE ** (-jnp.arange(ROPE_PAIRS, dtype=F32) / ROPE_PAIRS)
    ang_r = jnp.broadcast_to(jnp.arange(rows, dtype=F32)[:, None, None] * inv_freq, (rows, GRID_W, ROPE_PAIRS))
    ang_c = jnp.broadcast_to(jnp.arange(GRID_W, dtype=F32)[None, :, None] * inv_freq, (rows, GRID_W, ROPE_PAIRS))
    ang = jnp.stack([ang_r, ang_c], axis=2).reshape(n_tokens, 2, 1, ROPE_PAIRS)
    cos = jnp.broadcast_to(jnp.cos(ang), (n_tokens, 2, 2, ROPE_PAIRS))
    sin = jnp.broadcast_to(jnp.sin(ang), (n_tokens, 2, 2, ROPE_PAIRS))
    zero = jnp.zeros_like(sin[:, :, :1])
    sin_a = jnp.concatenate([-sin[:, :, :1], zero], axis=2)
    sin_b = jnp.concatenate([zero, sin[:, :, 1:]], axis=2)
    tile = lambda a: jnp.tile(a.reshape(n_tokens, ATT_HEAD_DIM), (1, 2))
    return tile(cos), tile(sin_a), tile(sin_b)


def kernel(x, c, ctx, c_ctx, w_mod, b_mod, norm1_w, w_in, ssd_conv_w, ssd_conv_b, ssd_a_log,
           ssd_dt_bias, ssd_d, ssd_norm_w, diff_lambda, att_subln_w, w_br_ssd, w_br_att, w_out,
           norm2_w, w_up, ffn_conv_w, ffn_conv_b, w_down, final_norm_w):
    bsz, n_lat, d = x.shape
    n_ctx = ctx.shape[1]
    depth = w_mod.shape[0]
    assert d == D_MODEL and bsz + 1 <= 16
    ctx_row = bsz

    tm_in = min(1024, n_lat)
    tm_in_c = min(1024, bsz * n_ctx)
    tn_in = 4096
    tm_merge = min(512, n_lat)
    tm_merge_c = min(512, bsz * n_ctx)
    tm_ffn = min(512, n_lat)
    tf = 256
    tq = min(2048, n_lat)

    cc = jnp.zeros((16, d), F32).at[:bsz].set(c).at[ctx_row].set(c_ctx)
    mod = _modulation(cc, w_mod, b_mod)

    rope_tabs = _rope_tables(n_lat)
    cols_full = dict(z=COL_Z, q=COL_Q, gs=COL_GS, ga=COL_GA, xs=COL_XS, b=COL_B, c=COL_C, k=COL_K, v=COL_V)

    xl = x.reshape(bsz * n_lat, d)
    xc = ctx.reshape(bsz * n_ctx, d)
    lat_row = lambda tm: (lambda i: (i * tm) // n_lat)
    ctx_rowf = lambda i: ctx_row

    for li in range(depth):
        with_ctx = li < depth - 1
        mod3 = mod[li].reshape(16, 1, N_MOD * d)
        w_main, w_dt32 = _split_in_weight(w_in[li])
        w_dt = jnp.pad(w_dt32, ((0, 0), (0, LANES - N_DT))).astype(BF16)
        w_dtt = w_dt32.T.astype(BF16)
        n1 = norm1_w[li].reshape(1, d)

        ul, dt_l, dtt_l = _in_proj(xl, mod3, lat_row(tm_in), n1, w_main, w_dt, w_dtt,
                                   tm=tm_in, tn=tn_in, col0=0)
        col0_c = 0 if with_ctx else COL_XS
        uc, dt_c, dtt_c = _in_proj(xc, mod3, ctx_rowf, n1, w_main, w_dt, w_dtt,
                                   tm=tm_in_c, tn=tn_in, col0=col0_c)
        cols_c = {k_: v_ - col0_c for k_, v_ in cols_full.items()}
        ul3 = ul.reshape(bsz, n_lat, -1)
        uc3 = uc.reshape(bsz, n_ctx, -1)

        ssd_p = dict(
            conv_w=ssd_conv_w[li], conv_b=ssd_conv_b[li].reshape(1, SSD_CONV_CH),
            bias_row=jnp.zeros((1, LANES), F32).at[0, :N_DT].set(_dt_param_order(ssd_dt_bias[li])),
            bias_col=_dt_param_order(ssd_dt_bias[li]).reshape(N_DT, 1),
            alog_col=_dt_param_order(ssd_a_log[li]).reshape(N_DT, 1),
            alog_x=jnp.repeat(ssd_a_log[li].reshape(2, SSD_GROUPS, SSD_HEADS_PER_GROUP).transpose(1, 0, 2)
                              .reshape(SSD_GROUPS, 2 * SSD_HEADS_PER_GROUP), SSD_HEAD_DIM, axis=1)
            .reshape(SSD_GROUPS, 1, 2 * GROUP_X),
            dskip_x=jnp.repeat(ssd_d[li], SSD_HEAD_DIM).reshape(SSD_GROUPS, 1, GROUP_X),
        )
        ys_l, ys_c = _ssd(ul3, uc3, cols_full, cols_c, dt_l, dt_c, dtt_l, dtt_c, ssd_p)

        lam_init = 0.8 - 0.6 * math.exp(-0.3 * li)
        sw = att_subln_w[li].reshape(1, ATT_V_DIM)
        ya_l = _attention(ul3, COL_Q, uc3, cols_c, ul3, cols_full, rope_tabs, diff_lambda[li], sw,
                          lam_init=lam_init, tq=tq)

        w_bs = w_br_ssd[li].astype(BF16)
        w_ba = w_br_att[li].astype(BF16)
        w_o = w_out[li].astype(BF16)
        sn = ssd_norm_w[li].reshape(1, SSD_WIDTH)
        n2 = norm2_w[li].reshape(1, d)
        w_u = w_up[li].astype(BF16)
        cw = ffn_conv_w[li]
        cb = ffn_conv_b[li].reshape(1, 2 * D_FF)
        w_d = w_down[li].astype(BF16)
        fw = final_norm_w.reshape(1, d)

        if with_ctx:
            ya_c = _attention(uc3, cols_c["q"], uc3, cols_c, None, None, None, diff_lambda[li], sw,
                              lam_init=lam_init, tq=n_ctx)
            xc = _merge(ys_c.reshape(-1, SSD_WIDTH), uc, cols_c, ya_c.reshape(-1, ATT_WIDTH), xc, mod3,
                        ctx_rowf, sn, w_bs, w_ba, w_o, tm=tm_merge_c)
            xc = _conv_ffn(xc, mod3, ctx_rowf, n2, w_u, cw, cb, w_d, fw, tm=n_ctx, tf=tf,
                           seg_len=n_ctx, final_norm=False)

        xl = _merge(ys_l.reshape(-1, SSD_WIDTH), ul, cols_full, ya_l.reshape(-1, ATT_WIDTH), xl, mod3,
                    lat_row(tm_merge), sn, w_bs, w_ba, w_o, tm=tm_merge)
        xl = _conv_ffn(xl, mod3, lat_row(tm_ffn), n2, w_u, cw, cb, w_d, fw, tm=tm_ffn, tf=tf,
                       seg_len=n_lat, final_norm=not with_ctx)

    return xl.reshape(bsz, n_lat, d)
```

```python
import functools
import math

import jax
import jax.numpy as jnp
import numpy as np
from jax import lax
from jax.experimental import pallas as pl
from jax.experimental.pallas import tpu as pltpu

F32 = jnp.float32
BF16 = jnp.bfloat16

D_MODEL = 1024
EPS = 1e-6
N_MOD = 6
GRID_W = 64

SSD_HEADS = 16
SSD_HEAD_DIM = 64
SSD_WIDTH = SSD_HEADS * SSD_HEAD_DIM
SSD_GROUPS = 4
SSD_HEADS_PER_GROUP = SSD_HEADS // SSD_GROUPS
SSD_STATE = 128
SSD_CHUNK = 128
SSD_BC_WIDTH = SSD_GROUPS * SSD_STATE
SSD_CONV_CH = SSD_WIDTH + 2 * SSD_BC_WIDTH
GROUP_X = SSD_HEADS_PER_GROUP * SSD_HEAD_DIM
SSD_GROUP_CHUNKS = 8

ATT_HEADS = 8
ATT_HEAD_DIM = 64
ATT_V_DIM = 2 * ATT_HEAD_DIM
ATT_QK_WIDTH = ATT_HEADS * 2 * ATT_HEAD_DIM
ATT_WIDTH = ATT_HEADS * ATT_V_DIM
ATT_SCALE = ATT_HEAD_DIM ** -0.5
LOG2_E = math.log2(math.e)
ATT_SUB_ROWS = 512
ATT_CHAINS_AHEAD = 2
ROPE_BASE = 10000.0
ROPE_PAIRS = ATT_HEAD_DIM // 4

D_FF = 2816
FFN_DOWN_GROUP = 4
N_DT = 2 * SSD_HEADS

COL_Z = 0
COL_Q = COL_Z + SSD_WIDTH
COL_GS = COL_Q + ATT_QK_WIDTH
COL_GA = COL_GS + D_MODEL
COL_XS = COL_GA + D_MODEL
COL_B = COL_XS + SSD_WIDTH
COL_C = COL_B + SSD_BC_WIDTH
COL_K = COL_C + SSD_BC_WIDTH
COL_V = COL_K + ATT_QK_WIDTH
N_COLS = COL_V + ATT_WIDTH

LANES = 128
BF16_ROWS = 16
VMEM_LIMIT = 56 * 1024 * 1024


def _sigmoid(x):
    return 0.5 * jnp.tanh(0.5 * x) + 0.5


def _silu(x):
    return x * _sigmoid(x)


def _softplus(x):
    return jnp.maximum(x, 0.0) + jnp.log1p(jnp.exp(-jnp.abs(x)))


def _bdot_nt(a, b):
    return lax.dot_general(a.astype(BF16), b.astype(BF16), (((1,), (1,)), ((), ())),
                           preferred_element_type=F32)


def _split3(x):
    hi = x.astype(BF16)
    r1 = x - hi.astype(F32)
    mid = r1.astype(BF16)
    lo = (r1 - mid.astype(F32)).astype(BF16)
    return hi, mid, lo


def _dot2_rhs(a_bf16, x):
    hi = x.astype(BF16)
    lo = (x - hi.astype(F32)).astype(BF16)
    return (jnp.dot(a_bf16, hi, preferred_element_type=F32)
            + jnp.dot(a_bf16, lo, preferred_element_type=F32))


def _dot2_lhs(x, b_bf16):
    hi = x.astype(BF16)
    lo = (x - hi.astype(F32)).astype(BF16)
    return (jnp.dot(hi, b_bf16, preferred_element_type=F32)
            + jnp.dot(lo, b_bf16, preferred_element_type=F32))


def _cparams(sem):
    return pltpu.CompilerParams(dimension_semantics=sem, vmem_limit_bytes=VMEM_LIMIT)


def _mod_kernel(c_ref, w_ref, b_ref, o_ref):
    a = _silu(c_ref[...])
    o_ref[...] = _dot3_both(a, w_ref[...]) + b_ref[...]


def _dot3_both(a, w):
    ah, am, al = _split3(a)
    wh, wm, wl = _split3(w)
    d = lambda p, q: jnp.dot(p, q, preferred_element_type=F32)
    return (d(ah, wh) + (d(ah, wm) + d(am, wh))
            + (d(ah, wl) + d(am, wm) + d(al, wh)))


def _modulation(cc, w_mod, b_mod):
    depth = w_mod.shape[0]
    tn = 1024
    return pl.pallas_call(
        _mod_kernel,
        grid=(depth, N_MOD * D_MODEL // tn),
        in_specs=[
            pl.BlockSpec((16, D_MODEL), lambda l, j: (0, 0)),
            pl.BlockSpec((None, D_MODEL, tn), lambda l, j: (l, 0, j)),
            pl.BlockSpec((None, 1, tn), lambda l, j: (l, 0, j)),
        ],
        out_specs=pl.BlockSpec((None, 16, tn), lambda l, j: (l, 0, j)),
        out_shape=jax.ShapeDtypeStruct((depth, 16, N_MOD * D_MODEL), F32),
        compiler_params=_cparams(("parallel", "parallel")),
        name="modulation",
    )(cc, w_mod, b_mod.reshape(depth, 1, N_MOD * D_MODEL))


def _rope(x, cos, sin_a, sin_b):
    return x * cos + pltpu.roll(x, LANES - ROPE_PAIRS, axis=1) * sin_a + pltpu.roll(x, ROPE_PAIRS, axis=1) * sin_b


def _in_proj_kernel(x_ref, sh_ref, sc_ref, nw_ref, w_ref, wdt_ref, wdtt_ref,
                    u_ref, dt_ref, dtt_ref, h_ref):
    @pl.when(pl.program_id(1) == 0)
    def _():
        x = x_ref[...]
        r = lax.rsqrt(jnp.mean(x * x, axis=-1, keepdims=True) + EPS)
        h = (x * r * nw_ref[...]) * (1.0 + sc_ref[0]) + sh_ref[0]
        hb = h.astype(BF16)
        h_ref[...] = hb
        dt_ref[...] = jnp.dot(hb, wdt_ref[...], preferred_element_type=F32)
        dtt_ref[...] = lax.dot_general(wdtt_ref[...], hb, (((1,), (1,)), ((), ())),
                                       preferred_element_type=F32)

    u_ref[...] = jnp.dot(h_ref[...], w_ref[...], preferred_element_type=F32).astype(BF16)


def _in_proj(x2d, mod3, mod_row, norm_w, w_main, w_dt, w_dtt, *, layer, tm, tn, col0):
    rows = x2d.shape[0]
    n_out = w_main.shape[2] - col0
    cb0 = col0 // tn
    return pl.pallas_call(
        _in_proj_kernel,
        grid=(rows // tm, n_out // tn),
        in_specs=[
            pl.BlockSpec((tm, D_MODEL), lambda i, j: (i, 0)),
            pl.BlockSpec((1, 1, D_MODEL), lambda i, j: (mod_row(i), 0, 0)),
            pl.BlockSpec((1, 1, D_MODEL), lambda i, j: (mod_row(i), 0, 1)),
            pl.BlockSpec((1, D_MODEL), lambda i, j: (0, 0)),
            pl.BlockSpec((None, D_MODEL, tn), lambda i, j: (layer, 0, j + cb0)),
            pl.BlockSpec((None, D_MODEL, LANES), lambda i, j: (layer, 0, 0)),
            pl.BlockSpec((None, N_DT, D_MODEL), lambda i, j: (layer, 0, 0)),
        ],
        out_specs=[
            pl.BlockSpec((tm, tn), lambda i, j: (i, j)),
            pl.BlockSpec((tm, LANES), lambda i, j: (i, 0)),
            pl.BlockSpec((N_DT, tm), lambda i, j: (0, i)),
        ],
        out_shape=[
            jax.ShapeDtypeStruct((rows, n_out), BF16),
            jax.ShapeDtypeStruct((rows, LANES), F32),
            jax.ShapeDtypeStruct((N_DT, rows), F32),
        ],
        scratch_shapes=[pltpu.VMEM((tm, D_MODEL), BF16)],
        compiler_params=_cparams(("parallel", "arbitrary")),
        name="in_proj",
    )(x2d, mod3, mod3, norm_w, w_main, w_dt, w_dtt)


def _ssd_kernel(xl_ref, bl_ref, cl_ref, xc_ref, bc_ref, cc_ref,
                dtl_ref, dtc_ref, dttl_ref, dttc_ref,
                cwx_ref, cwb_ref, cwc_ref, cbx_ref, cbb_ref, cbc_ref,
                biasr_ref, biasc_ref, alogx_ref, alogc_ref, dskip_ref,
                yl_ref, yc_ref,
                cm_s, y_s, ex_s, st_s, dec_s, hin_s, *, n_lat, n_ctx):
    t = SSD_CHUNK
    g = pl.program_id(1)
    nc_ctx = n_ctx // t
    nc_lat = n_lat // t
    n_chunks = nc_ctx + nc_lat
    gx = GROUP_X

    ri = lax.broadcasted_iota(jnp.int32, (t, t), 0)
    ci = lax.broadcasted_iota(jnp.int32, (t, t), 1)
    lower = ri >= ci
    tril = lower.astype(BF16)
    triu = (ri <= ci).astype(BF16)
    er = lax.broadcasted_iota(jnp.int32, (LANES, 2 * gx), 0)
    ec = lax.broadcasted_iota(jnp.int32, (LANES, 2 * gx), 1)
    expand = (er == g * (2 * SSD_HEADS_PER_GROUP) + ec // SSD_HEAD_DIM).astype(BF16)
    head_of_lane = lax.broadcasted_iota(jnp.int32, (t, gx), 1) // SSD_HEAD_DIM
    tt = jnp.concatenate([triu, tril], axis=1)
    a_x = -jnp.exp(alogx_ref[0]) * LOG2_E
    a_c = -jnp.exp(alogc_ref[...]) * LOG2_E

    si = lax.broadcasted_iota(jnp.int32, (2 * t, t + 2 * BF16_ROWS), 0)
    sj = lax.broadcasted_iota(jnp.int32, (2 * t, t + 2 * BF16_ROWS), 1)
    src = jnp.where(si < t, jnp.where(si == 0, t + BF16_ROWS - 1, si - 1),
                    jnp.where(si == 2 * t - 1, t + BF16_ROWS, si - t + 1))
    shift_m = (sj == src).astype(BF16)
    conv_w = jnp.concatenate([cwx_ref[...], cwb_ref[...], cwc_ref[...]], axis=1)
    conv_b = jnp.concatenate([cbx_ref[...], cbb_ref[...], cbc_ref[...]], axis=1)

    def phase_a(cs, x_ref, b_ref, c_ref, dt_ref, dtt_ref, n_seg_chunks, chunk0):
        seg_rows = n_seg_chunks * t
        n8 = 2 * SSD_HEADS_PER_GROUP
        st = [dict(c=c, r0=pl.multiple_of(c * t, t), o0=pl.multiple_of((c + chunk0) * t, t)) for c in cs]

        for s in st:
            c, r0 = s["c"], s["r0"]
            lo = pl.multiple_of(jnp.maximum(r0 - BF16_ROWS, 0), BF16_ROWS)
            hi = pl.multiple_of(jnp.minimum(r0 + t, seg_rows - BF16_ROWS), BF16_ROWS)
            rows = lambda a, n: jnp.concatenate(
                [x_ref[0, pl.ds(a, n), :], b_ref[0, pl.ds(a, n), :], c_ref[0, pl.ds(a, n), :]], axis=1)
            cur = rows(r0, t)
            zero = jnp.zeros((BF16_ROWS, cur.shape[1]), BF16)
            before = jnp.where(c > 0, rows(lo, BF16_ROWS), zero)
            after = jnp.where(c < n_seg_chunks - 1, rows(hi, BF16_ROWS), zero)
            s["cur"] = cur
            s["sh"] = jnp.dot(shift_m, jnp.concatenate([cur, before, after], axis=0),
                              preferred_element_type=F32)
            dt = _softplus(dt_ref[0, pl.ds(r0, t), :] + biasr_ref[...])
            s["dtx"] = _dot2_lhs(dt, expand)
            at = _softplus(dtt_ref[:, pl.ds(r0, t)] + biasc_ref[...]) * a_c
            hi3, mid3, lo3 = _split3(at)
            s["ct3"] = jnp.dot(jnp.concatenate([hi3, mid3, lo3], axis=0), tt, preferred_element_type=F32)

        for s in st:
            ax = s["dtx"] * a_x
            s["csf"] = _dot2_rhs(tril, ax[:, :gx])
            s["csb"] = _dot2_rhs(triu, ax[:, gx:])
            sh = s["sh"]
            xbc = _silu(sh[:t] * conv_w[0:1] + s["cur"].astype(F32) * conv_w[1:2]
                        + sh[t:] * conv_w[2:3] + conv_b)
            s["xs"] = xbc[:, :gx]
            s["bm"] = xbc[:, gx:gx + SSD_STATE].astype(BF16)
            s["cm"] = xbc[:, gx + SSD_STATE:].astype(BF16)
            cm_s[pl.ds(s["o0"], t), :] = s["cm"]
            s["cb"] = _bdot_nt(s["cm"], s["bm"])

        for s in st:
            ct3 = s["ct3"]
            cst = ct3[0:n8] + ct3[n8:2 * n8] + ct3[2 * n8:3 * n8]
            xs, dtx, cb = s["xs"], s["dtx"], s["cb"]
            y = dskip_ref[0] * xs
            for d, cs_d in ((0, s["csf"]), (1, s["csb"])):
                mask = lower if d == 0 else (ri <= ci)
                xd = (xs * dtx[:, d * gx:(d + 1) * gx]).astype(BF16)
                ms, blocks = [], []
                for r in range(SSD_HEADS_PER_GROUP):
                    col = cs_d[:, r * SSD_HEAD_DIM:r * SSD_HEAD_DIM + 1]
                    k = d * SSD_HEADS_PER_GROUP + r
                    row = cst[k:k + 1, d * t:(d + 1) * t]
                    seg = jnp.exp2(jnp.where(mask, col - row, -jnp.inf))
                    ms.append((cb * seg).astype(BF16))
                    blocks.append(jnp.where(head_of_lane == r, xd, jnp.zeros_like(xd)))
                y = y + jnp.dot(jnp.concatenate(ms, axis=1), jnp.concatenate(blocks, axis=0),
                                preferred_element_type=F32)
            y_s[pl.ds(s["o0"], t), :] = y

        for s in st:
            xs, dtx, csf, csb = s["xs"], s["dtx"], s["csf"], s["csb"]
            endf = csf[t - 1:t, :]
            endb = csb[0:1, :]
            ex_s[pl.ds(s["o0"], t), :] = jnp.exp2(jnp.concatenate([csf, csb], axis=1))
            xdd = jnp.concatenate([xs * dtx[:, :gx] * jnp.exp2(endf - csf),
                                   xs * dtx[:, gx:] * jnp.exp2(endb - csb)], axis=1).astype(BF16)
            bt = jnp.transpose(s["bm"].astype(F32)).astype(BF16)
            st_s[s["c"] + chunk0] = jnp.dot(bt, xdd, preferred_element_type=F32)
            dec_s[pl.ds(s["c"] + chunk0, 1), :] = jnp.exp2(jnp.concatenate([endf, endb], axis=1))

    def run_phase_a(n_seg_chunks, chunk0, refs):
        group = math.gcd(n_seg_chunks, SSD_GROUP_CHUNKS)

        def body(i, carry):
            phase_a([i * group + k for k in range(group)], *refs, n_seg_chunks, chunk0)
            return carry

        lax.fori_loop(0, n_seg_chunks // group, body, 0)

    run_phase_a(nc_ctx, 0, (xc_ref, bc_ref, cc_ref, dtc_ref, dttc_ref))
    run_phase_a(nc_lat, nc_ctx, (xl_ref, bl_ref, cl_ref, dtl_ref, dttl_ref))

    order_f = list(range(n_chunks))
    order_b = list(range(nc_ctx - 1, -1, -1)) + list(range(n_chunks - 1, nc_ctx - 1, -1))
    for d, order in ((0, order_f), (1, order_b)):
        h = jnp.zeros((SSD_STATE, gx), F32)
        for c in order:
            hin_s[c, :, d * gx:(d + 1) * gx] = h.astype(BF16)
            h = dec_s[c:c + 1, d * gx:(d + 1) * gx] * h + st_s[c, :, d * gx:(d + 1) * gx]

    def phase_c(cs, out_ref, chunk0):
        offs = [pl.multiple_of((c + chunk0) * t, t) for c in cs]
        yos = [jnp.dot(cm_s[pl.ds(o0, t), :], hin_s[c + chunk0], preferred_element_type=F32)
               for c, o0 in zip(cs, offs)]
        for c, o0, yo in zip(cs, offs, yos):
            yo = yo * ex_s[pl.ds(o0, t), :]
            y = y_s[pl.ds(o0, t), :] + yo[:, :gx] + yo[:, gx:]
            out_ref[0, pl.ds(pl.multiple_of(c * t, t), t), :] = y.astype(BF16)

    def run_phase_c(n_seg_chunks, chunk0, out_ref):
        group = math.gcd(n_seg_chunks, SSD_GROUP_CHUNKS)

        def body(i, carry):
            phase_c([i * group + k for k in range(group)], out_ref, chunk0)
            return carry

        lax.fori_loop(0, n_seg_chunks // group, body, 0)

    run_phase_c(nc_ctx, 0, yc_ref)
    run_phase_c(nc_lat, nc_ctx, yl_ref)


def _ssd(ul3, uc3, cols_l, cols_c, dt_l, dt_c, dtt_l, dtt_c, p):
    bsz, n_lat, _ = ul3.shape
    n_ctx = uc3.shape[1]
    gx = GROUP_X
    n_chunks = (n_lat + n_ctx) // SSD_CHUNK
    n_tot = n_lat + n_ctx

    def seq_specs(cols, n):
        xb, bb, cb = cols["xs"] // gx, cols["b"] // SSD_STATE, cols["c"] // SSD_STATE
        return [
            pl.BlockSpec((1, n, gx), lambda b, g: (b, 0, xb + g)),
            pl.BlockSpec((1, n, SSD_STATE), lambda b, g: (b, 0, bb + g)),
            pl.BlockSpec((1, n, SSD_STATE), lambda b, g: (b, 0, cb + g)),
        ]

    xoff = 0
    boff = SSD_WIDTH // SSD_STATE
    coff = (SSD_WIDTH + SSD_BC_WIDTH) // SSD_STATE
    n8 = 2 * SSD_HEADS_PER_GROUP
    in_specs = (
        seq_specs(cols_l, n_lat) + seq_specs(cols_c, n_ctx) + [
            pl.BlockSpec((1, n_lat, LANES), lambda b, g: (b, 0, 0)),
            pl.BlockSpec((1, n_ctx, LANES), lambda b, g: (b, 0, 0)),
            pl.BlockSpec((n8, n_lat), lambda b, g: (g, b)),
            pl.BlockSpec((n8, n_ctx), lambda b, g: (g, b)),
            pl.BlockSpec((3, gx), lambda b, g: (0, xoff + g)),
            pl.BlockSpec((3, SSD_STATE), lambda b, g: (0, boff + g)),
            pl.BlockSpec((3, SSD_STATE), lambda b, g: (0, coff + g)),
            pl.BlockSpec((1, gx), lambda b, g: (0, xoff + g)),
            pl.BlockSpec((1, SSD_STATE), lambda b, g: (0, boff + g)),
            pl.BlockSpec((1, SSD_STATE), lambda b, g: (0, coff + g)),
            pl.BlockSpec((1, LANES), lambda b, g: (0, 0)),
            pl.BlockSpec((n8, 1), lambda b, g: (g, 0)),
            pl.BlockSpec((1, 1, 2 * gx), lambda b, g: (g, 0, 0)),
            pl.BlockSpec((n8, 1), lambda b, g: (g, 0)),
            pl.BlockSpec((1, 1, gx), lambda b, g: (g, 0, 0)),
        ])
    return pl.pallas_call(
        functools.partial(_ssd_kernel, n_lat=n_lat, n_ctx=n_ctx),
        grid=(bsz, SSD_GROUPS),
        in_specs=in_specs,
        out_specs=[
            pl.BlockSpec((1, n_lat, gx), lambda b, g: (b, 0, g)),
            pl.BlockSpec((1, n_ctx, gx), lambda b, g: (b, 0, g)),
        ],
        out_shape=[
            jax.ShapeDtypeStruct((bsz, n_lat, SSD_WIDTH), BF16),
            jax.ShapeDtypeStruct((bsz, n_ctx, SSD_WIDTH), BF16),
        ],
        scratch_shapes=[
            pltpu.VMEM((n_tot, SSD_STATE), BF16),
            pltpu.VMEM((n_tot, gx), F32),
            pltpu.VMEM((n_tot, 2 * gx), F32),
            pltpu.VMEM((n_chunks, SSD_STATE, 2 * gx), F32),
            pltpu.VMEM((n_chunks + (-n_chunks) % 8, 2 * gx), F32),
            pltpu.VMEM((n_chunks, SSD_STATE, 2 * gx), BF16),
        ],
        compiler_params=_cparams(("parallel", "parallel")),
        name="ssd_scan",
    )(ul3, ul3, ul3, uc3, uc3, uc3,
      dt_l.reshape(bsz, n_lat, LANES), dt_c.reshape(bsz, n_ctx, LANES), dtt_l, dtt_c,
      p["conv_w"], p["conv_w"], p["conv_w"], p["conv_b"], p["conv_b"], p["conv_b"],
      p["bias_row"], p["bias_col"], p["alog_x"], p["alog_col"], p["dskip_x"])


def _attn_kernel(*refs, n_ctx, n_lat, tq, lam_init, heads):
    if n_lat:
        (q_ref, kc_ref, vc_ref, kl_ref, vl_ref, cos_ref, sa_ref, sb_ref, lam_ref, sw_ref,
         o_ref, kl_s) = refs
    else:
        q_ref, kc_ref, vc_ref, lam_ref, sw_ref, o_ref = refs
    qi = pl.program_id(2)

    if n_lat:
        @pl.when(qi == 0)
        def _():
            kl_s[...] = _rope(kl_ref[0].astype(F32), cos_ref[...], sa_ref[...], sb_ref[...]).astype(BF16)

    lf = lam_ref[...]
    lam = (jnp.exp(jnp.sum(lf[0:1] * lf[1:2], axis=-1, keepdims=True))
           - jnp.exp(jnp.sum(lf[2:3] * lf[3:4], axis=-1, keepdims=True)) + lam_init)

    for hh in range(heads):
        cols = slice(hh * ATT_V_DIM, (hh + 1) * ATT_V_DIM)
        _attend_head(q_ref.at[0, :, cols], kc_ref.at[0, :, cols], vc_ref.at[0, :, cols],
                     kl_s if n_lat else None, vl_ref.at[0] if n_lat else None,
                     (cos_ref, sa_ref, sb_ref) if n_lat else None, lam, sw_ref, o_ref.at[0, :, cols],
                     qi=qi, n_ctx=n_ctx, n_lat=n_lat, tq=tq, lam_init=lam_init)


def _attend_head(q_ref, kc_ref, vc_ref, kl_s, vl_ref, tabs, lam, sw_ref, o_ref, *,
                 qi, n_ctx, n_lat, tq, lam_init):
    hd = ATT_HEAD_DIM
    q = q_ref[...].astype(F32)
    if n_lat:
        cos_ref, sa_ref, sb_ref = tabs
    if n_lat:
        r0 = pl.multiple_of(qi * tq, tq)
        q = _rope(q, cos_ref[pl.ds(r0, tq), :], sa_ref[pl.ds(r0, tq), :], sb_ref[pl.ds(r0, tq), :])
    q = (q * (ATT_SCALE * LOG2_E)).astype(BF16)
    lane = lax.broadcasted_iota(jnp.int32, (1, ATT_V_DIM), 1)
    q_comp = [jnp.where(lane < hd, q, jnp.zeros_like(q)), jnp.where(lane >= hd, q, jnp.zeros_like(q))]

    sub = min(ATT_SUB_ROWS, tq)
    chains = [(r, c) for r in range(tq // sub) for c in range(2)]
    nt = (((1,), (1,)), ((), ()))

    def scores(r, c):
        qc = q_comp[c][r * sub:(r + 1) * sub, :]
        s = lax.dot_general(qc, kc_ref[...], nt, preferred_element_type=F32)
        if n_lat:
            s = jnp.concatenate([s, lax.dot_general(qc, kl_s[...], nt, preferred_element_type=F32)],
                                axis=1)
        return s

    def weighted_values(e):
        pv = jnp.dot(e[:, :n_ctx], vc_ref[...], preferred_element_type=F32)
        if n_lat:
            pv = pv + jnp.dot(e[:, n_ctx:], vl_ref[...], preferred_element_type=F32)
        return pv

    ahead = ATT_CHAINS_AHEAD
    pending = [scores(*ch) for ch in chains[:ahead]]
    outs = {}
    for n, (r, c) in enumerate(chains):
        s = pending.pop(0)
        if n + ahead < len(chains):
            pending.append(scores(*chains[n + ahead]))
        e = jnp.exp2(s - jnp.max(s, axis=-1, keepdims=True))
        inv = 1.0 / jnp.sum(e, axis=-1, keepdims=True)
        outs[c] = weighted_values(e.astype(BF16)) * inv
        if c == 1:
            o = outs[0] - lam * outs[1]
            on = o * lax.rsqrt(jnp.mean(o * o, axis=-1, keepdims=True) + EPS) * sw_ref[...]
            o_ref[r * sub:(r + 1) * sub, :] = (on * (1.0 - lam_init)).astype(BF16)


def _attention(uq3, col_q, uc3, cols_c, ul3, cols_l, rope_tabs, lam_p, subln_w, *, lam_init, tq):
    bsz, n_q, _ = uq3.shape
    n_ctx = uc3.shape[1]
    n_lat = 0 if ul3 is None else ul3.shape[1]
    vd = ATT_V_DIM
    heads = 1 if n_lat else ATT_HEADS
    hw = heads * vd
    in_specs = [
        pl.BlockSpec((1, tq, hw), lambda b, h, i: (b, i, col_q // hw + h)),
        pl.BlockSpec((1, n_ctx, hw), lambda b, h, i: (b, 0, cols_c["k"] // hw + h)),
        pl.BlockSpec((1, n_ctx, hw), lambda b, h, i: (b, 0, cols_c["v"] // hw + h)),
    ]
    args = [uq3, uc3, uc3]
    if n_lat:
        in_specs += [
            pl.BlockSpec((1, n_lat, vd), lambda b, h, i: (b, 0, cols_l["k"] // vd + h)),
            pl.BlockSpec((1, n_lat, vd), lambda b, h, i: (b, 0, cols_l["v"] // vd + h)),
            pl.BlockSpec((n_lat, vd), lambda b, h, i: (0, 0)),
            pl.BlockSpec((n_lat, vd), lambda b, h, i: (0, 0)),
            pl.BlockSpec((n_lat, vd), lambda b, h, i: (0, 0)),
        ]
        args += [ul3, ul3, *rope_tabs]
    in_specs += [
        pl.BlockSpec((4, ATT_HEAD_DIM), lambda b, h, i: (0, 0)),
        pl.BlockSpec((1, vd), lambda b, h, i: (0, 0)),
    ]
    args += [lam_p, subln_w]
    return pl.pallas_call(
        functools.partial(_attn_kernel, n_ctx=n_ctx, n_lat=n_lat, tq=tq, lam_init=lam_init,
                          heads=heads),
        grid=(bsz, ATT_HEADS // heads, n_q // tq),
        in_specs=in_specs,
        out_specs=pl.BlockSpec((1, tq, hw), lambda b, h, i: (b, i, h)),
        out_shape=jax.ShapeDtypeStruct((bsz, n_q, ATT_WIDTH), BF16),
        scratch_shapes=[pltpu.VMEM((n_lat, vd), BF16)] if n_lat else [],
        compiler_params=_cparams(("parallel", "parallel", "arbitrary")),
        name="diff_attn",
    )(*args)


def _merge_kernel(ys_ref, z_ref, ya_ref, gs_ref, ga_ref, x_ref, g1_ref, nw_ref,
                  wbs_ref, wba_ref, wo_ref, o_ref):
    ta = _sigmoid(ga_ref[...].astype(F32)) * jnp.dot(ya_ref[...], wba_ref[...], preferred_element_type=F32)
    yz = ys_ref[...].astype(F32) * _silu(z_ref[...].astype(F32))
    ysn = yz * lax.rsqrt(jnp.mean(yz * yz, axis=-1, keepdims=True) + EPS) * nw_ref[...]
    ts = jnp.dot(ysn.astype(BF16), wbs_ref[...], preferred_element_type=F32)
    tmix = _sigmoid(gs_ref[...].astype(F32)) * ts + ta
    o_ref[...] = x_ref[...] + g1_ref[0] * jnp.dot(tmix.astype(BF16), wo_ref[...],
                                                   preferred_element_type=F32)


def _merge(ys2, u2, cols, ya2, x2d, mod3, mod_row, norm_w, w_bs, w_ba, w_o, *, layer, tm):
    rows = x2d.shape[0]
    d = D_MODEL
    row_blk = lambda cb: pl.BlockSpec((tm, d), lambda i: (i, cb))
    full = lambda shape: pl.BlockSpec(shape, lambda i: (0, 0))
    weight = pl.BlockSpec((None, d, d), lambda i: (layer, 0, 0))
    return pl.pallas_call(
        _merge_kernel,
        grid=(rows // tm,),
        in_specs=[
            row_blk(0), row_blk(cols["z"] // d), row_blk(0),
            row_blk(cols["gs"] // d), row_blk(cols["ga"] // d), row_blk(0),
            pl.BlockSpec((1, 1, d), lambda i: (mod_row(i), 0, 2)),
            full((1, d)), weight, weight, weight,
        ],
        out_specs=row_blk(0),
        out_shape=jax.ShapeDtypeStruct((rows, d), F32),
        compiler_params=_cparams(("parallel",)),
        name="branch_merge",
    )(ys2, u2, ya2, u2, u2, x2d, mod3, norm_w, w_bs, w_ba, w_o)


def _ffn_kernel(x_ref, xp_ref, xn_ref, sh_ref, sc_ref, g2_ref, nw_ref, wu_ref, cw_ref, cb_ref,
                wd_ref, fw_ref, o_ref, h_s, u_s, act_s, acc_s, *, tm, tf, tiles_per_seg, final_norm):
    i = pl.program_id(0)
    halo = BF16_ROWS
    seg_first = (i % tiles_per_seg) == 0
    seg_last = (i % tiles_per_seg) == tiles_per_seg - 1

    def norm_mod(x):
        r = lax.rsqrt(jnp.mean(x * x, axis=-1, keepdims=True) + EPS)
        return ((x * r * nw_ref[...]) * (1.0 + sc_ref[0]) + sh_ref[0]).astype(BF16)

    h_s[0:halo, :] = norm_mod(xp_ref[...])
    h_s[halo:halo + tm, :] = norm_mod(x_ref[...])
    h_s[halo + tm:2 * halo + tm, :] = norm_mod(xn_ref[...])

    def up_proj(j):
        us = u_s.at[j % 2]
        for half in range(2):
            c0 = half * D_FF + j * tf
            us[:, half * tf:(half + 1) * tf] = jnp.dot(h_s[...], wu_ref[:, c0:c0 + tf],
                                                       preferred_element_type=F32)
        us[halo - 1:halo, :] = jnp.where(seg_first, 0.0, us[halo - 1:halo, :])
        us[halo + tm:halo + tm + 1, :] = jnp.where(seg_last, 0.0, us[halo + tm:halo + tm + 1, :])

    n_f = D_FF // tf
    bounds = [0]
    while bounds[-1] < n_f:
        left = n_f - bounds[-1]
        bounds.append(bounds[-1] + (FFN_DOWN_GROUP if left > FFN_DOWN_GROUP + 2 else (left + 1) // 2))
    group_of = {j: g for g in range(len(bounds) - 1) for j in range(bounds[g], bounds[g + 1])}

    def down_proj(g):
        j0, j1 = bounds[g], bounds[g + 1]
        down = jnp.dot(act_s[g % 2, :, 0:(j1 - j0) * tf], wd_ref[j0 * tf:j1 * tf, :],
                       preferred_element_type=F32)
        if g == 0:
            acc_s[...] = down
        else:
            acc_s[...] += down

    up_proj(0)
    for j in range(n_f):
        if j + 1 < n_f:
            up_proj(j + 1)
        if j in bounds[1:]:
            down_proj(group_of[j] - 1)
        us = u_s.at[j % 2]
        w = jnp.concatenate([cw_ref[:, j * tf:(j + 1) * tf],
                             cw_ref[:, D_FF + j * tf:D_FF + (j + 1) * tf]], axis=1)
        b = jnp.concatenate([cb_ref[:, j * tf:(j + 1) * tf],
                             cb_ref[:, D_FF + j * tf:D_FF + (j + 1) * tf]], axis=1)
        uc = (us[halo - 1:halo - 1 + tm, :] * w[0:1] + us[halo:halo + tm, :] * w[1:2]
              + us[halo + 1:halo + 1 + tm, :] * w[2:3] + b)
        g, k = group_of[j], j - bounds[group_of[j]]
        act_s[g % 2, :, k * tf:(k + 1) * tf] = (_silu(uc[:, :tf]) * uc[:, tf:]).astype(BF16)
    down_proj(len(bounds) - 2)
    y = x_ref[...] + g2_ref[0] * acc_s[...]
    if final_norm:
        y = y * lax.rsqrt(jnp.mean(y * y, axis=-1, keepdims=True) + EPS) * fw_ref[...]
    o_ref[...] = y


def _conv_ffn(x2d, mod3, mod_row, norm_w, w_up, conv_w, conv_b, w_down, final_w, *,
              layer, tm, tf, seg_len, final_norm):
    rows = x2d.shape[0]
    d = D_MODEL
    halo = BF16_ROWS
    hb = tm // halo
    last_blk = rows // halo - 1
    assert seg_len % tm == 0
    resident = lambda shape: pl.BlockSpec(shape, lambda i: (0, 0), pipeline_mode=pl.Buffered(1))
    layered = lambda shape: pl.BlockSpec((None, *shape), lambda i: (layer, 0, 0),
                                         pipeline_mode=pl.Buffered(1))
    return pl.pallas_call(
        functools.partial(_ffn_kernel, tm=tm, tf=tf, tiles_per_seg=seg_len // tm, final_norm=final_norm),
        grid=(rows // tm,),
        in_specs=[
            pl.BlockSpec((tm, d), lambda i: (i, 0)),
            pl.BlockSpec((halo, d), lambda i: (jnp.maximum(i * hb - 1, 0), 0)),
            pl.BlockSpec((halo, d), lambda i: (jnp.minimum((i + 1) * hb, last_blk), 0)),
            pl.BlockSpec((1, 1, d), lambda i: (mod_row(i), 0, 3)),
            pl.BlockSpec((1, 1, d), lambda i: (mod_row(i), 0, 4)),
            pl.BlockSpec((1, 1, d), lambda i: (mod_row(i), 0, 5)),
            resident((1, d)),
            layered((d, 2 * D_FF)),
            layered((3, 2 * D_FF)),
            layered((1, 2 * D_FF)),
            layered((D_FF, d)),
            resident((1, d)),
        ],
        out_specs=pl.BlockSpec((tm, d), lambda i: (i, 0)),
        out_shape=jax.ShapeDtypeStruct((rows, d), F32),
        scratch_shapes=[
            pltpu.VMEM((tm + 2 * halo, d), BF16),
            pltpu.VMEM((2, tm + 2 * halo, 2 * tf), F32),
            pltpu.VMEM((2, tm, FFN_DOWN_GROUP * tf), BF16),
            pltpu.VMEM((tm, d), F32),
        ],
        compiler_params=_cparams(("parallel",)),
        name="conv_ffn",
    )(x2d, x2d, x2d, mod3, mod3, mod3, norm_w, w_up, conv_w, conv_b, w_down, final_w)


def _split_in_weight(w):
    sizes = (SSD_WIDTH, SSD_CONV_CH, N_DT, ATT_QK_WIDTH, ATT_QK_WIDTH, ATT_WIDTH, 2 * D_MODEL)
    starts = np.concatenate([[0], np.cumsum(sizes)])
    z, xbc, dt, q, k, v, gates = (w[..., int(starts[n]):int(starts[n + 1])] for n in range(7))
    main = jnp.concatenate([z, q, gates, xbc, k, v], axis=-1).astype(BF16)
    return main, _dt_param_order(dt.reshape(*dt.shape[:-1], 2, SSD_HEADS))


def _dt_param_order(p2h):
    lead = p2h.shape[:-2]
    p = p2h.reshape(*lead, 2, SSD_GROUPS, SSD_HEADS_PER_GROUP)
    return jnp.swapaxes(p, -3, -2).reshape(*lead, N_DT)


def _rope_tables(n_tokens):
    rows = n_tokens // GRID_W
    inv_freq = ROPE_BASE ** (-jnp.arange(ROPE_PAIRS, dtype=F32) / ROPE_PAIRS)
    ang_r = jnp.broadcast_to(jnp.arange(rows, dtype=F32)[:, None, None] * inv_freq, (rows, GRID_W, ROPE_PAIRS))
    ang_c = jnp.broadcast_to(jnp.arange(GRID_W, dtype=F32)[None, :, None] * inv_freq, (rows, GRID_W, ROPE_PAIRS))
    ang = jnp.stack([ang_r, ang_c], axis=2).reshape(n_tokens, 2, 1, ROPE_PAIRS)
    cos = jnp.broadcast_to(jnp.cos(ang), (n_tokens, 2, 2, ROPE_PAIRS))
    sin = jnp.broadcast_to(jnp.sin(ang), (n_tokens, 2, 2, ROPE_PAIRS))
    zero = jnp.zeros_like(sin[:, :, :1])
    sin_a = jnp.concatenate([-sin[:, :, :1], zero], axis=2)
    sin_b = jnp.concatenate([zero, sin[:, :, 1:]], axis=2)
    tile = lambda a: jnp.tile(a.reshape(n_tokens, ATT_HEAD_DIM), (1, 2))
    return tile(cos), tile(sin_a), tile(sin_b)


def kernel(x, c, ctx, c_ctx, w_mod, b_mod, norm1_w, w_in, ssd_conv_w, ssd_conv_b, ssd_a_log,
           ssd_dt_bias, ssd_d, ssd_norm_w, diff_lambda, att_subln_w, w_br_ssd, w_br_att, w_out,
           norm2_w, w_up, ffn_conv_w, ffn_conv_b, w_down, final_norm_w):
    bsz, n_lat, d = x.shape
    n_ctx = ctx.shape[1]
    depth = w_mod.shape[0]
    assert d == D_MODEL and bsz + 1 <= 16
    ctx_row = bsz

    tm_in = min(1024, n_lat)
    tm_in_c = min(1024, bsz * n_ctx)
    tn_in = 4096
    tm_merge = min(512, n_lat)
    tm_merge_c = min(512, bsz * n_ctx)
    tm_ffn = min(512, n_lat)
    tf = 256
    tq = min(2048, n_lat)

    cc = jnp.zeros((16, d), F32).at[:bsz].set(c).at[ctx_row].set(c_ctx)
    mod = _modulation(cc, w_mod, b_mod)

    rope_tabs = _rope_tables(n_lat)
    cols_full = dict(z=COL_Z, q=COL_Q, gs=COL_GS, ga=COL_GA, xs=COL_XS, b=COL_B, c=COL_C, k=COL_K, v=COL_V)

    xl = x.reshape(bsz * n_lat, d)
    xc = ctx.reshape(bsz * n_ctx, d)
    lat_row = lambda tm: (lambda i: (i * tm) // n_lat)
    ctx_rowf = lambda i: ctx_row

    w_main, w_dt32 = _split_in_weight(w_in)
    w_dt = jnp.pad(w_dt32, ((0, 0), (0, 0), (0, LANES - N_DT))).astype(BF16)
    w_dtt = jnp.swapaxes(w_dt32, 1, 2).astype(BF16)
    w_bs = w_br_ssd.astype(BF16)
    w_ba = w_br_att.astype(BF16)
    w_o = w_out.astype(BF16)
    w_u = w_up.astype(BF16)
    w_d = w_down.astype(BF16)
    cb = ffn_conv_b.reshape(depth, 1, 2 * D_FF)

    for li in range(depth):
        with_ctx = li < depth - 1
        mod3 = mod[li].reshape(16, 1, N_MOD * d)
        n1 = norm1_w[li].reshape(1, d)

        ul, dt_l, dtt_l = _in_proj(xl, mod3, lat_row(tm_in), n1, w_main, w_dt, w_dtt,
                                   layer=li, tm=tm_in, tn=tn_in, col0=0)
        col0_c = 0 if with_ctx else COL_XS
        uc, dt_c, dtt_c = _in_proj(xc, mod3, ctx_rowf, n1, w_main, w_dt, w_dtt,
                                   layer=li, tm=tm_in_c, tn=tn_in, col0=col0_c)
        cols_c = {k_: v_ - col0_c for k_, v_ in cols_full.items()}
        ul3 = ul.reshape(bsz, n_lat, -1)
        uc3 = uc.reshape(bsz, n_ctx, -1)

        ssd_p = dict(
            conv_w=ssd_conv_w[li], conv_b=ssd_conv_b[li].reshape(1, SSD_CONV_CH),
            bias_row=jnp.zeros((1, LANES), F32).at[0, :N_DT].set(_dt_param_order(ssd_dt_bias[li])),
            bias_col=_dt_param_order(ssd_dt_bias[li]).reshape(N_DT, 1),
            alog_col=_dt_param_order(ssd_a_log[li]).reshape(N_DT, 1),
            alog_x=jnp.repeat(ssd_a_log[li].reshape(2, SSD_GROUPS, SSD_HEADS_PER_GROUP).transpose(1, 0, 2)
                              .reshape(SSD_GROUPS, 2 * SSD_HEADS_PER_GROUP), SSD_HEAD_DIM, axis=1)
            .reshape(SSD_GROUPS, 1, 2 * GROUP_X),
            dskip_x=jnp.repeat(ssd_d[li], SSD_HEAD_DIM).reshape(SSD_GROUPS, 1, GROUP_X),
        )
        ys_l, ys_c = _ssd(ul3, uc3, cols_full, cols_c, dt_l, dt_c, dtt_l, dtt_c, ssd_p)

        lam_init = 0.8 - 0.6 * math.exp(-0.3 * li)
        sw = att_subln_w[li].reshape(1, ATT_V_DIM)
        ya_l = _attention(ul3, COL_Q, uc3, cols_c, ul3, cols_full, rope_tabs, diff_lambda[li], sw,
                          lam_init=lam_init, tq=tq)

        sn = ssd_norm_w[li].reshape(1, SSD_WIDTH)
        n2 = norm2_w[li].reshape(1, d)
        fw = final_norm_w.reshape(1, d)

        if with_ctx:
            ya_c = _attention(uc3, cols_c["q"], uc3, cols_c, None, None, None, diff_lambda[li], sw,
                              lam_init=lam_init, tq=n_ctx)
            xc = _merge(ys_c.reshape(-1, SSD_WIDTH), uc, cols_c, ya_c.reshape(-1, ATT_WIDTH), xc, mod3,
                        ctx_rowf, sn, w_bs, w_ba, w_o, layer=li, tm=tm_merge_c)
            xc = _conv_ffn(xc, mod3, ctx_rowf, n2, w_u, ffn_conv_w, cb, w_d, fw, layer=li, tm=n_ctx,
                           tf=tf, seg_len=n_ctx, final_norm=False)

        xl = _merge(ys_l.reshape(-1, SSD_WIDTH), ul, cols_full, ya_l.reshape(-1, ATT_WIDTH), xl, mod3,
                    lat_row(tm_merge), sn, w_bs, w_ba, w_o, layer=li, tm=tm_merge)
        xl = _conv_ffn(xl, mod3, lat_row(tm_ffn), n2, w_u, ffn_conv_w, cb, w_d, fw, layer=li,
                       tm=tm_ffn, tf=tf, seg_len=n_lat, final_norm=not with_ctx)

    return xl.reshape(bsz, n_lat, d)
```

```python
import functools
import math

import jax
import jax.numpy as jnp
from jax import lax
from jax.experimental import pallas as pl
from jax.experimental.pallas import tpu as pltpu

F32 = jnp.float32
BF16 = jnp.bfloat16

D_MODEL = 1024
EPS = 1e-6
N_MOD = 6
GRID_W = 64

SSD_HEADS = 16
SSD_HEAD_DIM = 64
SSD_WIDTH = SSD_HEADS * SSD_HEAD_DIM
SSD_GROUPS = 4
SSD_HEADS_PER_GROUP = SSD_HEADS // SSD_GROUPS
SSD_STATE = 128
SSD_CHUNK = 128
SSD_BC_WIDTH = SSD_GROUPS * SSD_STATE
SSD_CONV_CH = SSD_WIDTH + 2 * SSD_BC_WIDTH
GROUP_X = SSD_HEADS_PER_GROUP * SSD_HEAD_DIM
SSD_GROUP_CHUNKS = 8

ATT_HEADS = 8
ATT_HEAD_DIM = 64
ATT_V_DIM = 2 * ATT_HEAD_DIM
ATT_QK_WIDTH = ATT_HEADS * 2 * ATT_HEAD_DIM
ATT_WIDTH = ATT_HEADS * ATT_V_DIM
ATT_SCALE = ATT_HEAD_DIM ** -0.5
LOG2_E = math.log2(math.e)
ATT_SUB_ROWS = 512
ATT_CHAINS_AHEAD = 2
ROPE_BASE = 10000.0
ROPE_PAIRS = ATT_HEAD_DIM // 4

D_FF = 2816
FFN_DOWN_GROUP = 4
N_DT = 2 * SSD_HEADS

COL_Z = 0
COL_XS = COL_Z + SSD_WIDTH
COL_B = COL_XS + SSD_WIDTH
COL_C = COL_B + SSD_BC_WIDTH
COL_Q = COL_C + SSD_BC_WIDTH
COL_K = COL_Q + ATT_QK_WIDTH
COL_V = COL_K + ATT_QK_WIDTH
COL_GS = COL_V + ATT_WIDTH
COL_GA = COL_GS + D_MODEL
N_COLS = COL_GA + D_MODEL
COL_CTX_LAST_END = COL_V + ATT_WIDTH

LANES = 128
BF16_ROWS = 16
VMEM_LIMIT = 56 * 1024 * 1024


def _sigmoid(x):
    return 0.5 * jnp.tanh(0.5 * x) + 0.5


def _silu(x):
    return x * _sigmoid(x)


def _softplus(x):
    return jnp.maximum(x, 0.0) + jnp.log1p(jnp.exp(-jnp.abs(x)))


def _bdot_nt(a, b):
    return lax.dot_general(a.astype(BF16), b.astype(BF16), (((1,), (1,)), ((), ())),
                           preferred_element_type=F32)


def _split3(x):
    hi = x.astype(BF16)
    r1 = x - hi.astype(F32)
    mid = r1.astype(BF16)
    lo = (r1 - mid.astype(F32)).astype(BF16)
    return hi, mid, lo


def _dot2_rhs(a_bf16, x):
    hi = x.astype(BF16)
    lo = (x - hi.astype(F32)).astype(BF16)
    return (jnp.dot(a_bf16, hi, preferred_element_type=F32)
            + jnp.dot(a_bf16, lo, preferred_element_type=F32))


def _dot2_lhs(x, b_bf16):
    hi = x.astype(BF16)
    lo = (x - hi.astype(F32)).astype(BF16)
    return (jnp.dot(hi, b_bf16, preferred_element_type=F32)
            + jnp.dot(lo, b_bf16, preferred_element_type=F32))


def _cparams(sem):
    return pltpu.CompilerParams(dimension_semantics=sem, vmem_limit_bytes=VMEM_LIMIT)


def _mod_kernel(c_ref, w_ref, b_ref, o_ref):
    a = _silu(c_ref[...])
    o_ref[...] = _dot3_both(a, w_ref[...]) + b_ref[...]


def _dot3_both(a, w):
    ah, am, al = _split3(a)
    wh, wm, wl = _split3(w)
    d = lambda p, q: jnp.dot(p, q, preferred_element_type=F32)
    return (d(ah, wh) + (d(ah, wm) + d(am, wh))
            + (d(ah, wl) + d(am, wm) + d(al, wh)))


def _modulation(cc, w_mod, b_mod):
    depth = w_mod.shape[0]
    tn = 1024
    return pl.pallas_call(
        _mod_kernel,
        grid=(depth, N_MOD * D_MODEL // tn),
        in_specs=[
            pl.BlockSpec((16, D_MODEL), lambda l, j: (0, 0)),
            pl.BlockSpec((None, D_MODEL, tn), lambda l, j: (l, 0, j)),
            pl.BlockSpec((None, 1, tn), lambda l, j: (l, 0, j)),
        ],
        out_specs=pl.BlockSpec((None, 16, tn), lambda l, j: (l, 0, j)),
        out_shape=jax.ShapeDtypeStruct((depth, 16, N_MOD * D_MODEL), F32),
        compiler_params=_cparams(("parallel", "parallel")),
        name="modulation",
    )(cc, w_mod, b_mod.reshape(depth, 1, N_MOD * D_MODEL))


def _rope(x, cos, sin_a, sin_b):
    return x * cos + pltpu.roll(x, LANES - ROPE_PAIRS, axis=1) * sin_a + pltpu.roll(x, ROPE_PAIRS, axis=1) * sin_b


def _in_proj_kernel(x_ref, sh_ref, sc_ref, nw_ref, w_ref, wdt_ref, wdtt_ref,
                    u_ref, dt_ref, dtt_ref, h_ref):
    @pl.when(pl.program_id(1) == 0)
    def _():
        x = x_ref[...]
        r = lax.rsqrt(jnp.mean(x * x, axis=-1, keepdims=True) + EPS)
        h = (x * r * nw_ref[...]) * (1.0 + sc_ref[0]) + sh_ref[0]
        hb = h.astype(BF16)
        h_ref[...] = hb
        dt_ref[...] = jnp.dot(hb, wdt_ref[...], preferred_element_type=F32)
        dtt_ref[...] = lax.dot_general(wdtt_ref[...], hb, (((1,), (1,)), ((), ())),
                                       preferred_element_type=F32)

    u_ref[...] = jnp.dot(h_ref[...], w_ref[...], preferred_element_type=F32).astype(BF16)


def _in_proj(x2d, mod3, mod_row, norm_w, w_main, w_dt, w_dtt, *, layer, tm, tn, n_out):
    rows = x2d.shape[0]
    cb0 = 0
    assert n_out % tn == 0
    return pl.pallas_call(
        _in_proj_kernel,
        grid=(rows // tm, n_out // tn),
        in_specs=[
            pl.BlockSpec((tm, D_MODEL), lambda i, j: (i, 0)),
            pl.BlockSpec((1, 1, D_MODEL), lambda i, j: (mod_row(i), 0, 0)),
            pl.BlockSpec((1, 1, D_MODEL), lambda i, j: (mod_row(i), 0, 1)),
            pl.BlockSpec((1, D_MODEL), lambda i, j: (0, 0)),
            pl.BlockSpec((None, D_MODEL, tn), lambda i, j: (layer, 0, j + cb0)),
            pl.BlockSpec((None, D_MODEL, LANES), lambda i, j: (layer, 0, 0)),
            pl.BlockSpec((None, N_DT, D_MODEL), lambda i, j: (layer, 0, 0)),
        ],
        out_specs=[
            pl.BlockSpec((tm, tn), lambda i, j: (i, j)),
            pl.BlockSpec((tm, LANES), lambda i, j: (i, 0)),
            pl.BlockSpec((N_DT, tm), lambda i, j: (0, i)),
        ],
        out_shape=[
            jax.ShapeDtypeStruct((rows, n_out), BF16),
            jax.ShapeDtypeStruct((rows, LANES), F32),
            jax.ShapeDtypeStruct((N_DT, rows), F32),
        ],
        scratch_shapes=[pltpu.VMEM((tm, D_MODEL), BF16)],
        compiler_params=_cparams(("parallel", "arbitrary")),
        name="in_proj",
    )(x2d, mod3, mod3, norm_w, w_main, w_dt, w_dtt)


def _ssd_kernel(xl_ref, bl_ref, cl_ref, xc_ref, bc_ref, cc_ref,
                dtl_ref, dtc_ref, dttl_ref, dttc_ref,
                cwx_ref, cwb_ref, cwc_ref, cbx_ref, cbb_ref, cbc_ref,
                biasr_ref, biasc_ref, alogx_ref, alogc_ref, dskip_ref,
                yl_ref, yc_ref,
                cm_s, y_s, ex_s, st_s, dec_s, hin_s, *, n_lat, n_ctx):
    t = SSD_CHUNK
    g = pl.program_id(1)
    nc_ctx = n_ctx // t
    nc_lat = n_lat // t
    n_chunks = nc_ctx + nc_lat
    gx = GROUP_X

    ri = lax.broadcasted_iota(jnp.int32, (t, t), 0)
    ci = lax.broadcasted_iota(jnp.int32, (t, t), 1)
    lower = ri >= ci
    tril = lower.astype(BF16)
    triu = (ri <= ci).astype(BF16)
    er = lax.broadcasted_iota(jnp.int32, (LANES, 2 * gx), 0)
    ec = lax.broadcasted_iota(jnp.int32, (LANES, 2 * gx), 1)
    expand = (er == g * (2 * SSD_HEADS_PER_GROUP) + ec // SSD_HEAD_DIM).astype(BF16)
    head_of_lane = lax.broadcasted_iota(jnp.int32, (t, gx), 1) // SSD_HEAD_DIM
    tt = jnp.concatenate([triu, tril], axis=1)
    a_x = -jnp.exp(alogx_ref[0]) * LOG2_E
    a_c = -jnp.exp(alogc_ref[...]) * LOG2_E

    si = lax.broadcasted_iota(jnp.int32, (2 * t, t + 2 * BF16_ROWS), 0)
    sj = lax.broadcasted_iota(jnp.int32, (2 * t, t + 2 * BF16_ROWS), 1)
    src = jnp.where(si < t, jnp.where(si == 0, t + BF16_ROWS - 1, si - 1),
                    jnp.where(si == 2 * t - 1, t + BF16_ROWS, si - t + 1))
    shift_m = (sj == src).astype(BF16)
    conv_w = jnp.concatenate([cwx_ref[...], cwb_ref[...], cwc_ref[...]], axis=1)
    conv_b = jnp.concatenate([cbx_ref[...], cbb_ref[...], cbc_ref[...]], axis=1)

    def phase_a(cs, x_ref, b_ref, c_ref, dt_ref, dtt_ref, n_seg_chunks, chunk0):
        seg_rows = n_seg_chunks * t
        n8 = 2 * SSD_HEADS_PER_GROUP
        st = [dict(c=c, r0=pl.multiple_of(c * t, t), o0=pl.multiple_of((c + chunk0) * t, t)) for c in cs]

        for s in st:
            c, r0 = s["c"], s["r0"]
            lo = pl.multiple_of(jnp.maximum(r0 - BF16_ROWS, 0), BF16_ROWS)
            hi = pl.multiple_of(jnp.minimum(r0 + t, seg_rows - BF16_ROWS), BF16_ROWS)
            rows = lambda a, n: jnp.concatenate(
                [x_ref[0, pl.ds(a, n), :], b_ref[0, pl.ds(a, n), :], c_ref[0, pl.ds(a, n), :]], axis=1)
            cur = rows(r0, t)
            zero = jnp.zeros((BF16_ROWS, cur.shape[1]), BF16)
            before = jnp.where(c > 0, rows(lo, BF16_ROWS), zero)
            after = jnp.where(c < n_seg_chunks - 1, rows(hi, BF16_ROWS), zero)
            s["cur"] = cur
            s["sh"] = jnp.dot(shift_m, jnp.concatenate([cur, before, after], axis=0),
                              preferred_element_type=F32)
            dt = _softplus(dt_ref[0, pl.ds(r0, t), :] + biasr_ref[...])
            s["dtx"] = _dot2_lhs(dt, expand)
            at = _softplus(dtt_ref[:, pl.ds(r0, t)] + biasc_ref[...]) * a_c
            hi3, mid3, lo3 = _split3(at)
            s["ct3"] = jnp.dot(jnp.concatenate([hi3, mid3, lo3], axis=0), tt, preferred_element_type=F32)

        for s in st:
            ax = s["dtx"] * a_x
            s["csf"] = _dot2_rhs(tril, ax[:, :gx])
            s["csb"] = _dot2_rhs(triu, ax[:, gx:])
            sh = s["sh"]
            xbc = _silu(sh[:t] * conv_w[0:1] + s["cur"].astype(F32) * conv_w[1:2]
                        + sh[t:] * conv_w[2:3] + conv_b)
            s["xs"] = xbc[:, :gx]
            s["bm"] = xbc[:, gx:gx + SSD_STATE].astype(BF16)
            s["cm"] = xbc[:, gx + SSD_STATE:].astype(BF16)
            cm_s[pl.ds(s["o0"], t), :] = s["cm"]
            s["cb"] = _bdot_nt(s["cm"], s["bm"])

        for s in st:
            ct3 = s["ct3"]
            cst = ct3[0:n8] + ct3[n8:2 * n8] + ct3[2 * n8:3 * n8]
            xs, dtx, cb = s["xs"], s["dtx"], s["cb"]
            y = dskip_ref[0] * xs
            for d, cs_d in ((0, s["csf"]), (1, s["csb"])):
                mask = lower if d == 0 else (ri <= ci)
                xd = (xs * dtx[:, d * gx:(d + 1) * gx]).astype(BF16)
                ms, blocks = [], []
                for r in range(SSD_HEADS_PER_GROUP):
                    col = cs_d[:, r * SSD_HEAD_DIM:r * SSD_HEAD_DIM + 1]
                    k = d * SSD_HEADS_PER_GROUP + r
                    row = cst[k:k + 1, d * t:(d + 1) * t]
                    seg = jnp.exp2(jnp.where(mask, col - row, -jnp.inf))
                    ms.append((cb * seg).astype(BF16))
                    blocks.append(jnp.where(head_of_lane == r, xd, jnp.zeros_like(xd)))
                y = y + jnp.dot(jnp.concatenate(ms, axis=1), jnp.concatenate(blocks, axis=0),
                                preferred_element_type=F32)
            y_s[pl.ds(s["o0"], t), :] = y

        for s in st:
            xs, dtx, csf, csb = s["xs"], s["dtx"], s["csf"], s["csb"]
            endf = csf[t - 1:t, :]
            endb = csb[0:1, :]
            ex_s[pl.ds(s["o0"], t), :] = jnp.exp2(jnp.concatenate([csf, csb], axis=1))
            xdd = jnp.concatenate([xs * dtx[:, :gx] * jnp.exp2(endf - csf),
                                   xs * dtx[:, gx:] * jnp.exp2(endb - csb)], axis=1).astype(BF16)
            bt = jnp.transpose(s["bm"].astype(F32)).astype(BF16)
            st_s[s["c"] + chunk0] = jnp.dot(bt, xdd, preferred_element_type=F32)
            dec_s[pl.ds(s["c"] + chunk0, 1), :] = jnp.exp2(jnp.concatenate([endf, endb], axis=1))

    def run_phase_a(n_seg_chunks, chunk0, refs):
        group = math.gcd(n_seg_chunks, SSD_GROUP_CHUNKS)

        def body(i, carry):
            phase_a([i * group + k for k in range(group)], *refs, n_seg_chunks, chunk0)
            return carry

        lax.fori_loop(0, n_seg_chunks // group, body, 0)

    run_phase_a(nc_ctx, 0, (xc_ref, bc_ref, cc_ref, dtc_ref, dttc_ref))
    run_phase_a(nc_lat, nc_ctx, (xl_ref, bl_ref, cl_ref, dtl_ref, dttl_ref))

    order_f = list(range(n_chunks))
    order_b = list(range(nc_ctx - 1, -1, -1)) + list(range(n_chunks - 1, nc_ctx - 1, -1))
    for d, order in ((0, order_f), (1, order_b)):
        h = jnp.zeros((SSD_STATE, gx), F32)
        for c in order:
            hin_s[c, :, d * gx:(d + 1) * gx] = h.astype(BF16)
            h = dec_s[c:c + 1, d * gx:(d + 1) * gx] * h + st_s[c, :, d * gx:(d + 1) * gx]

    def phase_c(cs, out_ref, chunk0):
        offs = [pl.multiple_of((c + chunk0) * t, t) for c in cs]
        yos = [jnp.dot(cm_s[pl.ds(o0, t), :], hin_s[c + chunk0], preferred_element_type=F32)
               for c, o0 in zip(cs, offs)]
        for c, o0, yo in zip(cs, offs, yos):
            yo = yo * ex_s[pl.ds(o0, t), :]
            y = y_s[pl.ds(o0, t), :] + yo[:, :gx] + yo[:, gx:]
            out_ref[0, pl.ds(pl.multiple_of(c * t, t), t), :] = y.astype(BF16)

    def run_phase_c(n_seg_chunks, chunk0, out_ref):
        group = math.gcd(n_seg_chunks, SSD_GROUP_CHUNKS)

        def body(i, carry):
            phase_c([i * group + k for k in range(group)], out_ref, chunk0)
            return carry

        lax.fori_loop(0, n_seg_chunks // group, body, 0)

    run_phase_c(nc_ctx, 0, yc_ref)
    run_phase_c(nc_lat, nc_ctx, yl_ref)


def _ssd(ul3, uc3, cols_l, cols_c, dt_l, dt_c, dtt_l, dtt_c, p):
    bsz, n_lat, _ = ul3.shape
    n_ctx = uc3.shape[1]
    gx = GROUP_X
    n_chunks = (n_lat + n_ctx) // SSD_CHUNK
    n_tot = n_lat + n_ctx

    def seq_specs(cols, n):
        xb, bb, cb = cols["xs"] // gx, cols["b"] // SSD_STATE, cols["c"] // SSD_STATE
        return [
            pl.BlockSpec((1, n, gx), lambda b, g: (b, 0, xb + g)),
            pl.BlockSpec((1, n, SSD_STATE), lambda b, g: (b, 0, bb + g)),
            pl.BlockSpec((1, n, SSD_STATE), lambda b, g: (b, 0, cb + g)),
        ]

    xoff = 0
    boff = SSD_WIDTH // SSD_STATE
    coff = (SSD_WIDTH + SSD_BC_WIDTH) // SSD_STATE
    n8 = 2 * SSD_HEADS_PER_GROUP
    in_specs = (
        seq_specs(cols_l, n_lat) + seq_specs(cols_c, n_ctx) + [
            pl.BlockSpec((1, n_lat, LANES), lambda b, g: (b, 0, 0)),
            pl.BlockSpec((1, n_ctx, LANES), lambda b, g: (b, 0, 0)),
            pl.BlockSpec((n8, n_lat), lambda b, g: (g, b)),
            pl.BlockSpec((n8, n_ctx), lambda b, g: (g, b)),
            pl.BlockSpec((3, gx), lambda b, g: (0, xoff + g)),
            pl.BlockSpec((3, SSD_STATE), lambda b, g: (0, boff + g)),
            pl.BlockSpec((3, SSD_STATE), lambda b, g: (0, coff + g)),
            pl.BlockSpec((1, gx), lambda b, g: (0, xoff + g)),
            pl.BlockSpec((1, SSD_STATE), lambda b, g: (0, boff + g)),
            pl.BlockSpec((1, SSD_STATE), lambda b, g: (0, coff + g)),
            pl.BlockSpec((1, LANES), lambda b, g: (0, 0)),
            pl.BlockSpec((n8, 1), lambda b, g: (g, 0)),
            pl.BlockSpec((1, 1, 2 * gx), lambda b, g: (g, 0, 0)),
            pl.BlockSpec((n8, 1), lambda b, g: (g, 0)),
            pl.BlockSpec((1, 1, gx), lambda b, g: (g, 0, 0)),
        ])
    return pl.pallas_call(
        functools.partial(_ssd_kernel, n_lat=n_lat, n_ctx=n_ctx),
        grid=(bsz, SSD_GROUPS),
        in_specs=in_specs,
        out_specs=[
            pl.BlockSpec((1, n_lat, gx), lambda b, g: (b, 0, g)),
            pl.BlockSpec((1, n_ctx, gx), lambda b, g: (b, 0, g)),
        ],
        out_shape=[
            jax.ShapeDtypeStruct((bsz, n_lat, SSD_WIDTH), BF16),
            jax.ShapeDtypeStruct((bsz, n_ctx, SSD_WIDTH), BF16),
        ],
        scratch_shapes=[
            pltpu.VMEM((n_tot, SSD_STATE), BF16),
            pltpu.VMEM((n_tot, gx), F32),
            pltpu.VMEM((n_tot, 2 * gx), F32),
            pltpu.VMEM((n_chunks, SSD_STATE, 2 * gx), F32),
            pltpu.VMEM((n_chunks + (-n_chunks) % 8, 2 * gx), F32),
            pltpu.VMEM((n_chunks, SSD_STATE, 2 * gx), BF16),
        ],
        compiler_params=_cparams(("parallel", "parallel")),
        name="ssd_scan",
    )(ul3, ul3, ul3, uc3, uc3, uc3,
      dt_l.reshape(bsz, n_lat, LANES), dt_c.reshape(bsz, n_ctx, LANES), dtt_l, dtt_c,
      p["conv_w"], p["conv_w"], p["conv_w"], p["conv_b"], p["conv_b"], p["conv_b"],
      p["bias_row"], p["bias_col"], p["alog_x"], p["alog_col"], p["dskip_x"])


def _attn_kernel(*refs, n_ctx, n_lat, tq, lam_init, heads):
    if n_lat:
        (q_ref, kc_ref, vc_ref, kl_ref, vl_ref, cos_ref, sa_ref, sb_ref, lam_ref, sw_ref,
         o_ref, kl_s) = refs
    else:
        q_ref, kc_ref, vc_ref, lam_ref, sw_ref, o_ref = refs
    qi = pl.program_id(2)

    if n_lat:
        @pl.when(qi == 0)
        def _():
            kl_s[...] = _rope(kl_ref[0].astype(F32), cos_ref[...], sa_ref[...], sb_ref[...]).astype(BF16)

    lf = lam_ref[...]
    lam = (jnp.exp(jnp.sum(lf[0:1] * lf[1:2], axis=-1, keepdims=True))
           - jnp.exp(jnp.sum(lf[2:3] * lf[3:4], axis=-1, keepdims=True)) + lam_init)

    for hh in range(heads):
        cols = slice(hh * ATT_V_DIM, (hh + 1) * ATT_V_DIM)
        _attend_head(q_ref.at[0, :, cols], kc_ref.at[0, :, cols], vc_ref.at[0, :, cols],
                     kl_s if n_lat else None, vl_ref.at[0] if n_lat else None,
                     (cos_ref, sa_ref, sb_ref) if n_lat else None, lam, sw_ref, o_ref.at[0, :, cols],
                     qi=qi, n_ctx=n_ctx, n_lat=n_lat, tq=tq, lam_init=lam_init)


def _attend_head(q_ref, kc_ref, vc_ref, kl_s, vl_ref, tabs, lam, sw_ref, o_ref, *,
                 qi, n_ctx, n_lat, tq, lam_init):
    hd = ATT_HEAD_DIM
    q = q_ref[...].astype(F32)
    if n_lat:
        cos_ref, sa_ref, sb_ref = tabs
    if n_lat:
        r0 = pl.multiple_of(qi * tq, tq)
        q = _rope(q, cos_ref[pl.ds(r0, tq), :], sa_ref[pl.ds(r0, tq), :], sb_ref[pl.ds(r0, tq), :])
    q = (q * (ATT_SCALE * LOG2_E)).astype(BF16)
    lane = lax.broadcasted_iota(jnp.int32, (1, ATT_V_DIM), 1)
    q_comp = [jnp.where(lane < hd, q, jnp.zeros_like(q)), jnp.where(lane >= hd, q, jnp.zeros_like(q))]

    sub = min(ATT_SUB_ROWS, tq)
    chains = [(r, c) for r in range(tq // sub) for c in range(2)]
    nt = (((1,), (1,)), ((), ()))

    def scores(r, c):
        qc = q_comp[c][r * sub:(r + 1) * sub, :]
        s = lax.dot_general(qc, kc_ref[...], nt, preferred_element_type=F32)
        if n_lat:
            s = jnp.concatenate([s, lax.dot_general(qc, kl_s[...], nt, preferred_element_type=F32)],
                                axis=1)
        return s

    def weighted_values(e):
        pv = jnp.dot(e[:, :n_ctx], vc_ref[...], preferred_element_type=F32)
        if n_lat:
            pv = pv + jnp.dot(e[:, n_ctx:], vl_ref[...], preferred_element_type=F32)
        return pv

    ahead = ATT_CHAINS_AHEAD
    pending = [scores(*ch) for ch in chains[:ahead]]
    outs = {}
    for n, (r, c) in enumerate(chains):
        s = pending.pop(0)
        if n + ahead < len(chains):
            pending.append(scores(*chains[n + ahead]))
        e = jnp.exp2(s - jnp.max(s, axis=-1, keepdims=True))
        inv = 1.0 / jnp.sum(e, axis=-1, keepdims=True)
        outs[c] = weighted_values(e.astype(BF16)) * inv
        if c == 1:
            o = outs[0] - lam * outs[1]
            on = o * lax.rsqrt(jnp.mean(o * o, axis=-1, keepdims=True) + EPS) * sw_ref[...]
            o_ref[r * sub:(r + 1) * sub, :] = (on * (1.0 - lam_init)).astype(BF16)


def _attention(uq3, col_q, uc3, cols_c, ul3, cols_l, rope_tabs, lam_p, subln_w, *, lam_init, tq):
    bsz, n_q, _ = uq3.shape
    n_ctx = uc3.shape[1]
    n_lat = 0 if ul3 is None else ul3.shape[1]
    vd = ATT_V_DIM
    heads = 1 if n_lat else ATT_HEADS
    hw = heads * vd
    in_specs = [
        pl.BlockSpec((1, tq, hw), lambda b, h, i: (b, i, col_q // hw + h)),
        pl.BlockSpec((1, n_ctx, hw), lambda b, h, i: (b, 0, cols_c["k"] // hw + h)),
        pl.BlockSpec((1, n_ctx, hw), lambda b, h, i: (b, 0, cols_c["v"] // hw + h)),
    ]
    args = [uq3, uc3, uc3]
    if n_lat:
        in_specs += [
            pl.BlockSpec((1, n_lat, vd), lambda b, h, i: (b, 0, cols_l["k"] // vd + h)),
            pl.BlockSpec((1, n_lat, vd), lambda b, h, i: (b, 0, cols_l["v"] // vd + h)),
            pl.BlockSpec((n_lat, vd), lambda b, h, i: (0, 0)),
            pl.BlockSpec((n_lat, vd), lambda b, h, i: (0, 0)),
            pl.BlockSpec((n_lat, vd), lambda b, h, i: (0, 0)),
        ]
        args += [ul3, ul3, *rope_tabs]
    in_specs += [
        pl.BlockSpec((4, ATT_HEAD_DIM), lambda b, h, i: (0, 0)),
        pl.BlockSpec((1, vd), lambda b, h, i: (0, 0)),
    ]
    args += [lam_p, subln_w]
    return pl.pallas_call(
        functools.partial(_attn_kernel, n_ctx=n_ctx, n_lat=n_lat, tq=tq, lam_init=lam_init,
                          heads=heads),
        grid=(bsz, ATT_HEADS // heads, n_q // tq),
        in_specs=in_specs,
        out_specs=pl.BlockSpec((1, tq, hw), lambda b, h, i: (b, i, h)),
        out_shape=jax.ShapeDtypeStruct((bsz, n_q, ATT_WIDTH), BF16),
        scratch_shapes=[pltpu.VMEM((n_lat, vd), BF16)] if n_lat else [],
        compiler_params=_cparams(("parallel", "parallel", "arbitrary")),
        name="diff_attn",
    )(*args)


def _merge_kernel(ys_ref, z_ref, ya_ref, gs_ref, ga_ref, x_ref, g1_ref, nw_ref,
                  wbs_ref, wba_ref, wo_ref, o_ref):
    ta = _sigmoid(ga_ref[...].astype(F32)) * jnp.dot(ya_ref[...], wba_ref[...], preferred_element_type=F32)
    yz = ys_ref[...].astype(F32) * _silu(z_ref[...].astype(F32))
    ysn = yz * lax.rsqrt(jnp.mean(yz * yz, axis=-1, keepdims=True) + EPS) * nw_ref[...]
    ts = jnp.dot(ysn.astype(BF16), wbs_ref[...], preferred_element_type=F32)
    tmix = _sigmoid(gs_ref[...].astype(F32)) * ts + ta
    o_ref[...] = x_ref[...] + g1_ref[0] * jnp.dot(tmix.astype(BF16), wo_ref[...],
                                                   preferred_element_type=F32)


def _merge(ys2, u2, cols, ya2, x2d, mod3, mod_row, norm_w, w_bs, w_ba, w_o, *, layer, tm):
    rows = x2d.shape[0]
    d = D_MODEL
    row_blk = lambda cb: pl.BlockSpec((tm, d), lambda i: (i, cb))
    full = lambda shape: pl.BlockSpec(shape, lambda i: (0, 0))
    weight = pl.BlockSpec((None, d, d), lambda i: (layer, 0, 0))
    return pl.pallas_call(
        _merge_kernel,
        grid=(rows // tm,),
        in_specs=[
            row_blk(0), row_blk(cols["z"] // d), row_blk(0),
            row_blk(cols["gs"] // d), row_blk(cols["ga"] // d), row_blk(0),
            pl.BlockSpec((1, 1, d), lambda i: (mod_row(i), 0, 2)),
            full((1, d)), weight, weight, weight,
        ],
        out_specs=row_blk(0),
        out_shape=jax.ShapeDtypeStruct((rows, d), F32),
        compiler_params=_cparams(("parallel",)),
        name="branch_merge",
    )(ys2, u2, ya2, u2, u2, x2d, mod3, norm_w, w_bs, w_ba, w_o)


def _ffn_kernel(x_ref, xp_ref, xn_ref, sh_ref, sc_ref, g2_ref, nw_ref, wu_ref, cw_ref, cb_ref,
                wd_ref, fw_ref, o_ref, h_s, u_s, act_s, acc_s, *, tm, tf, tiles_per_seg, final_norm):
    i = pl.program_id(0)
    halo = BF16_ROWS
    seg_first = (i % tiles_per_seg) == 0
    seg_last = (i % tiles_per_seg) == tiles_per_seg - 1

    def norm_mod(x):
        r = lax.rsqrt(jnp.mean(x * x, axis=-1, keepdims=True) + EPS)
        return ((x * r * nw_ref[...]) * (1.0 + sc_ref[0]) + sh_ref[0]).astype(BF16)

    h_s[0:halo, :] = norm_mod(xp_ref[...])
    h_s[halo:halo + tm, :] = norm_mod(x_ref[...])
    h_s[halo + tm:2 * halo + tm, :] = norm_mod(xn_ref[...])

    def up_proj(j):
        us = u_s.at[j % 2]
        for half in range(2):
            c0 = half * D_FF + j * tf
            us[:, half * tf:(half + 1) * tf] = jnp.dot(h_s[...], wu_ref[:, c0:c0 + tf],
                                                       preferred_element_type=F32)
        us[halo - 1:halo, :] = jnp.where(seg_first, 0.0, us[halo - 1:halo, :])
        us[halo + tm:halo + tm + 1, :] = jnp.where(seg_last, 0.0, us[halo + tm:halo + tm + 1, :])

    n_f = D_FF // tf
    bounds = [0]
    while bounds[-1] < n_f:
        left = n_f - bounds[-1]
        bounds.append(bounds[-1] + (FFN_DOWN_GROUP if left > FFN_DOWN_GROUP + 2 else (left + 1) // 2))
    group_of = {j: g for g in range(len(bounds) - 1) for j in range(bounds[g], bounds[g + 1])}

    def down_proj(g):
        j0, j1 = bounds[g], bounds[g + 1]
        down = jnp.dot(act_s[g % 2, :, 0:(j1 - j0) * tf], wd_ref[j0 * tf:j1 * tf, :],
                       preferred_element_type=F32)
        if g == 0:
            acc_s[...] = down
        else:
            acc_s[...] += down

    up_proj(0)
    for j in range(n_f):
        if j + 1 < n_f:
            up_proj(j + 1)
        if j in bounds[1:]:
            down_proj(group_of[j] - 1)
        us = u_s.at[j % 2]
        w = jnp.concatenate([cw_ref[:, j * tf:(j + 1) * tf],
                             cw_ref[:, D_FF + j * tf:D_FF + (j + 1) * tf]], axis=1)
        b = jnp.concatenate([cb_ref[:, j * tf:(j + 1) * tf],
                             cb_ref[:, D_FF + j * tf:D_FF + (j + 1) * tf]], axis=1)
        uc = (us[halo - 1:halo - 1 + tm, :] * w[0:1] + us[halo:halo + tm, :] * w[1:2]
              + us[halo + 1:halo + 1 + tm, :] * w[2:3] + b)
        g, k = group_of[j], j - bounds[group_of[j]]
        act_s[g % 2, :, k * tf:(k + 1) * tf] = (_silu(uc[:, :tf]) * uc[:, tf:]).astype(BF16)
    down_proj(len(bounds) - 2)
    y = x_ref[...] + g2_ref[0] * acc_s[...]
    if final_norm:
        y = y * lax.rsqrt(jnp.mean(y * y, axis=-1, keepdims=True) + EPS) * fw_ref[...]
    o_ref[...] = y


def _conv_ffn(x2d, mod3, mod_row, norm_w, w_up, conv_w, conv_b, w_down, final_w, *,
              layer, tm, tf, seg_len, final_norm):
    rows = x2d.shape[0]
    d = D_MODEL
    halo = BF16_ROWS
    hb = tm // halo
    last_blk = rows // halo - 1
    assert seg_len % tm == 0
    resident = lambda shape: pl.BlockSpec(shape, lambda i: (0, 0), pipeline_mode=pl.Buffered(1))
    layered = lambda shape: pl.BlockSpec((None, *shape), lambda i: (layer, 0, 0),
                                         pipeline_mode=pl.Buffered(1))
    return pl.pallas_call(
        functools.partial(_ffn_kernel, tm=tm, tf=tf, tiles_per_seg=seg_len // tm, final_norm=final_norm),
        grid=(rows // tm,),
        in_specs=[
            pl.BlockSpec((tm, d), lambda i: (i, 0)),
            pl.BlockSpec((halo, d), lambda i: (jnp.maximum(i * hb - 1, 0), 0)),
            pl.BlockSpec((halo, d), lambda i: (jnp.minimum((i + 1) * hb, last_blk), 0)),
            pl.BlockSpec((1, 1, d), lambda i: (mod_row(i), 0, 3)),
            pl.BlockSpec((1, 1, d), lambda i: (mod_row(i), 0, 4)),
            pl.BlockSpec((1, 1, d), lambda i: (mod_row(i), 0, 5)),
            resident((1, d)),
            layered((d, 2 * D_FF)),
            layered((3, 2 * D_FF)),
            layered((1, 2 * D_FF)),
            layered((D_FF, d)),
            resident((1, d)),
        ],
        out_specs=pl.BlockSpec((tm, d), lambda i: (i, 0)),
        out_shape=jax.ShapeDtypeStruct((rows, d), F32),
        scratch_shapes=[
            pltpu.VMEM((tm + 2 * halo, d), BF16),
            pltpu.VMEM((2, tm + 2 * halo, 2 * tf), F32),
            pltpu.VMEM((2, tm, FFN_DOWN_GROUP * tf), BF16),
            pltpu.VMEM((tm, d), F32),
        ],
        compiler_params=_cparams(("parallel",)),
        name="conv_ffn",
    )(x2d, x2d, x2d, mod3, mod3, mod3, norm_w, w_up, conv_w, conv_b, w_down, final_w)


def _split_in_weight(w):
    dt0 = SSD_WIDTH + SSD_CONV_CH
    dt = w[..., dt0:dt0 + N_DT]
    main = jnp.concatenate([w[..., :dt0], w[..., dt0 + N_DT:]], axis=-1).astype(BF16)
    return main, _dt_param_order(dt.reshape(*dt.shape[:-1], 2, SSD_HEADS))


def _dt_param_order(p2h):
    lead = p2h.shape[:-2]
    p = p2h.reshape(*lead, 2, SSD_GROUPS, SSD_HEADS_PER_GROUP)
    return jnp.swapaxes(p, -3, -2).reshape(*lead, N_DT)


def _rope_tables(n_tokens):
    rows = n_tokens // GRID_W
    inv_freq = ROPE_BASE ** (-jnp.arange(ROPE_PAIRS, dtype=F32) / ROPE_PAIRS)
    ang_r = jnp.broadcast_to(jnp.arange(rows, dtype=F32)[:, None, None] * inv_freq, (rows, GRID_W, ROPE_PAIRS))
    ang_c = jnp.broadcast_to(jnp.arange(GRID_W, dtype=F32)[None, :, None] * inv_freq, (rows, GRID_W, ROPE_PAIRS))
    ang = jnp.stack([ang_r, ang_c], axis=2).reshape(n_tokens, 2, 1, ROPE_PAIRS)
    cos = jnp.broadcast_to(jnp.cos(ang), (n_tokens, 2, 2, ROPE_PAIRS))
    sin = jnp.broadcast_to(jnp.sin(ang), (n_tokens, 2, 2, ROPE_PAIRS))
    zero = jnp.zeros_like(sin[:, :, :1])
    sin_a = jnp.concatenate([-sin[:, :, :1], zero], axis=2)
    sin_b = jnp.concatenate([zero, sin[:, :, 1:]], axis=2)
    tile = lambda a: jnp.tile(a.reshape(n_tokens, ATT_HEAD_DIM), (1, 2))
    return tile(cos), tile(sin_a), tile(sin_b)


def kernel(x, c, ctx, c_ctx, w_mod, b_mod, norm1_w, w_in, ssd_conv_w, ssd_conv_b, ssd_a_log,
           ssd_dt_bias, ssd_d, ssd_norm_w, diff_lambda, att_subln_w, w_br_ssd, w_br_att, w_out,
           norm2_w, w_up, ffn_conv_w, ffn_conv_b, w_down, final_norm_w):
    bsz, n_lat, d = x.shape
    n_ctx = ctx.shape[1]
    depth = w_mod.shape[0]
    assert d == D_MODEL and bsz + 1 <= 16
    ctx_row = bsz

    tm_in = min(1024, n_lat)
    tm_in_c = min(1024, bsz * n_ctx)
    tn_in = 4096
    tn_in_c_last = 2048
    tm_merge = min(512, n_lat)
    tm_merge_c = min(512, bsz * n_ctx)
    tm_ffn = min(512, n_lat)
    tf = 256
    tq = min(2048, n_lat)

    cc = jnp.zeros((16, d), F32).at[:bsz].set(c).at[ctx_row].set(c_ctx)
    mod = _modulation(cc, w_mod, b_mod)

    rope_tabs = _rope_tables(n_lat)
    cols_full = dict(z=COL_Z, q=COL_Q, gs=COL_GS, ga=COL_GA, xs=COL_XS, b=COL_B, c=COL_C, k=COL_K, v=COL_V)

    xl = x.reshape(bsz * n_lat, d)
    xc = ctx.reshape(bsz * n_ctx, d)
    lat_row = lambda tm: (lambda i: (i * tm) // n_lat)
    ctx_rowf = lambda i: ctx_row

    w_main, w_dt32 = _split_in_weight(w_in)
    w_dt = jnp.pad(w_dt32, ((0, 0), (0, 0), (0, LANES - N_DT))).astype(BF16)
    w_dtt = jnp.swapaxes(w_dt32, 1, 2).astype(BF16)
    w_bs = w_br_ssd.astype(BF16)
    w_ba = w_br_att.astype(BF16)
    w_o = w_out.astype(BF16)
    w_u = w_up.astype(BF16)
    w_d = w_down.astype(BF16)
    cb = ffn_conv_b.reshape(depth, 1, 2 * D_FF)

    for li in range(depth):
        with_ctx = li < depth - 1
        mod3 = mod[li].reshape(16, 1, N_MOD * d)
        n1 = norm1_w[li].reshape(1, d)

        ul, dt_l, dtt_l = _in_proj(xl, mod3, lat_row(tm_in), n1, w_main, w_dt, w_dtt,
                                   layer=li, tm=tm_in, tn=tn_in, n_out=N_COLS)
        uc, dt_c, dtt_c = _in_proj(xc, mod3, ctx_rowf, n1, w_main, w_dt, w_dtt, layer=li, tm=tm_in_c,
                                   tn=tn_in if with_ctx else tn_in_c_last,
                                   n_out=N_COLS if with_ctx else COL_CTX_LAST_END)
        cols_c = cols_full
        ul3 = ul.reshape(bsz, n_lat, -1)
        uc3 = uc.reshape(bsz, n_ctx, -1)

        ssd_p = dict(
            conv_w=ssd_conv_w[li], conv_b=ssd_conv_b[li].reshape(1, SSD_CONV_CH),
            bias_row=jnp.zeros((1, LANES), F32).at[0, :N_DT].set(_dt_param_order(ssd_dt_bias[li])),
            bias_col=_dt_param_order(ssd_dt_bias[li]).reshape(N_DT, 1),
            alog_col=_dt_param_order(ssd_a_log[li]).reshape(N_DT, 1),
            alog_x=jnp.repeat(ssd_a_log[li].reshape(2, SSD_GROUPS, SSD_HEADS_PER_GROUP).transpose(1, 0, 2)
                              .reshape(SSD_GROUPS, 2 * SSD_HEADS_PER_GROUP), SSD_HEAD_DIM, axis=1)
            .reshape(SSD_GROUPS, 1, 2 * GROUP_X),
            dskip_x=jnp.repeat(ssd_d[li], SSD_HEAD_DIM).reshape(SSD_GROUPS, 1, GROUP_X),
        )
        ys_l, ys_c = _ssd(ul3, uc3, cols_full, cols_c, dt_l, dt_c, dtt_l, dtt_c, ssd_p)

        lam_init = 0.8 - 0.6 * math.exp(-0.3 * li)
        sw = att_subln_w[li].reshape(1, ATT_V_DIM)
        ya_l = _attention(ul3, COL_Q, uc3, cols_c, ul3, cols_full, rope_tabs, diff_lambda[li], sw,
                          lam_init=lam_init, tq=tq)

        sn = ssd_norm_w[li].reshape(1, SSD_WIDTH)
        n2 = norm2_w[li].reshape(1, d)
        fw = final_norm_w.reshape(1, d)

        if with_ctx:
            ya_c = _attention(uc3, cols_c["q"], uc3, cols_c, None, None, None, diff_lambda[li], sw,
                              lam_init=lam_init, tq=n_ctx)
            xc = _merge(ys_c.reshape(-1, SSD_WIDTH), uc, cols_c, ya_c.reshape(-1, ATT_WIDTH), xc, mod3,
                        ctx_rowf, sn, w_bs, w_ba, w_o, layer=li, tm=tm_merge_c)
            xc = _conv_ffn(xc, mod3, ctx_rowf, n2, w_u, ffn_conv_w, cb, w_d, fw, layer=li, tm=n_ctx,
                           tf=tf, seg_len=n_ctx, final_norm=False)

        xl = _merge(ys_l.reshape(-1, SSD_WIDTH), ul, cols_full, ya_l.reshape(-1, ATT_WIDTH), xl, mod3,
                    lat_row(tm_merge), sn, w_bs, w_ba, w_o, layer=li, tm=tm_merge)
        xl = _conv_ffn(xl, mod3, lat_row(tm_ffn), n2, w_u, ffn_conv_w, cb, w_d, fw, layer=li,
                       tm=tm_ffn, tf=tf, seg_len=n_lat, final_norm=not with_ctx)

    return xl.reshape(bsz, n_lat, d)
```

```python
import functools
import math

import jax
import jax.numpy as jnp
from jax import lax
from jax.experimental import pallas as pl
from jax.experimental.pallas import tpu as pltpu

F32 = jnp.float32
BF16 = jnp.bfloat16

D_MODEL = 1024
EPS = 1e-6
N_MOD = 6
GRID_W = 64

SSD_HEADS = 16
SSD_HEAD_DIM = 64
SSD_WIDTH = SSD_HEADS * SSD_HEAD_DIM
SSD_GROUPS = 4
SSD_HEADS_PER_GROUP = SSD_HEADS // SSD_GROUPS
SSD_STATE = 128
SSD_CHUNK = 128
SSD_BC_WIDTH = SSD_GROUPS * SSD_STATE
SSD_CONV_CH = SSD_WIDTH + 2 * SSD_BC_WIDTH
GROUP_X = SSD_HEADS_PER_GROUP * SSD_HEAD_DIM
SSD_GROUP_CHUNKS = 8

ATT_HEADS = 8
ATT_HEAD_DIM = 64
ATT_V_DIM = 2 * ATT_HEAD_DIM
ATT_QK_WIDTH = ATT_HEADS * 2 * ATT_HEAD_DIM
ATT_WIDTH = ATT_HEADS * ATT_V_DIM
ATT_SCALE = ATT_HEAD_DIM ** -0.5
LOG2_E = math.log2(math.e)
ATT_SUB_ROWS = 512
ATT_CHAINS_AHEAD = 2
ROPE_BASE = 10000.0
ROPE_PAIRS = ATT_HEAD_DIM // 4

D_FF = 2816
MERGE_ROW_PIECES = 2
FFN_DOWN_GROUP = 4
N_DT = 2 * SSD_HEADS

COL_Z = 0
COL_XS = COL_Z + SSD_WIDTH
COL_B = COL_XS + SSD_WIDTH
COL_C = COL_B + SSD_BC_WIDTH
COL_Q = COL_C + SSD_BC_WIDTH
COL_K = COL_Q + ATT_QK_WIDTH
COL_V = COL_K + ATT_QK_WIDTH
COL_GS = COL_V + ATT_WIDTH
COL_GA = COL_GS + D_MODEL
N_COLS = COL_GA + D_MODEL
COL_CTX_LAST_END = COL_V + ATT_WIDTH

LANES = 128
BF16_ROWS = 16
VMEM_LIMIT = 56 * 1024 * 1024


def _sigmoid(x):
    return 0.5 * jnp.tanh(0.5 * x) + 0.5


def _silu(x):
    return x * _sigmoid(x)


def _softplus(x):
    return jnp.maximum(x, 0.0) + jnp.log1p(jnp.exp(-jnp.abs(x)))


def _bdot_nt(a, b):
    return lax.dot_general(a.astype(BF16), b.astype(BF16), (((1,), (1,)), ((), ())),
                           preferred_element_type=F32)


def _split3(x):
    hi = x.astype(BF16)
    r1 = x - hi.astype(F32)
    mid = r1.astype(BF16)
    lo = (r1 - mid.astype(F32)).astype(BF16)
    return hi, mid, lo


def _dot2_rhs(a_bf16, x):
    hi = x.astype(BF16)
    lo = (x - hi.astype(F32)).astype(BF16)
    return (jnp.dot(a_bf16, hi, preferred_element_type=F32)
            + jnp.dot(a_bf16, lo, preferred_element_type=F32))


def _dot2_lhs(x, b_bf16):
    hi = x.astype(BF16)
    lo = (x - hi.astype(F32)).astype(BF16)
    return (jnp.dot(hi, b_bf16, preferred_element_type=F32)
            + jnp.dot(lo, b_bf16, preferred_element_type=F32))


def _cparams(sem):
    return pltpu.CompilerParams(dimension_semantics=sem, vmem_limit_bytes=VMEM_LIMIT)


def _mod_kernel(c_ref, w_ref, b_ref, o_ref):
    a = _silu(c_ref[...])
    o_ref[...] = _dot3_both(a, w_ref[...]) + b_ref[...]


def _dot3_both(a, w):
    ah, am, al = _split3(a)
    wh, wm, wl = _split3(w)
    d = lambda p, q: jnp.dot(p, q, preferred_element_type=F32)
    return (d(ah, wh) + (d(ah, wm) + d(am, wh))
            + (d(ah, wl) + d(am, wm) + d(al, wh)))


def _modulation(cc, w_mod, b_mod):
    depth = w_mod.shape[0]
    tn = 1024
    return pl.pallas_call(
        _mod_kernel,
        grid=(depth, N_MOD * D_MODEL // tn),
        in_specs=[
            pl.BlockSpec((16, D_MODEL), lambda l, j: (0, 0)),
            pl.BlockSpec((None, D_MODEL, tn), lambda l, j: (l, 0, j)),
            pl.BlockSpec((None, 1, tn), lambda l, j: (l, 0, j)),
        ],
        out_specs=pl.BlockSpec((None, 16, tn), lambda l, j: (l, 0, j)),
        out_shape=jax.ShapeDtypeStruct((depth, 16, N_MOD * D_MODEL), F32),
        compiler_params=_cparams(("parallel", "parallel")),
        name="modulation",
    )(cc, w_mod, b_mod.reshape(depth, 1, N_MOD * D_MODEL))


def _rope(x, cos, sin_a, sin_b):
    return x * cos + pltpu.roll(x, LANES - ROPE_PAIRS, axis=1) * sin_a + pltpu.roll(x, ROPE_PAIRS, axis=1) * sin_b


def _in_proj_kernel(x_ref, sh_ref, sc_ref, nw_ref, w_ref, wdt_ref, wdtt_ref,
                    u_ref, dt_ref, dtt_ref, h_ref):
    @pl.when(pl.program_id(1) == 0)
    def _():
        x = x_ref[...]
        r = lax.rsqrt(jnp.mean(x * x, axis=-1, keepdims=True) + EPS)
        h = (x * r * nw_ref[...]) * (1.0 + sc_ref[0]) + sh_ref[0]
        hb = h.astype(BF16)
        h_ref[...] = hb
        dt_ref[...] = jnp.dot(hb, wdt_ref[...], preferred_element_type=F32)
        dtt_ref[...] = lax.dot_general(wdtt_ref[...], hb, (((1,), (1,)), ((), ())),
                                       preferred_element_type=F32)

    u_ref[...] = jnp.dot(h_ref[...], w_ref[...], preferred_element_type=F32).astype(BF16)


def _in_proj(x2d, mod3, mod_row, norm_w, w_main, w_dt, w_dtt, *, layer, tm, tn, n_out):
    rows = x2d.shape[0]
    cb0 = 0
    assert n_out % tn == 0
    return pl.pallas_call(
        _in_proj_kernel,
        grid=(rows // tm, n_out // tn),
        in_specs=[
            pl.BlockSpec((tm, D_MODEL), lambda i, j: (i, 0)),
            pl.BlockSpec((1, 1, D_MODEL), lambda i, j: (mod_row(i), 0, 0)),
            pl.BlockSpec((1, 1, D_MODEL), lambda i, j: (mod_row(i), 0, 1)),
            pl.BlockSpec((1, D_MODEL), lambda i, j: (0, 0)),
            pl.BlockSpec((None, D_MODEL, tn), lambda i, j: (layer, 0, j + cb0)),
            pl.BlockSpec((None, D_MODEL, LANES), lambda i, j: (layer, 0, 0)),
            pl.BlockSpec((None, N_DT, D_MODEL), lambda i, j: (layer, 0, 0)),
        ],
        out_specs=[
            pl.BlockSpec((tm, tn), lambda i, j: (i, j)),
            pl.BlockSpec((tm, LANES), lambda i, j: (i, 0)),
            pl.BlockSpec((N_DT, tm), lambda i, j: (0, i)),
        ],
        out_shape=[
            jax.ShapeDtypeStruct((rows, n_out), BF16),
            jax.ShapeDtypeStruct((rows, LANES), F32),
            jax.ShapeDtypeStruct((N_DT, rows), F32),
        ],
        scratch_shapes=[pltpu.VMEM((tm, D_MODEL), BF16)],
        compiler_params=_cparams(("parallel", "arbitrary")),
        name="in_proj",
    )(x2d, mod3, mod3, norm_w, w_main, w_dt, w_dtt)


def _ssd_kernel(xl_ref, bl_ref, cl_ref, xc_ref, bc_ref, cc_ref,
                dtl_ref, dtc_ref, dttl_ref, dttc_ref,
                cwx_ref, cwb_ref, cwc_ref, cbx_ref, cbb_ref, cbc_ref,
                biasr_ref, biasc_ref, alogx_ref, alogc_ref, dskip_ref,
                yl_ref, yc_ref,
                cm_s, y_s, ex_s, st_s, dec_s, hin_s, *, n_lat, n_ctx):
    t = SSD_CHUNK
    g = pl.program_id(1)
    nc_ctx = n_ctx // t
    nc_lat = n_lat // t
    n_chunks = nc_ctx + nc_lat
    gx = GROUP_X

    ri = lax.broadcasted_iota(jnp.int32, (t, t), 0)
    ci = lax.broadcasted_iota(jnp.int32, (t, t), 1)
    lower = ri >= ci
    tril = lower.astype(BF16)
    triu = (ri <= ci).astype(BF16)
    er = lax.broadcasted_iota(jnp.int32, (LANES, 2 * gx), 0)
    ec = lax.broadcasted_iota(jnp.int32, (LANES, 2 * gx), 1)
    expand = (er == g * (2 * SSD_HEADS_PER_GROUP) + ec // SSD_HEAD_DIM).astype(BF16)
    head_of_lane = lax.broadcasted_iota(jnp.int32, (t, gx), 1) // SSD_HEAD_DIM
    tt = jnp.concatenate([triu, tril], axis=1)
    a_x = -jnp.exp(alogx_ref[0]) * LOG2_E
    a_c = -jnp.exp(alogc_ref[...]) * LOG2_E

    si = lax.broadcasted_iota(jnp.int32, (2 * t, t + 2 * BF16_ROWS), 0)
    sj = lax.broadcasted_iota(jnp.int32, (2 * t, t + 2 * BF16_ROWS), 1)
    src = jnp.where(si < t, jnp.where(si == 0, t + BF16_ROWS - 1, si - 1),
                    jnp.where(si == 2 * t - 1, t + BF16_ROWS, si - t + 1))
    shift_m = (sj == src).astype(BF16)
    conv_w = jnp.concatenate([cwx_ref[...], cwb_ref[...], cwc_ref[...]], axis=1)
    conv_b = jnp.concatenate([cbx_ref[...], cbb_ref[...], cbc_ref[...]], axis=1)

    def phase_a(cs, x_ref, b_ref, c_ref, dt_ref, dtt_ref, n_seg_chunks, chunk0):
        seg_rows = n_seg_chunks * t
        n8 = 2 * SSD_HEADS_PER_GROUP
        st = [dict(c=c, r0=pl.multiple_of(c * t, t), o0=pl.multiple_of((c + chunk0) * t, t)) for c in cs]

        for s in st:
            c, r0 = s["c"], s["r0"]
            lo = pl.multiple_of(jnp.maximum(r0 - BF16_ROWS, 0), BF16_ROWS)
            hi = pl.multiple_of(jnp.minimum(r0 + t, seg_rows - BF16_ROWS), BF16_ROWS)
            rows = lambda a, n: jnp.concatenate(
                [x_ref[0, pl.ds(a, n), :], b_ref[0, pl.ds(a, n), :], c_ref[0, pl.ds(a, n), :]], axis=1)
            cur = rows(r0, t)
            zero = jnp.zeros((BF16_ROWS, cur.shape[1]), BF16)
            before = jnp.where(c > 0, rows(lo, BF16_ROWS), zero)
            after = jnp.where(c < n_seg_chunks - 1, rows(hi, BF16_ROWS), zero)
            s["cur"] = cur
            s["sh"] = jnp.dot(shift_m, jnp.concatenate([cur, before, after], axis=0),
                              preferred_element_type=F32)
            dt = _softplus(dt_ref[0, pl.ds(r0, t), :] + biasr_ref[...])
            s["dtx"] = _dot2_lhs(dt, expand)
            at = _softplus(dtt_ref[:, pl.ds(r0, t)] + biasc_ref[...]) * a_c
            hi3, mid3, lo3 = _split3(at)
            s["ct3"] = jnp.dot(jnp.concatenate([hi3, mid3, lo3], axis=0), tt, preferred_element_type=F32)

        for s in st:
            ax = s["dtx"] * a_x
            s["csf"] = _dot2_rhs(tril, ax[:, :gx])
            s["csb"] = _dot2_rhs(triu, ax[:, gx:])
            sh = s["sh"]
            xbc = _silu(sh[:t] * conv_w[0:1] + s["cur"].astype(F32) * conv_w[1:2]
                        + sh[t:] * conv_w[2:3] + conv_b)
            s["xs"] = xbc[:, :gx]
            s["bm"] = xbc[:, gx:gx + SSD_STATE].astype(BF16)
            s["cm"] = xbc[:, gx + SSD_STATE:].astype(BF16)
            cm_s[pl.ds(s["o0"], t), :] = s["cm"]
            s["cb"] = _bdot_nt(s["cm"], s["bm"])

        for s in st:
            ct3 = s["ct3"]
            cst = ct3[0:n8] + ct3[n8:2 * n8] + ct3[2 * n8:3 * n8]
            xs, dtx, cb = s["xs"], s["dtx"], s["cb"]
            y = dskip_ref[0] * xs
            for d, cs_d in ((0, s["csf"]), (1, s["csb"])):
                mask = lower if d == 0 else (ri <= ci)
                xd = (xs * dtx[:, d * gx:(d + 1) * gx]).astype(BF16)
                ms, blocks = [], []
                for r in range(SSD_HEADS_PER_GROUP):
                    col = cs_d[:, r * SSD_HEAD_DIM:r * SSD_HEAD_DIM + 1]
                    k = d * SSD_HEADS_PER_GROUP + r
                    row = cst[k:k + 1, d * t:(d + 1) * t]
                    seg = jnp.exp2(jnp.where(mask, col - row, -jnp.inf))
                    ms.append((cb * seg).astype(BF16))
                    blocks.append(jnp.where(head_of_lane == r, xd, jnp.zeros_like(xd)))
                y = y + jnp.dot(jnp.concatenate(ms, axis=1), jnp.concatenate(blocks, axis=0),
                                preferred_element_type=F32)
            y_s[pl.ds(s["o0"], t), :] = y

        for s in st:
            xs, dtx, csf, csb = s["xs"], s["dtx"], s["csf"], s["csb"]
            endf = csf[t - 1:t, :]
            endb = csb[0:1, :]
            ex_s[pl.ds(s["o0"], t), :] = jnp.exp2(jnp.concatenate([csf, csb], axis=1))
            xdd = jnp.concatenate([xs * dtx[:, :gx] * jnp.exp2(endf - csf),
                                   xs * dtx[:, gx:] * jnp.exp2(endb - csb)], axis=1).astype(BF16)
            bt = jnp.transpose(s["bm"].astype(F32)).astype(BF16)
            st_s[s["c"] + chunk0] = jnp.dot(bt, xdd, preferred_element_type=F32)
            dec_s[pl.ds(s["c"] + chunk0, 1), :] = jnp.exp2(jnp.concatenate([endf, endb], axis=1))

    def run_phase_a(n_seg_chunks, chunk0, refs):
        group = math.gcd(n_seg_chunks, SSD_GROUP_CHUNKS)

        def body(i, carry):
            phase_a([i * group + k for k in range(group)], *refs, n_seg_chunks, chunk0)
            return carry

        lax.fori_loop(0, n_seg_chunks // group, body, 0)

    run_phase_a(nc_ctx, 0, (xc_ref, bc_ref, cc_ref, dtc_ref, dttc_ref))
    run_phase_a(nc_lat, nc_ctx, (xl_ref, bl_ref, cl_ref, dtl_ref, dttl_ref))

    order_f = list(range(n_chunks))
    order_b = list(range(nc_ctx - 1, -1, -1)) + list(range(n_chunks - 1, nc_ctx - 1, -1))
    for d, order in ((0, order_f), (1, order_b)):
        h = jnp.zeros((SSD_STATE, gx), F32)
        for c in order:
            hin_s[c, :, d * gx:(d + 1) * gx] = h.astype(BF16)
            h = dec_s[c:c + 1, d * gx:(d + 1) * gx] * h + st_s[c, :, d * gx:(d + 1) * gx]

    def phase_c(cs, out_ref, chunk0):
        offs = [pl.multiple_of((c + chunk0) * t, t) for c in cs]
        yos = [jnp.dot(cm_s[pl.ds(o0, t), :], hin_s[c + chunk0], preferred_element_type=F32)
               for c, o0 in zip(cs, offs)]
        for c, o0, yo in zip(cs, offs, yos):
            yo = yo * ex_s[pl.ds(o0, t), :]
            y = y_s[pl.ds(o0, t), :] + yo[:, :gx] + yo[:, gx:]
            out_ref[0, pl.ds(pl.multiple_of(c * t, t), t), :] = y.astype(BF16)

    def run_phase_c(n_seg_chunks, chunk0, out_ref):
        group = math.gcd(n_seg_chunks, SSD_GROUP_CHUNKS)

        def body(i, carry):
            phase_c([i * group + k for k in range(group)], out_ref, chunk0)
            return carry

        lax.fori_loop(0, n_seg_chunks // group, body, 0)

    run_phase_c(nc_ctx, 0, yc_ref)
    run_phase_c(nc_lat, nc_ctx, yl_ref)


def _ssd(ul3, uc3, cols_l, cols_c, dt_l, dt_c, dtt_l, dtt_c, p):
    bsz, n_lat, _ = ul3.shape
    n_ctx = uc3.shape[1]
    gx = GROUP_X
    n_chunks = (n_lat + n_ctx) // SSD_CHUNK
    n_tot = n_lat + n_ctx

    def seq_specs(cols, n):
        xb, bb, cb = cols["xs"] // gx, cols["b"] // SSD_STATE, cols["c"] // SSD_STATE
        return [
            pl.BlockSpec((1, n, gx), lambda b, g: (b, 0, xb + g)),
            pl.BlockSpec((1, n, SSD_STATE), lambda b, g: (b, 0, bb + g)),
            pl.BlockSpec((1, n, SSD_STATE), lambda b, g: (b, 0, cb + g)),
        ]

    xoff = 0
    boff = SSD_WIDTH // SSD_STATE
    coff = (SSD_WIDTH + SSD_BC_WIDTH) // SSD_STATE
    n8 = 2 * SSD_HEADS_PER_GROUP
    in_specs = (
        seq_specs(cols_l, n_lat) + seq_specs(cols_c, n_ctx) + [
            pl.BlockSpec((1, n_lat, LANES), lambda b, g: (b, 0, 0)),
            pl.BlockSpec((1, n_ctx, LANES), lambda b, g: (b, 0, 0)),
            pl.BlockSpec((n8, n_lat), lambda b, g: (g, b)),
            pl.BlockSpec((n8, n_ctx), lambda b, g: (g, b)),
            pl.BlockSpec((3, gx), lambda b, g: (0, xoff + g)),
            pl.BlockSpec((3, SSD_STATE), lambda b, g: (0, boff + g)),
            pl.BlockSpec((3, SSD_STATE), lambda b, g: (0, coff + g)),
            pl.BlockSpec((1, gx), lambda b, g: (0, xoff + g)),
            pl.BlockSpec((1, SSD_STATE), lambda b, g: (0, boff + g)),
            pl.BlockSpec((1, SSD_STATE), lambda b, g: (0, coff + g)),
            pl.BlockSpec((1, LANES), lambda b, g: (0, 0)),
            pl.BlockSpec((n8, 1), lambda b, g: (g, 0)),
            pl.BlockSpec((1, 1, 2 * gx), lambda b, g: (g, 0, 0)),
            pl.BlockSpec((n8, 1), lambda b, g: (g, 0)),
            pl.BlockSpec((1, 1, gx), lambda b, g: (g, 0, 0)),
        ])
    return pl.pallas_call(
        functools.partial(_ssd_kernel, n_lat=n_lat, n_ctx=n_ctx),
        grid=(bsz, SSD_GROUPS),
        in_specs=in_specs,
        out_specs=[
            pl.BlockSpec((1, n_lat, gx), lambda b, g: (b, 0, g)),
            pl.BlockSpec((1, n_ctx, gx), lambda b, g: (b, 0, g)),
        ],
        out_shape=[
            jax.ShapeDtypeStruct((bsz, n_lat, SSD_WIDTH), BF16),
            jax.ShapeDtypeStruct((bsz, n_ctx, SSD_WIDTH), BF16),
        ],
        scratch_shapes=[
            pltpu.VMEM((n_tot, SSD_STATE), BF16),
            pltpu.VMEM((n_tot, gx), F32),
            pltpu.VMEM((n_tot, 2 * gx), F32),
            pltpu.VMEM((n_chunks, SSD_STATE, 2 * gx), F32),
            pltpu.VMEM((n_chunks + (-n_chunks) % 8, 2 * gx), F32),
            pltpu.VMEM((n_chunks, SSD_STATE, 2 * gx), BF16),
        ],
        compiler_params=_cparams(("parallel", "parallel")),
        name="ssd_scan",
    )(ul3, ul3, ul3, uc3, uc3, uc3,
      dt_l.reshape(bsz, n_lat, LANES), dt_c.reshape(bsz, n_ctx, LANES), dtt_l, dtt_c,
      p["conv_w"], p["conv_w"], p["conv_w"], p["conv_b"], p["conv_b"], p["conv_b"],
      p["bias_row"], p["bias_col"], p["alog_x"], p["alog_col"], p["dskip_x"])


def _attn_kernel(*refs, n_ctx, n_lat, tq, lam_init, heads):
    if n_lat:
        (q_ref, kc_ref, vc_ref, kl_ref, vl_ref, cos_ref, sa_ref, sb_ref, lam_ref, sw_ref,
         o_ref, kl_s) = refs
    else:
        q_ref, kc_ref, vc_ref, lam_ref, sw_ref, o_ref = refs
    qi = pl.program_id(2)

    if n_lat:
        @pl.when(qi == 0)
        def _():
            kl_s[...] = _rope(kl_ref[0].astype(F32), cos_ref[...], sa_ref[...], sb_ref[...]).astype(BF16)

    lf = lam_ref[...]
    lam = (jnp.exp(jnp.sum(lf[0:1] * lf[1:2], axis=-1, keepdims=True))
           - jnp.exp(jnp.sum(lf[2:3] * lf[3:4], axis=-1, keepdims=True)) + lam_init)

    for hh in range(heads):
        cols = slice(hh * ATT_V_DIM, (hh + 1) * ATT_V_DIM)
        _attend_head(q_ref.at[0, :, cols], kc_ref.at[0, :, cols], vc_ref.at[0, :, cols],
                     kl_s if n_lat else None, vl_ref.at[0] if n_lat else None,
                     (cos_ref, sa_ref, sb_ref) if n_lat else None, lam, sw_ref, o_ref.at[0, :, cols],
                     qi=qi, n_ctx=n_ctx, n_lat=n_lat, tq=tq, lam_init=lam_init)


def _attend_head(q_ref, kc_ref, vc_ref, kl_s, vl_ref, tabs, lam, sw_ref, o_ref, *,
                 qi, n_ctx, n_lat, tq, lam_init):
    hd = ATT_HEAD_DIM
    q = q_ref[...].astype(F32)
    if n_lat:
        cos_ref, sa_ref, sb_ref = tabs
    if n_lat:
        r0 = pl.multiple_of(qi * tq, tq)
        q = _rope(q, cos_ref[pl.ds(r0, tq), :], sa_ref[pl.ds(r0, tq), :], sb_ref[pl.ds(r0, tq), :])
    q = (q * (ATT_SCALE * LOG2_E)).astype(BF16)
    lane = lax.broadcasted_iota(jnp.int32, (1, ATT_V_DIM), 1)
    q_comp = [jnp.where(lane < hd, q, jnp.zeros_like(q)), jnp.where(lane >= hd, q, jnp.zeros_like(q))]

    sub = min(ATT_SUB_ROWS, tq)
    chains = [(r, c) for r in range(tq // sub) for c in range(2)]
    nt = (((1,), (1,)), ((), ()))

    def scores(r, c):
        qc = q_comp[c][r * sub:(r + 1) * sub, :]
        s = lax.dot_general(qc, kc_ref[...], nt, preferred_element_type=F32)
        if n_lat:
            s = jnp.concatenate([s, lax.dot_general(qc, kl_s[...], nt, preferred_element_type=F32)],
                                axis=1)
        return s

    def weighted_values(e):
        pv = jnp.dot(e[:, :n_ctx], vc_ref[...], preferred_element_type=F32)
        if n_lat:
            pv = pv + jnp.dot(e[:, n_ctx:], vl_ref[...], preferred_element_type=F32)
        return pv

    ahead = ATT_CHAINS_AHEAD
    pending = [scores(*ch) for ch in chains[:ahead]]
    outs = {}
    for n, (r, c) in enumerate(chains):
        s = pending.pop(0)
        if n + ahead < len(chains):
            pending.append(scores(*chains[n + ahead]))
        e = jnp.exp2(s - jnp.max(s, axis=-1, keepdims=True))
        inv = 1.0 / jnp.sum(e, axis=-1, keepdims=True)
        outs[c] = weighted_values(e.astype(BF16)) * inv
        if c == 1:
            o = outs[0] - lam * outs[1]
            on = o * lax.rsqrt(jnp.mean(o * o, axis=-1, keepdims=True) + EPS) * sw_ref[...]
            o_ref[r * sub:(r + 1) * sub, :] = (on * (1.0 - lam_init)).astype(BF16)


def _attention(uq3, col_q, uc3, cols_c, ul3, cols_l, rope_tabs, lam_p, subln_w, *, lam_init, tq):
    bsz, n_q, _ = uq3.shape
    n_ctx = uc3.shape[1]
    n_lat = 0 if ul3 is None else ul3.shape[1]
    vd = ATT_V_DIM
    heads = 1 if n_lat else ATT_HEADS
    hw = heads * vd
    in_specs = [
        pl.BlockSpec((1, tq, hw), lambda b, h, i: (b, i, col_q // hw + h)),
        pl.BlockSpec((1, n_ctx, hw), lambda b, h, i: (b, 0, cols_c["k"] // hw + h)),
        pl.BlockSpec((1, n_ctx, hw), lambda b, h, i: (b, 0, cols_c["v"] // hw + h)),
    ]
    args = [uq3, uc3, uc3]
    if n_lat:
        in_specs += [
            pl.BlockSpec((1, n_lat, vd), lambda b, h, i: (b, 0, cols_l["k"] // vd + h)),
            pl.BlockSpec((1, n_lat, vd), lambda b, h, i: (b, 0, cols_l["v"] // vd + h)),
            pl.BlockSpec((n_lat, vd), lambda b, h, i: (0, 0)),
            pl.BlockSpec((n_lat, vd), lambda b, h, i: (0, 0)),
            pl.BlockSpec((n_lat, vd), lambda b, h, i: (0, 0)),
        ]
        args += [ul3, ul3, *rope_tabs]
    in_specs += [
        pl.BlockSpec((4, ATT_HEAD_DIM), lambda b, h, i: (0, 0)),
        pl.BlockSpec((1, vd), lambda b, h, i: (0, 0)),
    ]
    args += [lam_p, subln_w]
    return pl.pallas_call(
        functools.partial(_attn_kernel, n_ctx=n_ctx, n_lat=n_lat, tq=tq, lam_init=lam_init,
                          heads=heads),
        grid=(bsz, ATT_HEADS // heads, n_q // tq),
        in_specs=in_specs,
        out_specs=pl.BlockSpec((1, tq, hw), lambda b, h, i: (b, i, h)),
        out_shape=jax.ShapeDtypeStruct((bsz, n_q, ATT_WIDTH), BF16),
        scratch_shapes=[pltpu.VMEM((n_lat, vd), BF16)] if n_lat else [],
        compiler_params=_cparams(("parallel", "parallel", "arbitrary")),
        name="diff_attn",
    )(*args)


def _merge_kernel(ys_ref, z_ref, ya_ref, gs_ref, ga_ref, x_ref, g1_ref, nw_ref,
                  wbs_ref, wba_ref, wo_ref, o_ref):
    tm = x_ref.shape[0]
    n_p = MERGE_ROW_PIECES if tm % (MERGE_ROW_PIECES * BF16_ROWS) == 0 else 1
    rp = tm // n_p
    pieces = [slice(p * rp, (p + 1) * rp) for p in range(n_p)]
    ta = [jnp.dot(ya_ref[r, :], wba_ref[...], preferred_element_type=F32) for r in pieces]
    ts = []
    for r in pieces:
        yz = ys_ref[r, :].astype(F32) * _silu(z_ref[r, :].astype(F32))
        ysn = yz * lax.rsqrt(jnp.mean(yz * yz, axis=-1, keepdims=True) + EPS) * nw_ref[...]
        ts.append(jnp.dot(ysn.astype(BF16), wbs_ref[...], preferred_element_type=F32))
    for r, t_s, t_a in zip(pieces, ts, ta):
        tmix = _sigmoid(gs_ref[r, :].astype(F32)) * t_s + _sigmoid(ga_ref[r, :].astype(F32)) * t_a
        o_ref[r, :] = x_ref[r, :] + g1_ref[0] * jnp.dot(tmix.astype(BF16), wo_ref[...],
                                                         preferred_element_type=F32)


def _merge(ys2, u2, cols, ya2, x2d, mod3, mod_row, norm_w, w_bs, w_ba, w_o, *, layer, tm):
    rows = x2d.shape[0]
    d = D_MODEL
    row_blk = lambda cb: pl.BlockSpec((tm, d), lambda i: (i, cb))
    full = lambda shape: pl.BlockSpec(shape, lambda i: (0, 0))
    weight = pl.BlockSpec((None, d, d), lambda i: (layer, 0, 0))
    return pl.pallas_call(
        _merge_kernel,
        grid=(rows // tm,),
        in_specs=[
            row_blk(0), row_blk(cols["z"] // d), row_blk(0),
            row_blk(cols["gs"] // d), row_blk(cols["ga"] // d), row_blk(0),
            pl.BlockSpec((1, 1, d), lambda i: (mod_row(i), 0, 2)),
            full((1, d)), weight, weight, weight,
        ],
        out_specs=row_blk(0),
        out_shape=jax.ShapeDtypeStruct((rows, d), F32),
        compiler_params=_cparams(("parallel",)),
        name="branch_merge",
    )(ys2, u2, ya2, u2, u2, x2d, mod3, norm_w, w_bs, w_ba, w_o)


def _ffn_kernel(x_ref, xp_ref, xn_ref, sh_ref, sc_ref, g2_ref, nw_ref, wu_ref, cw_ref, cb_ref,
                wd_ref, fw_ref, o_ref, h_s, u_s, act_s, acc_s, *, tm, tf, tiles_per_seg, final_norm):
    i = pl.program_id(0)
    halo = BF16_ROWS
    seg_first = (i % tiles_per_seg) == 0
    seg_last = (i % tiles_per_seg) == tiles_per_seg - 1

    def norm_mod(x):
        r = lax.rsqrt(jnp.mean(x * x, axis=-1, keepdims=True) + EPS)
        return ((x * r * nw_ref[...]) * (1.0 + sc_ref[0]) + sh_ref[0]).astype(BF16)

    h_s[0:halo, :] = norm_mod(xp_ref[...])
    h_s[halo:halo + tm, :] = norm_mod(x_ref[...])
    h_s[halo + tm:2 * halo + tm, :] = norm_mod(xn_ref[...])

    def up_proj(j):
        us = u_s.at[j % 2]
        for half in range(2):
            c0 = half * D_FF + j * tf
            us[:, half * tf:(half + 1) * tf] = jnp.dot(h_s[...], wu_ref[:, c0:c0 + tf],
                                                       preferred_element_type=F32)
        us[halo - 1:halo, :] = jnp.where(seg_first, 0.0, us[halo - 1:halo, :])
        us[halo + tm:halo + tm + 1, :] = jnp.where(seg_last, 0.0, us[halo + tm:halo + tm + 1, :])

    n_f = D_FF // tf
    bounds = [0]
    while bounds[-1] < n_f:
        left = n_f - bounds[-1]
        bounds.append(bounds[-1] + (FFN_DOWN_GROUP if left > FFN_DOWN_GROUP + 2 else (left + 1) // 2))
    group_of = {j: g for g in range(len(bounds) - 1) for j in range(bounds[g], bounds[g + 1])}

    def down_proj(g):
        j0, j1 = bounds[g], bounds[g + 1]
        down = jnp.dot(act_s[g % 2, :, 0:(j1 - j0) * tf], wd_ref[j0 * tf:j1 * tf, :],
                       preferred_element_type=F32)
        if g == 0:
            acc_s[...] = down
        else:
            acc_s[...] += down

    up_proj(0)
    for j in range(n_f):
        if j + 1 < n_f:
            up_proj(j + 1)
        if j in bounds[1:]:
            down_proj(group_of[j] - 1)
        us = u_s.at[j % 2]
        w = jnp.concatenate([cw_ref[:, j * tf:(j + 1) * tf],
                             cw_ref[:, D_FF + j * tf:D_FF + (j + 1) * tf]], axis=1)
        b = jnp.concatenate([cb_ref[:, j * tf:(j + 1) * tf],
                             cb_ref[:, D_FF + j * tf:D_FF + (j + 1) * tf]], axis=1)
        uc = (us[halo - 1:halo - 1 + tm, :] * w[0:1] + us[halo:halo + tm, :] * w[1:2]
              + us[halo + 1:halo + 1 + tm, :] * w[2:3] + b)
        g, k = group_of[j], j - bounds[group_of[j]]
        act_s[g % 2, :, k * tf:(k + 1) * tf] = (_silu(uc[:, :tf]) * uc[:, tf:]).astype(BF16)
    down_proj(len(bounds) - 2)
    y = x_ref[...] + g2_ref[0] * acc_s[...]
    if final_norm:
        y = y * lax.rsqrt(jnp.mean(y * y, axis=-1, keepdims=True) + EPS) * fw_ref[...]
    o_ref[...] = y


def _conv_ffn(x2d, mod3, mod_row, norm_w, w_up, conv_w, conv_b, w_down, final_w, *,
              layer, tm, tf, seg_len, final_norm):
    rows = x2d.shape[0]
    d = D_MODEL
    halo = BF16_ROWS
    hb = tm // halo
    last_blk = rows // halo - 1
    assert seg_len % tm == 0
    resident = lambda shape: pl.BlockSpec(shape, lambda i: (0, 0), pipeline_mode=pl.Buffered(1))
    layered = lambda shape: pl.BlockSpec((None, *shape), lambda i: (layer, 0, 0),
                                         pipeline_mode=pl.Buffered(1))
    return pl.pallas_call(
        functools.partial(_ffn_kernel, tm=tm, tf=tf, tiles_per_seg=seg_len // tm, final_norm=final_norm),
        grid=(rows // tm,),
        in_specs=[
            pl.BlockSpec((tm, d), lambda i: (i, 0)),
            pl.BlockSpec((halo, d), lambda i: (jnp.maximum(i * hb - 1, 0), 0)),
            pl.BlockSpec((halo, d), lambda i: (jnp.minimum((i + 1) * hb, last_blk), 0)),
            pl.BlockSpec((1, 1, d), lambda i: (mod_row(i), 0, 3)),
            pl.BlockSpec((1, 1, d), lambda i: (mod_row(i), 0, 4)),
            pl.BlockSpec((1, 1, d), lambda i: (mod_row(i), 0, 5)),
            resident((1, d)),
            layered((d, 2 * D_FF)),
            layered((3, 2 * D_FF)),
            layered((1, 2 * D_FF)),
            layered((D_FF, d)),
            resident((1, d)),
        ],
        out_specs=pl.BlockSpec((tm, d), lambda i: (i, 0)),
        out_shape=jax.ShapeDtypeStruct((rows, d), F32),
        scratch_shapes=[
            pltpu.VMEM((tm + 2 * halo, d), BF16),
            pltpu.VMEM((2, tm + 2 * halo, 2 * tf), F32),
            pltpu.VMEM((2, tm, FFN_DOWN_GROUP * tf), BF16),
            pltpu.VMEM((tm, d), F32),
        ],
        compiler_params=_cparams(("parallel",)),
        name="conv_ffn",
    )(x2d, x2d, x2d, mod3, mod3, mod3, norm_w, w_up, conv_w, conv_b, w_down, final_w)


def _split_in_weight(w):
    dt0 = SSD_WIDTH + SSD_CONV_CH
    dt = w[..., dt0:dt0 + N_DT]
    main = jnp.concatenate([w[..., :dt0], w[..., dt0 + N_DT:]], axis=-1).astype(BF16)
    return main, _dt_param_order(dt.reshape(*dt.shape[:-1], 2, SSD_HEADS))


def _dt_param_order(p2h):
    lead = p2h.shape[:-2]
    p = p2h.reshape(*lead, 2, SSD_GROUPS, SSD_HEADS_PER_GROUP)
    return jnp.swapaxes(p, -3, -2).reshape(*lead, N_DT)


def _rope_tables(n_tokens):
    rows = n_tokens // GRID_W
    inv_freq = ROPE_BASE ** (-jnp.arange(ROPE_PAIRS, dtype=F32) / ROPE_PAIRS)
    ang_r = jnp.broadcast_to(jnp.arange(rows, dtype=F32)[:, None, None] * inv_freq, (rows, GRID_W, ROPE_PAIRS))
    ang_c = jnp.broadcast_to(jnp.arange(GRID_W, dtype=F32)[None, :, None] * inv_freq, (rows, GRID_W, ROPE_PAIRS))
    ang = jnp.stack([ang_r, ang_c], axis=2).reshape(n_tokens, 2, 1, ROPE_PAIRS)
    cos = jnp.broadcast_to(jnp.cos(ang), (n_tokens, 2, 2, ROPE_PAIRS))
    sin = jnp.broadcast_to(jnp.sin(ang), (n_tokens, 2, 2, ROPE_PAIRS))
    zero = jnp.zeros_like(sin[:, :, :1])
    sin_a = jnp.concatenate([-sin[:, :, :1], zero], axis=2)
    sin_b = jnp.concatenate([zero, sin[:, :, 1:]], axis=2)
    tile = lambda a: jnp.tile(a.reshape(n_tokens, ATT_HEAD_DIM), (1, 2))
    return tile(cos), tile(sin_a), tile(sin_b)


def kernel(x, c, ctx, c_ctx, w_mod, b_mod, norm1_w, w_in, ssd_conv_w, ssd_conv_b, ssd_a_log,
           ssd_dt_bias, ssd_d, ssd_norm_w, diff_lambda, att_subln_w, w_br_ssd, w_br_att, w_out,
           norm2_w, w_up, ffn_conv_w, ffn_conv_b, w_down, final_norm_w):
    bsz, n_lat, d = x.shape
    n_ctx = ctx.shape[1]
    depth = w_mod.shape[0]
    assert d == D_MODEL and bsz + 1 <= 16
    ctx_row = bsz

    tm_in = min(1024, n_lat)
    tm_in_c = min(1024, bsz * n_ctx)
    tn_in = 4096
    tn_in_c_last = 2048
    tm_merge = min(512, n_lat)
    tm_merge_c = min(512, bsz * n_ctx)
    tm_ffn = min(512, n_lat)
    tf = 256
    tq = min(2048, n_lat)

    cc = jnp.zeros((16, d), F32).at[:bsz].set(c).at[ctx_row].set(c_ctx)
    mod = _modulation(cc, w_mod, b_mod)

    rope_tabs = _rope_tables(n_lat)
    cols_full = dict(z=COL_Z, q=COL_Q, gs=COL_GS, ga=COL_GA, xs=COL_XS, b=COL_B, c=COL_C, k=COL_K, v=COL_V)

    xl = x.reshape(bsz * n_lat, d)
    xc = ctx.reshape(bsz * n_ctx, d)
    lat_row = lambda tm: (lambda i: (i * tm) // n_lat)
    ctx_rowf = lambda i: ctx_row

    w_main, w_dt32 = _split_in_weight(w_in)
    w_dt = jnp.pad(w_dt32, ((0, 0), (0, 0), (0, LANES - N_DT))).astype(BF16)
    w_dtt = jnp.swapaxes(w_dt32, 1, 2).astype(BF16)
    w_bs = w_br_ssd.astype(BF16)
    w_ba = w_br_att.astype(BF16)
    w_o = w_out.astype(BF16)
    w_u = w_up.astype(BF16)
    w_d = w_down.astype(BF16)
    cb = ffn_conv_b.reshape(depth, 1, 2 * D_FF)

    for li in range(depth):
        with_ctx = li < depth - 1
        mod3 = mod[li].reshape(16, 1, N_MOD * d)
        n1 = norm1_w[li].reshape(1, d)

        ul, dt_l, dtt_l = _in_proj(xl, mod3, lat_row(tm_in), n1, w_main, w_dt, w_dtt,
                                   layer=li, tm=tm_in, tn=tn_in, n_out=N_COLS)
        uc, dt_c, dtt_c = _in_proj(xc, mod3, ctx_rowf, n1, w_main, w_dt, w_dtt, layer=li, tm=tm_in_c,
                                   tn=tn_in if with_ctx else tn_in_c_last,
                                   n_out=N_COLS if with_ctx else COL_CTX_LAST_END)
        cols_c = cols_full
        ul3 = ul.reshape(bsz, n_lat, -1)
        uc3 = uc.reshape(bsz, n_ctx, -1)

        ssd_p = dict(
            conv_w=ssd_conv_w[li], conv_b=ssd_conv_b[li].reshape(1, SSD_CONV_CH),
            bias_row=jnp.zeros((1, LANES), F32).at[0, :N_DT].set(_dt_param_order(ssd_dt_bias[li])),
            bias_col=_dt_param_order(ssd_dt_bias[li]).reshape(N_DT, 1),
            alog_col=_dt_param_order(ssd_a_log[li]).reshape(N_DT, 1),
            alog_x=jnp.repeat(ssd_a_log[li].reshape(2, SSD_GROUPS, SSD_HEADS_PER_GROUP).transpose(1, 0, 2)
                              .reshape(SSD_GROUPS, 2 * SSD_HEADS_PER_GROUP), SSD_HEAD_DIM, axis=1)
            .reshape(SSD_GROUPS, 1, 2 * GROUP_X),
            dskip_x=jnp.repeat(ssd_d[li], SSD_HEAD_DIM).reshape(SSD_GROUPS, 1, GROUP_X),
        )
        ys_l, ys_c = _ssd(ul3, uc3, cols_full, cols_c, dt_l, dt_c, dtt_l, dtt_c, ssd_p)

        lam_init = 0.8 - 0.6 * math.exp(-0.3 * li)
        sw = att_subln_w[li].reshape(1, ATT_V_DIM)
        ya_l = _attention(ul3, COL_Q, uc3, cols_c, ul3, cols_full, rope_tabs, diff_lambda[li], sw,
                          lam_init=lam_init, tq=tq)

        sn = ssd_norm_w[li].reshape(1, SSD_WIDTH)
        n2 = norm2_w[li].reshape(1, d)
        fw = final_norm_w.reshape(1, d)

        if with_ctx:
            ya_c = _attention(uc3, cols_c["q"], uc3, cols_c, None, None, None, diff_lambda[li], sw,
                              lam_init=lam_init, tq=n_ctx)
            xc = _merge(ys_c.reshape(-1, SSD_WIDTH), uc, cols_c, ya_c.reshape(-1, ATT_WIDTH), xc, mod3,
                        ctx_rowf, sn, w_bs, w_ba, w_o, layer=li, tm=tm_merge_c)
            xc = _conv_ffn(xc, mod3, ctx_rowf, n2, w_u, ffn_conv_w, cb, w_d, fw, layer=li, tm=n_ctx,
                           tf=tf, seg_len=n_ctx, final_norm=False)

        xl = _merge(ys_l.reshape(-1, SSD_WIDTH), ul, cols_full, ya_l.reshape(-1, ATT_WIDTH), xl, mod3,
                    lat_row(tm_merge), sn, w_bs, w_ba, w_o, layer=li, tm=tm_merge)
        xl = _conv_ffn(xl, mod3, lat_row(tm_ffn), n2, w_u, ffn_conv_w, cb, w_d, fw, layer=li,
                       tm=tm_ffn, tf=tf, seg_len=n_lat, final_norm=not with_ctx)

    return xl.reshape(bsz, n_lat, d)
```
